```python
import math
import jax, jax.numpy as jnp
from jax import lax
import numpy as np

D_MODEL = 4096
BATCH = 2
SEQ = 4096
DEPTH = 1
DEC_BATCH = 4
DEC_SEQ = 4096
PAST_LEN = 128

N_META = 16
GRID_W = 64
Q_BLOCK = 128
HEAD_DIM = 128
MLA_HEADS = 16
MLA_Q_LORA = 1024
MLA_KV_LORA = 512
MLA_NOPE_DIM = 128
MLA_ROPE_DIM = 64
MLA_V_DIM = 128
GQA_HEADS = 16
GQA_KV_HEADS = 4
GQA_GROUP = GQA_HEADS // GQA_KV_HEADS
D_FF = 11008
CONV_WIDTH = 3
ROPE_THETA = 10000.0
NORM_EPS = 1e-6
IN_SPLITS = (MLA_Q_LORA, MLA_KV_LORA, MLA_ROPE_DIM, GQA_HEADS * HEAD_DIM, GQA_KV_HEADS * HEAD_DIM, GQA_KV_HEADS * HEAD_DIM, D_MODEL, D_MODEL)
D_IN = sum(IN_SPLITS)

kernel_name = "hybrid_mla_axial_gqa_encoder"


def _rms(x, g):
    x32 = x.astype(jnp.float32)
    y = x32 * lax.rsqrt(jnp.mean(x32 * x32, axis=-1, keepdims=True) + NORM_EPS)
    return (y * g.astype(jnp.float32)).astype(x.dtype)


def _rope_tables(pos, dim):
    inv = ROPE_THETA ** (-jnp.arange(0, dim, 2, dtype=jnp.float32) / dim)
    ang = pos.astype(jnp.float32)[:, None] * inv[None, :]
    return jnp.cos(ang), jnp.sin(ang)


def _apply_rope(x, cos, sin):
    half = x.shape[-1] // 2
    c = cos[None, :, None, :].astype(x.dtype)
    s = sin[None, :, None, :].astype(x.dtype)
    x1, x2 = x[..., :half], x[..., half:]
    return jnp.concatenate([x1 * c - x2 * s, x1 * s + x2 * c], axis=-1)


def _axial_rope(x, row_cos, row_sin, col_cos, col_sin):
    half = x.shape[-1] // 2
    return jnp.concatenate([_apply_rope(x[..., :half], row_cos, row_sin),
                            _apply_rope(x[..., half:], col_cos, col_sin)], axis=-1)


def _attend_block(q, k, v, scale):
    s = jnp.einsum("bqkgd,bskd->bkgqs", q, k).astype(jnp.float32) * scale
    p = jax.nn.softmax(s, axis=-1).astype(v.dtype)
    return jnp.einsum("bkgqs,bskd->bqkgd", p, v)


def _bidir_attention(q, k, v, scale):
    b, l, hk, g, d = q.shape
    n_real = l - N_META
    n_blk = n_real // Q_BLOCK
    o_meta = _attend_block(q[:, :N_META], k, v, scale)
    q_blocks = jnp.moveaxis(q[:, N_META:].reshape(b, n_blk, Q_BLOCK, hk, g, d), 1, 0)
    o_blocks = lax.map(lambda qb: _attend_block(qb, k, v, scale), q_blocks)
    o_real = jnp.moveaxis(o_blocks, 0, 1).reshape(b, n_real, hk, g, v.shape[-1])
    return jnp.concatenate([o_meta, o_real], axis=1)


def _dwconv3(a, w, bias):
    ap = jnp.pad(a, ((0, 0), (1, 1), (0, 0)))
    return ap[:, :-2] * w[0] + ap[:, 1:-1] * w[1] + ap[:, 2:] * w[2] + bias


def _layer(h, rope1, axial, g_pre_mix, w_in, g_cq, w_uq, g_ckv, w_ukv, g_qn, g_kn,
           w_pa, w_pb, w_o, g_post_mix, g_pre_ffn, w_up, w_conv, b_conv, w_down, g_post_ffn):
    b, l, _ = h.shape
    cos1, sin1 = rope1
    u = _rms(h, g_pre_mix)
    z = u @ w_in
    offsets = np.cumsum(IN_SPLITS)[:-1].tolist()
    z_cq, z_ckv, z_kr, z_gq, z_gk, z_gv, z_ga, z_gb = jnp.split(z, offsets, axis=-1)

    cq = _rms(z_cq, g_cq)
    qa = (cq @ w_uq).reshape(b, l, MLA_HEADS, MLA_NOPE_DIM + MLA_ROPE_DIM)
    qa = jnp.concatenate([qa[..., :MLA_NOPE_DIM], _apply_rope(qa[..., MLA_NOPE_DIM:], cos1, sin1)], axis=-1)
    ckv = _rms(z_ckv, g_ckv)
    kv = (ckv @ w_ukv).reshape(b, l, MLA_HEADS, MLA_NOPE_DIM + MLA_V_DIM)
    k_nope, v_a = kv[..., :MLA_NOPE_DIM], kv[..., MLA_NOPE_DIM:]
    k_rope = _apply_rope(z_kr[:, :, None, :], cos1, sin1)
    ka = jnp.concatenate([k_nope, jnp.broadcast_to(k_rope, (b, l, MLA_HEADS, MLA_ROPE_DIM))], axis=-1)
    o_a = _bidir_attention(qa[:, :, :, None, :], ka, v_a, 1.0 / math.sqrt(MLA_NOPE_DIM + MLA_ROPE_DIM))
    o_a = o_a.reshape(b, l, MLA_HEADS * MLA_V_DIM)

    qb = _axial_rope(_rms(z_gq.reshape(b, l, GQA_HEADS, HEAD_DIM), g_qn), *axial)
    kb = _axial_rope(_rms(z_gk.reshape(b, l, GQA_KV_HEADS, HEAD_DIM), g_kn), *axial)
    vb = z_gv.reshape(b, l, GQA_KV_HEADS, HEAD_DIM)
    qb = qb.reshape(b, l, GQA_KV_HEADS, GQA_GROUP, HEAD_DIM)
    o_b = _bidir_attention(qb, kb, vb, 1.0 / math.sqrt(HEAD_DIM)).reshape(b, l, GQA_HEADS * HEAD_DIM)

    merged = jax.nn.sigmoid(z_ga) * (o_a @ w_pa) + jax.nn.sigmoid(z_gb) * (o_b @ w_pb)
    h = h + _rms(merged @ w_o, g_post_mix)

    up = _rms(h, g_pre_ffn) @ w_up
    a, gv = up[..., :D_FF], up[..., D_FF:]
    f = jax.nn.gelu(_dwconv3(a, w_conv, b_conv)) * gv
    return h + _rms(f @ w_down, g_post_ffn)


def _trunk(x, meta_tokens, params):
    b, n_tok, _ = x.shape
    rows = n_tok // GRID_W
    h = jnp.concatenate([jnp.broadcast_to(meta_tokens.astype(x.dtype)[None], (b, N_META, D_MODEL)), x], axis=1)
    l = N_META + n_tok
    rope1 = _rope_tables(jnp.arange(l), MLA_ROPE_DIM)
    row_ids = jnp.concatenate([jnp.zeros((N_META,), jnp.int32), jnp.repeat(jnp.arange(rows, dtype=jnp.int32), GRID_W)])
    col_ids = jnp.concatenate([jnp.zeros((N_META,), jnp.int32), jnp.tile(jnp.arange(GRID_W, dtype=jnp.int32), rows)])
    row_cos, row_sin = _rope_tables(row_ids, HEAD_DIM // 2)
    col_cos, col_sin = _rope_tables(col_ids, HEAD_DIM // 2)
    axial = (row_cos, row_sin, col_cos, col_sin)
    for i in range(DEPTH):
        h = _layer(h, rope1, axial, *[p[i] for p in params])
    return h[:, N_META:]


def setup_inputs(seed: int = 0) -> dict:
    key = jax.random.key(seed)
    ks = jax.random.split(key, 24)
    f32 = jnp.float32

    def nrm(k, shape, scale):
        return jax.random.normal(k, shape, f32) * scale

    def gain(k, n):
        return 1.0 + 0.02 * jax.random.normal(k, (DEPTH, n), f32)

    return {
        "x_prompt": nrm(ks[0], (BATCH, SEQ, D_MODEL), 1.0),
        "x_sample": nrm(ks[1], (DEC_BATCH, DEC_SEQ, D_MODEL), 1.0),
        "meta_tokens": nrm(ks[2], (N_META, D_MODEL), 1.0),
        "g_pre_mix": gain(ks[3], D_MODEL),
        "w_in": nrm(ks[4], (DEPTH, D_MODEL, D_IN), D_MODEL ** -0.5),
        "g_cq": gain(ks[5], MLA_Q_LORA),
        "w_uq": nrm(ks[6], (DEPTH, MLA_Q_LORA, MLA_HEADS * (MLA_NOPE_DIM + MLA_ROPE_DIM)), MLA_Q_LORA ** -0.5),
        "g_ckv": gain(ks[7], MLA_KV_LORA),
        "w_ukv": nrm(ks[8], (DEPTH, MLA_KV_LORA, MLA_HEADS * (MLA_NOPE_DIM + MLA_V_DIM)), MLA_KV_LORA ** -0.5),
        "g_qn": gain(ks[9], HEAD_DIM),
        "g_kn": gain(ks[10], HEAD_DIM),
        "w_pa": nrm(ks[11], (DEPTH, MLA_HEADS * MLA_V_DIM, D_MODEL), (MLA_HEADS * MLA_V_DIM) ** -0.5),
        "w_pb": nrm(ks[12], (DEPTH, GQA_HEADS * HEAD_DIM, D_MODEL), (GQA_HEADS * HEAD_DIM) ** -0.5),
        "w_o": nrm(ks[13], (DEPTH, D_MODEL, D_MODEL), D_MODEL ** -0.5),
        "g_post_mix": gain(ks[14], D_MODEL),
        "g_pre_ffn": gain(ks[15], D_MODEL),
        "w_up": nrm(ks[16], (DEPTH, D_MODEL, 2 * D_FF), D_MODEL ** -0.5),
        "w_conv": nrm(ks[17], (DEPTH, CONV_WIDTH, D_FF), CONV_WIDTH ** -0.5),
        "b_conv": nrm(ks[18], (DEPTH, D_FF), 0.01),
        "w_down": nrm(ks[19], (DEPTH, D_FF, D_MODEL), D_FF ** -0.5),
        "g_post_ffn": gain(ks[20], D_MODEL),
    }


def reference(x_prompt, x_sample, meta_tokens, g_pre_mix, w_in, g_cq, w_uq, g_ckv, w_ukv, g_qn, g_kn,
              w_pa, w_pb, w_o, g_post_mix, g_pre_ffn, w_up, w_conv, b_conv, w_down, g_post_ffn):
    params = (g_pre_mix, w_in, g_cq, w_uq, g_ckv, w_ukv, g_qn, g_kn, w_pa, w_pb, w_o,
              g_post_mix, g_pre_ffn, w_up, w_conv, b_conv, w_down, g_post_ffn)
    y_prompt = _trunk(x_prompt, meta_tokens, params)
    y_sample = _trunk(x_sample, meta_tokens, params)
    return (y_prompt, y_sample)
```

```python
import functools
import math

import jax
import jax.numpy as jnp
import numpy as np
from jax import lax
from jax.experimental import pallas as pl
from jax.experimental.pallas import tpu as pltpu

F32 = jnp.float32
BF16 = jnp.bfloat16

D_MODEL = 4096
SEQ = 4096
N_META = 16
GRID_W = 64
HEAD_DIM = 128
MLA_HEADS = 16
MLA_Q_LORA = 1024
MLA_KV_LORA = 512
MLA_NOPE = 128
MLA_ROPE = 64
MLA_V = 128
MLA_QK = 256
GQA_HEADS = 16
GQA_KV_HEADS = 4
GQA_GROUP = GQA_HEADS // GQA_KV_HEADS
D_FF = 11008
D_FF_PAD = 11264
ROPE_THETA = 10000.0
EPS = 1e-6

OFF_GA = 0
OFF_GB = 4096
OFF_GQ = 8192
OFF_CQ = 10240
OFF_CKV = 11264
OFF_GK = 11776
OFF_GV = 12288
OFF_KR = 12800
D_IN_PAD = 13312

VMEM_LIMIT = 60 * 1024 * 1024


def _params(sem):
    return pltpu.CompilerParams(dimension_semantics=sem, vmem_limit_bytes=VMEM_LIMIT)


def _rms(x, g):
    return x * lax.rsqrt(jnp.mean(x * x, axis=-1, keepdims=True) + EPS) * g


def _norm_kernel(x_ref, g_ref, o_ref):
    o_ref[...] = _rms(x_ref[...], g_ref[...]).astype(o_ref.dtype)


def _norm_rows(x, g, tm, name):
    m = x.shape[0]
    return pl.pallas_call(
        _norm_kernel,
        grid=(m // tm,),
        in_specs=[pl.BlockSpec((tm, D_MODEL), lambda i: (i, 0)), pl.BlockSpec((1, D_MODEL), lambda i: (0, 0))],
        out_specs=pl.BlockSpec((tm, D_MODEL), lambda i: (i, 0)),
        out_shape=jax.ShapeDtypeStruct((m, D_MODEL), BF16),
        compiler_params=_params(("parallel",)),
        name=name,
    )(x, g)


def _mm_kernel(a_ref, b_ref, o_ref):
    o_ref[...] = jnp.dot(a_ref[...], b_ref[...], preferred_element_type=F32).astype(o_ref.dtype)


def _matmul(a, b, tm, tn, out_dtype, name):
    m, k = a.shape
    n = b.shape[1]
    return pl.pallas_call(
        _mm_kernel,
        grid=(m // tm, n // tn),
        in_specs=[pl.BlockSpec((tm, k), lambda i, j: (i, 0)), pl.BlockSpec((k, tn), lambda i, j: (0, j))],
        out_specs=pl.BlockSpec((tm, tn), lambda i, j: (i, j)),
        out_shape=jax.ShapeDtypeStruct((m, n), out_dtype),
        compiler_params=_params(("parallel", "parallel")),
        name=name,
    )(a, b)


def _rope(x, cos, sin_signed):
    return x * cos + pltpu.roll(x, 64, 1) * sin_signed


def _heads_kernel(cq_ref, ckv_ref, kr_ref, gq_ref, gk_ref, wuq_ref, wuk_ref, wuv_ref,
                  gcq_ref, gckv_ref, gqn_ref, gkn_ref, cosa_ref, sina_ref, cosb_ref, sinb_ref,
                  qa_ref, ka_ref, va_ref, qb_ref, kb_ref):
    cosa, sina = cosa_ref[...], sina_ref[...]
    cosb, sinb = cosb_ref[...], sinb_ref[...]
    scale_a = 1.0 / math.sqrt(MLA_NOPE + MLA_ROPE)
    scale_b = 1.0 / math.sqrt(HEAD_DIM)

    cq = _rms(cq_ref[...].astype(F32), gcq_ref[...]).astype(BF16)
    for h in range(MLA_HEADS):
        q = jnp.dot(cq, wuq_ref[:, h * MLA_QK:(h + 1) * MLA_QK], preferred_element_type=F32)
        qa_ref[:, h * MLA_QK:h * MLA_QK + MLA_NOPE] = (q[:, :MLA_NOPE] * scale_a).astype(BF16)
        qa_ref[:, h * MLA_QK + MLA_NOPE:(h + 1) * MLA_QK] = (
            _rope(q[:, MLA_NOPE:], cosa, sina) * scale_a).astype(BF16)

    ckv = _rms(ckv_ref[...].astype(F32), gckv_ref[...]).astype(BF16)
    k_rope = _rope(kr_ref[...].astype(F32), cosa, sina).astype(BF16)
    k_nope = jnp.dot(ckv, wuk_ref[...], preferred_element_type=F32).astype(BF16)
    for h in range(MLA_HEADS):
        ka_ref[:, h * MLA_QK:h * MLA_QK + MLA_NOPE] = k_nope[:, h * MLA_NOPE:(h + 1) * MLA_NOPE]
        ka_ref[:, h * MLA_QK + MLA_NOPE:(h + 1) * MLA_QK] = k_rope
    va_ref[...] = jnp.dot(ckv, wuv_ref[...], preferred_element_type=F32).astype(BF16)

    gqn, gkn = gqn_ref[...], gkn_ref[...]
    for h in range(GQA_HEADS):
        x = gq_ref[:, h * HEAD_DIM:(h + 1) * HEAD_DIM].astype(F32)
        qb_ref[:, h * HEAD_DIM:(h + 1) * HEAD_DIM] = (_rope(_rms(x, gqn), cosb, sinb) * scale_b).astype(BF16)
    for h in range(GQA_KV_HEADS):
        x = gk_ref[:, h * HEAD_DIM:(h + 1) * HEAD_DIM].astype(F32)
        kb_ref[:, h * HEAD_DIM:(h + 1) * HEAD_DIM] = _rope(_rms(x, gkn), cosb, sinb).astype(BF16)


def _heads(z, w, tabs, tm, tab_blocks, name):
    m = z.shape[0]

    def zspec(width, off):
        return pl.BlockSpec((tm, width), lambda i: (i, off // width))

    def full(a):
        return pl.BlockSpec(a.shape, lambda i: (0,) * a.ndim)

    tab_spec = pl.BlockSpec((tm, 128), lambda i: (i % tab_blocks, 0))
    out_widths = (MLA_HEADS * MLA_QK, MLA_HEADS * MLA_QK, MLA_HEADS * MLA_V, GQA_HEADS * HEAD_DIM,
                  GQA_KV_HEADS * HEAD_DIM)
    consts = (w["w_uq"], w["w_uk"], w["w_uv"], w["g_cq"], w["g_ckv"], w["g_qn"], w["g_kn"])
    return pl.pallas_call(
        _heads_kernel,
        grid=(m // tm,),
        in_specs=[zspec(MLA_Q_LORA, OFF_CQ), zspec(MLA_KV_LORA, OFF_CKV), zspec(128, OFF_KR),
                  zspec(GQA_HEADS * HEAD_DIM, OFF_GQ), zspec(GQA_KV_HEADS * HEAD_DIM, OFF_GK)]
        + [full(a) for a in consts] + [tab_spec] * 4,
        out_specs=[pl.BlockSpec((tm, n), lambda i: (i, 0)) for n in out_widths],
        out_shape=[jax.ShapeDtypeStruct((m, n), BF16) for n in out_widths],
        compiler_params=_params(("parallel",)),
        name=name,
    )(z, z, z, z, z, *consts, *tabs)


def _attn_kernel(tk, q_ref, k_ref, v_ref, km_ref, vm_ref, o_ref):
    nt = (((1,), (1,)), ((), ()))
    q = q_ref[...]
    s = lax.dot_general(q, km_ref[...], nt, preferred_element_type=F32)
    m = jnp.max(s, axis=-1, keepdims=True)
    p = jnp.exp(s - m)
    l = jnp.sum(p, axis=-1, keepdims=True)
    acc = jnp.dot(p.astype(BF16), vm_ref[...], preferred_element_type=F32)
    for c in range(k_ref.shape[0] // tk):
        s = lax.dot_general(q, k_ref[c * tk:(c + 1) * tk, :], nt, preferred_element_type=F32)
        m_new = jnp.maximum(m, jnp.max(s, axis=-1, keepdims=True))
        alpha = jnp.exp(m - m_new)
        p = jnp.exp(s - m_new)
        l = alpha * l + jnp.sum(p, axis=-1, keepdims=True)
        acc = alpha * acc + jnp.dot(p.astype(BF16), v_ref[c * tk:(c + 1) * tk, :], preferred_element_type=F32)
        m = m_new
    o_ref[...] = (acc / l).astype(o_ref.dtype)


def _attention(q, k, v, km, vm, *, batch, heads, group, dk, v_col0, lq, tq, tk, name):
    nq = lq // tq
    dv = HEAD_DIM
    return pl.pallas_call(
        functools.partial(_attn_kernel, tk),
        grid=(batch, heads, nq),
        in_specs=[
            pl.BlockSpec((tq, dk), lambda b, h, i: (b * nq + i, h)),
            pl.BlockSpec((SEQ, dk), lambda b, h, i: (b, h // group)),
            pl.BlockSpec((SEQ, dv), lambda b, h, i: (b, v_col0 + h // group)),
            pl.BlockSpec((N_META, dk), lambda b, h, i: (b, h // group)),
            pl.BlockSpec((N_META, dv), lambda b, h, i: (b, v_col0 + h // group)),
        ],
        out_specs=pl.BlockSpec((tq, dv), lambda b, h, i: (b * nq + i, h)),
        out_shape=jax.ShapeDtypeStruct((batch * lq, heads * dv), BF16),
        compiler_params=_params(("parallel", "parallel", "parallel")),
        name=name,
    )(q, k, v, km, vm)


def _merge_kernel(oa_ref, ob_ref, ga_ref, gb_ref, wpa_ref, wpb_ref, o_ref):
    a = jnp.dot(oa_ref[...], wpa_ref[...], preferred_element_type=F32)
    b = jnp.dot(ob_ref[...], wpb_ref[...], preferred_element_type=F32)
    ga = jax.nn.sigmoid(ga_ref[...].astype(F32))
    gb = jax.nn.sigmoid(gb_ref[...].astype(F32))
    o_ref[...] = (ga * a + gb * b).astype(o_ref.dtype)


def _merge(oa, ob, z, wpa, wpb, tm, tn, name):
    m, k = oa.shape
    return pl.pallas_call(
        _merge_kernel,
        grid=(m // tm, D_MODEL // tn),
        in_specs=[
            pl.BlockSpec((tm, k), lambda i, j: (i, 0)),
            pl.BlockSpec((tm, k), lambda i, j: (i, 0)),
            pl.BlockSpec((tm, tn), lambda i, j: (i, OFF_GA // tn + j)),
            pl.BlockSpec((tm, tn), lambda i, j: (i, OFF_GB // tn + j)),
            pl.BlockSpec((k, tn), lambda i, j: (0, j)),
            pl.BlockSpec((k, tn), lambda i, j: (0, j)),
        ],
        out_specs=pl.BlockSpec((tm, tn), lambda i, j: (i, j)),
        out_shape=jax.ShapeDtypeStruct((m, D_MODEL), BF16),
        compiler_params=_params(("parallel", "parallel")),
        name=name,
    )(oa, ob, z, z, wpa, wpb)


PROJ_CHUNK = 512
NORM_ROWS = 128


def _proj_residual_kernel(nk, a_ref, w_ref, res_ref, g_ref, g2_ref, h_ref, hn_ref):
    kk = pl.program_id(1)
    a = a_ref[...]
    for c in range(0, D_MODEL, PROJ_CHUNK):
        part = jnp.dot(a, w_ref[:, c:c + PROJ_CHUNK], preferred_element_type=F32)

        @pl.when(kk == 0)
        def _():
            h_ref[:, c:c + PROJ_CHUNK] = part

        @pl.when(kk > 0)
        def _():
            h_ref[:, c:c + PROJ_CHUNK] += part

    @pl.when(kk == nk - 1)
    def _():
        for r in range(0, h_ref.shape[0], NORM_ROWS):
            rows = slice(r, min(r + NORM_ROWS, h_ref.shape[0]))
            h = res_ref[rows, :] + _rms(h_ref[rows, :], g_ref[...])
            h_ref[rows, :] = h
            if hn_ref is not None:
                hn_ref[rows, :] = _rms(h, g2_ref[...]).astype(hn_ref.dtype)


def _proj_residual_nohn_kernel(nk, a_ref, w_ref, res_ref, g_ref, h_ref):
    _proj_residual_kernel(nk, a_ref, w_ref, res_ref, g_ref, None, h_ref, None)


def _proj_residual(a, w, res, g, g2, tm, tk, name):
    m, k = a.shape
    nk = k // tk
    gspec = pl.BlockSpec((1, D_MODEL), lambda i, kk: (0, 0))
    row_spec = pl.BlockSpec((tm, D_MODEL), lambda i, kk: (i, 0))
    in_specs = [pl.BlockSpec((tm, tk), lambda i, kk: (i, kk)),
                pl.BlockSpec((tk, D_MODEL), lambda i, kk: (kk, 0)), row_spec, gspec]
    if g2 is None:
        return pl.pallas_call(
            functools.partial(_proj_residual_nohn_kernel, nk),
            grid=(m // tm, nk), in_specs=in_specs, out_specs=row_spec,
            out_shape=jax.ShapeDtypeStruct((m, D_MODEL), F32),
            compiler_params=_params(("parallel", "arbitrary")), name=name,
        )(a, w, res, g)
    return pl.pallas_call(
        functools.partial(_proj_residual_kernel, nk),
        grid=(m // tm, nk), in_specs=in_specs + [gspec], out_specs=[row_spec, row_spec],
        out_shape=[jax.ShapeDtypeStruct((m, D_MODEL), F32), jax.ShapeDtypeStruct((m, D_MODEL), BF16)],
        compiler_params=_params(("parallel", "arbitrary")), name=name,
    )(a, w, res, g, g2)


HALO = 16


def _ffn_up_kernel(tm, tiles_per_seq, hn_ref, prev_ref, next_ref, am_ref, wa_ref, wg_ref, wc_ref, bc_ref,
                   f_ref, ext_ref, a_ref):
    i = pl.program_id(0)

    @pl.when(pl.program_id(1) == 0)
    def _():
        ext_ref[0:HALO, :] = prev_ref[...]
        ext_ref[HALO:HALO + tm, :] = hn_ref[...]
        ext_ref[HALO + tm:, :] = next_ref[...]

    a_ref[...] = jnp.dot(ext_ref[...], wa_ref[...], preferred_element_type=F32)

    @pl.when(i % tiles_per_seq == 0)
    def _():
        a_ref[HALO - 1:HALO, :] = am_ref[0:1, :]

    @pl.when(i % tiles_per_seq == tiles_per_seq - 1)
    def _():
        a_ref[HALO + tm:HALO + tm + 1, :] = jnp.zeros((1, a_ref.shape[1]), F32)

    wc = wc_ref[...]
    conv = (a_ref[HALO - 1:HALO - 1 + tm, :] * wc[0:1, :] + a_ref[HALO:HALO + tm, :] * wc[1:2, :]
            + a_ref[HALO + 1:HALO + 1 + tm, :] * wc[2:3, :] + bc_ref[...])
    gate = jnp.dot(ext_ref[HALO:HALO + tm, :], wg_ref[...], preferred_element_type=F32)
    f_ref[...] = (jax.nn.gelu(conv) * gate).astype(f_ref.dtype)


def _ffn_up(hn, a_meta, wa, wg, wc, bc, tm, tf, name):
    m = hn.shape[0]
    tiles_per_seq = SEQ // tm
    hb = tm // HALO
    last = m // HALO - 1
    return pl.pallas_call(
        functools.partial(_ffn_up_kernel, tm, tiles_per_seq),
        grid=(m // tm, D_FF_PAD // tf),
        in_specs=[
            pl.BlockSpec((tm, D_MODEL), lambda i, j: (i, 0)),
            pl.BlockSpec((HALO, D_MODEL), lambda i, j: (jnp.maximum(i * hb - 1, 0), 0)),
            pl.BlockSpec((HALO, D_MODEL), lambda i, j: (jnp.minimum((i + 1) * hb, last), 0)),
            pl.BlockSpec((8, tf), lambda i, j: (i // tiles_per_seq, j)),
            pl.BlockSpec((D_MODEL, tf), lambda i, j: (0, j)),
            pl.BlockSpec((D_MODEL, tf), lambda i, j: (0, j)),
            pl.BlockSpec((3, tf), lambda i, j: (0, j)),
            pl.BlockSpec((1, tf), lambda i, j: (0, j)),
        ],
        out_specs=pl.BlockSpec((tm, tf), lambda i, j: (i, j)),
        out_shape=jax.ShapeDtypeStruct((m, D_FF_PAD), BF16),
        scratch_shapes=[pltpu.VMEM((tm + 2 * HALO, D_MODEL), BF16), pltpu.VMEM((tm + 2 * HALO, tf), F32)],
        compiler_params=_params(("parallel", "arbitrary")),
        name=name,
    )(hn, hn, hn, a_meta, wa, wg, wc, bc)


def _prep_weights(g_pre_mix, w_in, g_cq, w_uq, g_ckv, w_ukv, g_qn, g_kn, w_pa, w_pb, w_o,
                  g_post_mix, g_pre_ffn, w_up, w_conv, b_conv, w_down, g_post_ffn):
    perm_b = np.concatenate([np.arange(0, 32), np.arange(64, 96), np.arange(32, 64), np.arange(96, 128)])
    o = np.cumsum((0, MLA_Q_LORA, MLA_KV_LORA, MLA_ROPE, GQA_HEADS * HEAD_DIM, GQA_KV_HEADS * HEAD_DIM,
                   GQA_KV_HEADS * HEAD_DIM, D_MODEL, D_MODEL))
    w = w_in[0]
    cq, ckv, kr, gq, gk, gv, ga, gb = (w[:, o[i]:o[i + 1]] for i in range(8))
    gq = gq.reshape(D_MODEL, GQA_HEADS, HEAD_DIM)[:, :, perm_b].reshape(D_MODEL, -1)
    gk = gk.reshape(D_MODEL, GQA_KV_HEADS, HEAD_DIM)[:, :, perm_b].reshape(D_MODEL, -1)
    z32 = jnp.zeros((D_MODEL, 32), w.dtype)
    kr = jnp.concatenate([kr[:, :32], z32, kr[:, 32:], z32], axis=1)
    tail = jnp.zeros((D_MODEL, D_IN_PAD - OFF_KR - 128), w.dtype)
    w_in_p = jnp.concatenate([ga, gb, gq, cq, ckv, gk, gv, kr, tail], axis=1).astype(BF16)

    uq = w_uq[0].reshape(MLA_Q_LORA, MLA_HEADS, MLA_NOPE + MLA_ROPE)
    zq = jnp.zeros((MLA_Q_LORA, MLA_HEADS, 32), uq.dtype)
    uq = jnp.concatenate([uq[:, :, :MLA_NOPE], uq[:, :, MLA_NOPE:MLA_NOPE + 32], zq,
                          uq[:, :, MLA_NOPE + 32:], zq], axis=2).reshape(MLA_Q_LORA, -1).astype(BF16)
    ukv = w_ukv[0].reshape(MLA_KV_LORA, MLA_HEADS, MLA_NOPE + MLA_V)
    uk = ukv[:, :, :MLA_NOPE].reshape(MLA_KV_LORA, -1).astype(BF16)
    uv = ukv[:, :, MLA_NOPE:].reshape(MLA_KV_LORA, -1).astype(BF16)

    fpad = D_FF_PAD - D_FF
    up = w_up[0]
    return dict(
        g_pre_mix=g_pre_mix, w_in=w_in_p, g_cq=g_cq, w_uq=uq, g_ckv=g_ckv, w_uk=uk, w_uv=uv,
        g_qn=g_qn[:, perm_b], g_kn=g_kn[:, perm_b],
        w_pa=w_pa[0].astype(BF16), w_pb=w_pb[0].astype(BF16), w_o=w_o[0].astype(BF16),
        g_post_mix=g_post_mix, g_pre_ffn=g_pre_ffn,
        w_up_a=jnp.pad(up[:, :D_FF], ((0, 0), (0, fpad))).astype(BF16),
        w_up_g=jnp.pad(up[:, D_FF:], ((0, 0), (0, fpad))).astype(BF16),
        w_conv=jnp.pad(w_conv[0], ((0, 0), (0, fpad))), b_conv=jnp.pad(b_conv, ((0, 0), (0, fpad))),
        w_down=jnp.pad(w_down[0], ((0, fpad), (0, 0))).astype(BF16), g_post_ffn=g_post_ffn,
    )


def _rope_tables(pos, dim):
    inv = ROPE_THETA ** (-jnp.arange(0, dim, 2, dtype=F32) / dim)
    ang = pos.astype(F32)[:, None] * inv[None, :]
    return jnp.cos(ang), jnp.sin(ang)


def _tables(pos, row, col):
    c1, s1 = _rope_tables(pos, MLA_ROPE)
    z = jnp.zeros_like(c1)
    cos_a = jnp.concatenate([c1, z, c1, z], axis=1)
    sin_a = jnp.concatenate([-s1, z, s1, z], axis=1)
    cr, sr = _rope_tables(row, HEAD_DIM // 2)
    cc, sc = _rope_tables(col, HEAD_DIM // 2)
    cos_b = jnp.concatenate([cr, cc, cr, cc], axis=1)
    sin_b = jnp.concatenate([-sr, -sc, sr, sc], axis=1)
    return cos_a, sin_a, cos_b, sin_b


def kernel(x_prompt, x_sample, meta_tokens, g_pre_mix, w_in, g_cq, w_uq, g_ckv, w_ukv, g_qn, g_kn,
           w_pa, w_pb, w_o, g_post_mix, g_pre_ffn, w_up, w_conv, b_conv, w_down, g_post_ffn):
    w = _prep_weights(g_pre_mix, w_in, g_cq, w_uq, g_ckv, w_ukv, g_qn, g_kn, w_pa, w_pb, w_o,
                      g_post_mix, g_pre_ffn, w_up, w_conv, b_conv, w_down, g_post_ffn)
    t = jnp.arange(SEQ, dtype=jnp.int32)
    tabs_real = _tables(t + N_META, t // GRID_W, t % GRID_W)
    zero = jnp.zeros((N_META,), jnp.int32)
    tabs_meta = _tables(jnp.arange(N_META), zero, zero)
    return (_trunk(x_prompt, meta_tokens, w, tabs_real, tabs_meta, "p"),
            _trunk(x_sample, meta_tokens, w, tabs_real, tabs_meta, "s"))


def _trunk(x, meta_tokens, w, tabs_real, tabs_meta, tag):
    batch = x.shape[0]
    xr = x.reshape(batch * SEQ, D_MODEL)
    xm = jnp.broadcast_to(meta_tokens[None], (batch, N_META, D_MODEL)).reshape(batch * N_META, D_MODEL)
    mrows = batch * N_META
    tabs_meta = tuple(jnp.tile(a, (batch, 1)) for a in tabs_meta)

    z = _matmul(_norm_rows(xr, w["g_pre_mix"], 256, "norm_" + tag), w["w_in"], 1024, 1024, BF16,
                "in_proj_" + tag)
    zm = _matmul(_norm_rows(xm, w["g_pre_mix"], mrows, "norm_meta_" + tag), w["w_in"], mrows, 1024, BF16,
                 "in_proj_meta_" + tag)
    qa, ka, va, qb, kb = _heads(z, w, tabs_real, 256, SEQ // 256, "heads_" + tag)
    qam, kam, vam, qbm, kbm = _heads(zm, w, tabs_meta, mrows, 1, "heads_meta_" + tag)

    cfg_a = dict(batch=batch, heads=MLA_HEADS, group=1, dk=MLA_QK, v_col0=0)
    cfg_b = dict(batch=batch, heads=GQA_HEADS, group=GQA_GROUP, dk=HEAD_DIM, v_col0=OFF_GV // HEAD_DIM)
    oa = _attention(qa, ka, va, kam, vam, lq=SEQ, tq=512, tk=1024, name="attn_a_" + tag, **cfg_a)
    ob = _attention(qb, kb, z, kbm, zm, lq=SEQ, tq=512, tk=1024, name="attn_b_" + tag, **cfg_b)
    oam = _attention(qam, ka, va, kam, vam, lq=N_META, tq=N_META, tk=1024, name="attn_a_meta_" + tag, **cfg_a)
    obm = _attention(qbm, kb, z, kbm, zm, lq=N_META, tq=N_META, tk=1024, name="attn_b_meta_" + tag, **cfg_b)

    mg = _merge(oa, ob, z, w["w_pa"], w["w_pb"], 1024, 1024, "merge_" + tag)
    mgm = _merge(oam, obm, zm, w["w_pa"], w["w_pb"], mrows, 1024, "merge_meta_" + tag)
    h1, hn = _proj_residual(mg, w["w_o"], xr, w["g_post_mix"], w["g_pre_ffn"], 512, 512, "out_proj_" + tag)
    _, hnm = _proj_residual(mgm, w["w_o"], xm, w["g_post_mix"], w["g_pre_ffn"], mrows, 512,
                            "out_proj_meta_" + tag)

    am = _matmul(hnm, w["w_up_a"], mrows, 1024, F32, "ffn_up_meta_" + tag)
    am_last = am.reshape(batch, N_META, D_FF_PAD)[:, N_META - 1:, :]
    am_last = jnp.pad(am_last, ((0, 0), (0, 7), (0, 0))).reshape(batch * 8, D_FF_PAD)
    f = _ffn_up(hn, am_last, w["w_up_a"], w["w_up_g"], w["w_conv"], w["b_conv"], 1024, 512, "ffn_up_" + tag)
    y = _proj_residual(f, w["w_down"], h1, w["g_post_ffn"], None, 512, 1024, "ffn_down_" + tag)
    return y.reshape(batch, SEQ, D_MODEL)
```

```python
import functools
import math

import jax
import jax.numpy as jnp
import numpy as np
from jax import lax
from jax.experimental import pallas as pl
from jax.experimental.pallas import tpu as pltpu

F32 = jnp.float32
BF16 = jnp.bfloat16

D_MODEL = 4096
SEQ = 4096
N_META = 16
GRID_W = 64
HEAD_DIM = 128
MLA_HEADS = 16
MLA_Q_LORA = 1024
MLA_KV_LORA = 512
MLA_NOPE = 128
MLA_ROPE = 64
MLA_V = 128
MLA_QK = 256
GQA_HEADS = 16
GQA_KV_HEADS = 4
GQA_GROUP = GQA_HEADS // GQA_KV_HEADS
D_FF = 11008
D_FF_PAD = 11264
ROPE_THETA = 10000.0
EPS = 1e-6
LOG2E = math.log2(math.e)

OFF_GA = 0
OFF_GB = 4096
OFF_GQ = 8192
OFF_CQ = 10240
OFF_CKV = 11264
OFF_GK = 11776
OFF_GV = 12288
OFF_KR = 12800
D_IN_PAD = 13312

VMEM_LIMIT = 60 * 1024 * 1024


def _params(sem):
    return pltpu.CompilerParams(dimension_semantics=sem, vmem_limit_bytes=VMEM_LIMIT)


def _rms(x, g):
    return x * lax.rsqrt(jnp.mean(x * x, axis=-1, keepdims=True) + EPS) * g


def _norm_kernel(x_ref, g_ref, o_ref):
    o_ref[...] = _rms(x_ref[...], g_ref[...]).astype(o_ref.dtype)


def _norm_rows(x, g, tm, name):
    m = x.shape[0]
    return pl.pallas_call(
        _norm_kernel,
        grid=(m // tm,),
        in_specs=[pl.BlockSpec((tm, D_MODEL), lambda i: (i, 0)), pl.BlockSpec((1, D_MODEL), lambda i: (0, 0))],
        out_specs=pl.BlockSpec((tm, D_MODEL), lambda i: (i, 0)),
        out_shape=jax.ShapeDtypeStruct((m, D_MODEL), BF16),
        compiler_params=_params(("parallel",)),
        name=name,
    )(x, g)


def _mm_kernel(a_ref, b_ref, o_ref):
    o_ref[...] = jnp.dot(a_ref[...], b_ref[...], preferred_element_type=F32).astype(o_ref.dtype)


def _matmul(a, b, tm, tn, out_dtype, name):
    m, k = a.shape
    n = b.shape[1]
    return pl.pallas_call(
        _mm_kernel,
        grid=(m // tm, n // tn),
        in_specs=[pl.BlockSpec((tm, k), lambda i, j: (i, 0)), pl.BlockSpec((k, tn), lambda i, j: (0, j))],
        out_specs=pl.BlockSpec((tm, tn), lambda i, j: (i, j)),
        out_shape=jax.ShapeDtypeStruct((m, n), out_dtype),
        compiler_params=_params(("parallel", "parallel")),
        name=name,
    )(a, b)


def _rope(x, cos, sin_signed):
    return x * cos + pltpu.roll(x, 64, 1) * sin_signed


def _heads_kernel(cq_ref, ckv_ref, kr_ref, gq_ref, gk_ref, wuq_ref, wuk_ref, wuv_ref,
                  gcq_ref, gckv_ref, gqn_ref, gkn_ref, cosa_ref, sina_ref, cosb_ref, sinb_ref,
                  qa_ref, ka_ref, va_ref, qb_ref, kb_ref):
    cosa, sina = cosa_ref[...], sina_ref[...]
    cosb, sinb = cosb_ref[...], sinb_ref[...]
    scale_a = LOG2E / math.sqrt(MLA_NOPE + MLA_ROPE)
    scale_b = LOG2E / math.sqrt(HEAD_DIM)

    cq = _rms(cq_ref[...].astype(F32), gcq_ref[...]).astype(BF16)
    for h in range(MLA_HEADS):
        q = jnp.dot(cq, wuq_ref[:, h * MLA_QK:(h + 1) * MLA_QK], preferred_element_type=F32)
        qa_ref[:, h * MLA_QK:h * MLA_QK + MLA_NOPE] = (q[:, :MLA_NOPE] * scale_a).astype(BF16)
        qa_ref[:, h * MLA_QK + MLA_NOPE:(h + 1) * MLA_QK] = (
            _rope(q[:, MLA_NOPE:], cosa, sina) * scale_a).astype(BF16)

    ckv = _rms(ckv_ref[...].astype(F32), gckv_ref[...]).astype(BF16)
    k_rope = _rope(kr_ref[...].astype(F32), cosa, sina).astype(BF16)
    k_nope = jnp.dot(ckv, wuk_ref[...], preferred_element_type=F32).astype(BF16)
    for h in range(MLA_HEADS):
        ka_ref[:, h * MLA_QK:h * MLA_QK + MLA_NOPE] = k_nope[:, h * MLA_NOPE:(h + 1) * MLA_NOPE]
        ka_ref[:, h * MLA_QK + MLA_NOPE:(h + 1) * MLA_QK] = k_rope
    va_ref[...] = jnp.dot(ckv, wuv_ref[...], preferred_element_type=F32).astype(BF16)

    gqn, gkn = gqn_ref[...], gkn_ref[...]
    for h in range(GQA_HEADS):
        x = gq_ref[:, h * HEAD_DIM:(h + 1) * HEAD_DIM].astype(F32)
        qb_ref[:, h * HEAD_DIM:(h + 1) * HEAD_DIM] = (_rope(_rms(x, gqn), cosb, sinb) * scale_b).astype(BF16)
    for h in range(GQA_KV_HEADS):
        x = gk_ref[:, h * HEAD_DIM:(h + 1) * HEAD_DIM].astype(F32)
        kb_ref[:, h * HEAD_DIM:(h + 1) * HEAD_DIM] = _rope(_rms(x, gkn), cosb, sinb).astype(BF16)


def _heads(z, w, tabs, tm, tab_blocks, name):
    m = z.shape[0]

    def zspec(width, off):
        return pl.BlockSpec((tm, width), lambda i: (i, off // width))

    def full(a):
        return pl.BlockSpec(a.shape, lambda i: (0,) * a.ndim)

    tab_spec = pl.BlockSpec((tm, 128), lambda i: (i % tab_blocks, 0))
    out_widths = (MLA_HEADS * MLA_QK, MLA_HEADS * MLA_QK, MLA_HEADS * MLA_V, GQA_HEADS * HEAD_DIM,
                  GQA_KV_HEADS * HEAD_DIM)
    consts = (w["w_uq"], w["w_uk"], w["w_uv"], w["g_cq"], w["g_ckv"], w["g_qn"], w["g_kn"])
    return pl.pallas_call(
        _heads_kernel,
        grid=(m // tm,),
        in_specs=[zspec(MLA_Q_LORA, OFF_CQ), zspec(MLA_KV_LORA, OFF_CKV), zspec(128, OFF_KR),
                  zspec(GQA_HEADS * HEAD_DIM, OFF_GQ), zspec(GQA_KV_HEADS * HEAD_DIM, OFF_GK)]
        + [full(a) for a in consts] + [tab_spec] * 4,
        out_specs=[pl.BlockSpec((tm, n), lambda i: (i, 0)) for n in out_widths],
        out_shape=[jax.ShapeDtypeStruct((m, n), BF16) for n in out_widths],
        compiler_params=_params(("parallel",)),
        name=name,
    )(z, z, z, z, z, *consts, *tabs)


def _attn_kernel(tk, q_ref, k_ref, v_ref, km_ref, vm_ref, o_ref):
    nt = (((1,), (1,)), ((), ()))
    q = q_ref[...]
    s = lax.dot_general(q, km_ref[...], nt, preferred_element_type=F32)
    m = jnp.max(s, axis=-1, keepdims=True)
    p = jnp.exp2(s - m)
    l = jnp.sum(p, axis=-1, keepdims=True)
    acc = jnp.dot(p.astype(BF16), vm_ref[...], preferred_element_type=F32)
    for c in range(k_ref.shape[0] // tk):
        s = lax.dot_general(q, k_ref[c * tk:(c + 1) * tk, :], nt, preferred_element_type=F32)
        m_new = jnp.maximum(m, jnp.max(s, axis=-1, keepdims=True))
        alpha = jnp.exp2(m - m_new)
        p = jnp.exp2(s - m_new)
        l = alpha * l + jnp.sum(p, axis=-1, keepdims=True)
        acc = alpha * acc + jnp.dot(p.astype(BF16), v_ref[c * tk:(c + 1) * tk, :], preferred_element_type=F32)
        m = m_new
    o_ref[...] = (acc / l).astype(o_ref.dtype)


def _attention(q, k, v, km, vm, *, batch, heads, group, dk, v_col0, lq, tq, tk, name):
    nq = lq // tq
    dv = HEAD_DIM
    return pl.pallas_call(
        functools.partial(_attn_kernel, tk),
        grid=(batch, heads, nq),
        in_specs=[
            pl.BlockSpec((tq, dk), lambda b, h, i: (b * nq + i, h)),
            pl.BlockSpec((SEQ, dk), lambda b, h, i: (b, h // group)),
            pl.BlockSpec((SEQ, dv), lambda b, h, i: (b, v_col0 + h // group)),
            pl.BlockSpec((N_META, dk), lambda b, h, i: (b, h // group)),
            pl.BlockSpec((N_META, dv), lambda b, h, i: (b, v_col0 + h // group)),
        ],
        out_specs=pl.BlockSpec((tq, dv), lambda b, h, i: (b * nq + i, h)),
        out_shape=jax.ShapeDtypeStruct((batch * lq, heads * dv), BF16),
        compiler_params=_params(("parallel", "parallel", "parallel")),
        name=name,
    )(q, k, v, km, vm)


def _merge_kernel(oa_ref, ob_ref, ga_ref, gb_ref, wpa_ref, wpb_ref, o_ref):
    a = jnp.dot(oa_ref[...], wpa_ref[...], preferred_element_type=F32)
    b = jnp.dot(ob_ref[...], wpb_ref[...], preferred_element_type=F32)
    ga = jax.nn.sigmoid(ga_ref[...].astype(F32))
    gb = jax.nn.sigmoid(gb_ref[...].astype(F32))
    o_ref[...] = (ga * a + gb * b).astype(o_ref.dtype)


def _merge(oa, ob, z, wpa, wpb, tm, tn, name):
    m, k = oa.shape
    return pl.pallas_call(
        _merge_kernel,
        grid=(m // tm, D_MODEL // tn),
        in_specs=[
            pl.BlockSpec((tm, k), lambda i, j: (i, 0)),
            pl.BlockSpec((tm, k), lambda i, j: (i, 0)),
            pl.BlockSpec((tm, tn), lambda i, j: (i, OFF_GA // tn + j)),
            pl.BlockSpec((tm, tn), lambda i, j: (i, OFF_GB // tn + j)),
            pl.BlockSpec((k, tn), lambda i, j: (0, j)),
            pl.BlockSpec((k, tn), lambda i, j: (0, j)),
        ],
        out_specs=pl.BlockSpec((tm, tn), lambda i, j: (i, j)),
        out_shape=jax.ShapeDtypeStruct((m, D_MODEL), BF16),
        compiler_params=_params(("parallel", "parallel")),
        name=name,
    )(oa, ob, z, z, wpa, wpb)


PROJ_CHUNK = 512
NORM_ROWS = 128


def _proj_residual_kernel(nk, a_ref, w_ref, res_ref, g_ref, g2_ref, h_ref, hn_ref):
    kk = pl.program_id(1)

    def accumulate(first):
        a = a_ref[...]
        for c in range(0, D_MODEL, PROJ_CHUNK):
            part = jnp.dot(a, w_ref[:, c:c + PROJ_CHUNK], preferred_element_type=F32)
            if first:
                h_ref[:, c:c + PROJ_CHUNK] = part
            else:
                h_ref[:, c:c + PROJ_CHUNK] += part

    pl.when(kk == 0)(functools.partial(accumulate, True))
    pl.when(kk > 0)(functools.partial(accumulate, False))

    @pl.when(kk == nk - 1)
    def _():
        for r in range(0, h_ref.shape[0], NORM_ROWS):
            rows = slice(r, min(r + NORM_ROWS, h_ref.shape[0]))
            h = res_ref[rows, :] + _rms(h_ref[rows, :], g_ref[...])
            h_ref[rows, :] = h
            if hn_ref is not None:
                hn_ref[rows, :] = _rms(h, g2_ref[...]).astype(hn_ref.dtype)


def _proj_residual_nohn_kernel(nk, a_ref, w_ref, res_ref, g_ref, h_ref):
    _proj_residual_kernel(nk, a_ref, w_ref, res_ref, g_ref, None, h_ref, None)


def _proj_residual(a, w, res, g, g2, tm, tk, name):
    m, k = a.shape
    nk = k // tk
    gspec = pl.BlockSpec((1, D_MODEL), lambda i, kk: (0, 0))
    row_spec = pl.BlockSpec((tm, D_MODEL), lambda i, kk: (i, 0))
    in_specs = [pl.BlockSpec((tm, tk), lambda i, kk: (i, kk)),
                pl.BlockSpec((tk, D_MODEL), lambda i, kk: (kk, 0)), row_spec, gspec]
    if g2 is None:
        return pl.pallas_call(
            functools.partial(_proj_residual_nohn_kernel, nk),
            grid=(m // tm, nk), in_specs=in_specs, out_specs=row_spec,
            out_shape=jax.ShapeDtypeStruct((m, D_MODEL), F32),
            compiler_params=_params(("parallel", "arbitrary")), name=name,
        )(a, w, res, g)
    return pl.pallas_call(
        functools.partial(_proj_residual_kernel, nk),
        grid=(m // tm, nk), in_specs=in_specs + [gspec], out_specs=[row_spec, row_spec],
        out_shape=[jax.ShapeDtypeStruct((m, D_MODEL), F32), jax.ShapeDtypeStruct((m, D_MODEL), BF16)],
        compiler_params=_params(("parallel", "arbitrary")), name=name,
    )(a, w, res, g, g2)


HALO = 16


def _ffn_up_kernel(tm, tiles_per_seq, hn_ref, prev_ref, next_ref, am_ref, wa_ref, wg_ref, wc_ref, bc_ref,
                   f_ref, ext_ref, a_ref):
    i = pl.program_id(0)

    @pl.when(pl.program_id(1) == 0)
    def _():
        ext_ref[0:HALO, :] = prev_ref[...]
        ext_ref[HALO:HALO + tm, :] = hn_ref[...]
        ext_ref[HALO + tm:, :] = next_ref[...]

    gate = jnp.dot(ext_ref[HALO:HALO + tm, :], wg_ref[...], preferred_element_type=F32)
    a_ref[...] = jnp.dot(ext_ref[...], wa_ref[...], preferred_element_type=F32)

    row = lax.broadcasted_iota(jnp.int32, (8, a_ref.shape[1]), 0)
    first = i % tiles_per_seq == 0
    last = i % tiles_per_seq == tiles_per_seq - 1
    a_ref[HALO - 8:HALO, :] = jnp.where(jnp.logical_and(row == 7, first), am_ref[0:1, :], a_ref[HALO - 8:HALO, :])
    a_ref[HALO + tm:HALO + tm + 8, :] = jnp.where(jnp.logical_and(row == 0, last), 0.0,
                                                  a_ref[HALO + tm:HALO + tm + 8, :])

    wc = wc_ref[...]
    conv = (a_ref[HALO - 1:HALO - 1 + tm, :] * wc[0:1, :] + a_ref[HALO:HALO + tm, :] * wc[1:2, :]
            + a_ref[HALO + 1:HALO + 1 + tm, :] * wc[2:3, :] + bc_ref[...])
    f_ref[...] = (jax.nn.gelu(conv) * gate).astype(f_ref.dtype)


def _ffn_up(hn, a_meta, wa, wg, wc, bc, tm, tf, name):
    m = hn.shape[0]
    tiles_per_seq = SEQ // tm
    hb = tm // HALO
    last = m // HALO - 1
    return pl.pallas_call(
        functools.partial(_ffn_up_kernel, tm, tiles_per_seq),
        grid=(m // tm, D_FF_PAD // tf),
        in_specs=[
            pl.BlockSpec((tm, D_MODEL), lambda i, j: (i, 0)),
            pl.BlockSpec((HALO, D_MODEL), lambda i, j: (jnp.maximum(i * hb - 1, 0), 0)),
            pl.BlockSpec((HALO, D_MODEL), lambda i, j: (jnp.minimum((i + 1) * hb, last), 0)),
            pl.BlockSpec((8, tf), lambda i, j: (i // tiles_per_seq, j)),
            pl.BlockSpec((D_MODEL, tf), lambda i, j: (0, j)),
            pl.BlockSpec((D_MODEL, tf), lambda i, j: (0, j)),
            pl.BlockSpec((3, tf), lambda i, j: (0, j)),
            pl.BlockSpec((1, tf), lambda i, j: (0, j)),
        ],
        out_specs=pl.BlockSpec((tm, tf), lambda i, j: (i, j)),
        out_shape=jax.ShapeDtypeStruct((m, D_FF_PAD), BF16),
        scratch_shapes=[pltpu.VMEM((tm + 2 * HALO, D_MODEL), BF16), pltpu.VMEM((tm + 2 * HALO, tf), F32)],
        compiler_params=_params(("parallel", "arbitrary")),
        name=name,
    )(hn, hn, hn, a_meta, wa, wg, wc, bc)


def _prep_weights(g_pre_mix, w_in, g_cq, w_uq, g_ckv, w_ukv, g_qn, g_kn, w_pa, w_pb, w_o,
                  g_post_mix, g_pre_ffn, w_up, w_conv, b_conv, w_down, g_post_ffn):
    perm_b = np.concatenate([np.arange(0, 32), np.arange(64, 96), np.arange(32, 64), np.arange(96, 128)])
    o = np.cumsum((0, MLA_Q_LORA, MLA_KV_LORA, MLA_ROPE, GQA_HEADS * HEAD_DIM, GQA_KV_HEADS * HEAD_DIM,
                   GQA_KV_HEADS * HEAD_DIM, D_MODEL, D_MODEL))
    w = w_in[0]
    cq, ckv, kr, gq, gk, gv, ga, gb = (w[:, o[i]:o[i + 1]] for i in range(8))
    gq = gq.reshape(D_MODEL, GQA_HEADS, HEAD_DIM)[:, :, perm_b].reshape(D_MODEL, -1)
    gk = gk.reshape(D_MODEL, GQA_KV_HEADS, HEAD_DIM)[:, :, perm_b].reshape(D_MODEL, -1)
    z32 = jnp.zeros((D_MODEL, 32), w.dtype)
    kr = jnp.concatenate([kr[:, :32], z32, kr[:, 32:], z32], axis=1)
    tail = jnp.zeros((D_MODEL, D_IN_PAD - OFF_KR - 128), w.dtype)
    w_in_p = jnp.concatenate([ga, gb, gq, cq, ckv, gk, gv, kr, tail], axis=1).astype(BF16)

    uq = w_uq[0].reshape(MLA_Q_LORA, MLA_HEADS, MLA_NOPE + MLA_ROPE)
    zq = jnp.zeros((MLA_Q_LORA, MLA_HEADS, 32), uq.dtype)
    uq = jnp.concatenate([uq[:, :, :MLA_NOPE], uq[:, :, MLA_NOPE:MLA_NOPE + 32], zq,
                          uq[:, :, MLA_NOPE + 32:], zq], axis=2).reshape(MLA_Q_LORA, -1).astype(BF16)
    ukv = w_ukv[0].reshape(MLA_KV_LORA, MLA_HEADS, MLA_NOPE + MLA_V)
    uk = ukv[:, :, :MLA_NOPE].reshape(MLA_KV_LORA, -1).astype(BF16)
    uv = ukv[:, :, MLA_NOPE:].reshape(MLA_KV_LORA, -1).astype(BF16)

    fpad = D_FF_PAD - D_FF
    up = w_up[0]
    return dict(
        g_pre_mix=g_pre_mix, w_in=w_in_p, g_cq=g_cq, w_uq=uq, g_ckv=g_ckv, w_uk=uk, w_uv=uv,
        g_qn=g_qn[:, perm_b], g_kn=g_kn[:, perm_b],
        w_pa=w_pa[0].astype(BF16), w_pb=w_pb[0].astype(BF16), w_o=w_o[0].astype(BF16),
        g_post_mix=g_post_mix, g_pre_ffn=g_pre_ffn,
        w_up_a=jnp.pad(up[:, :D_FF], ((0, 0), (0, fpad))).astype(BF16),
        w_up_g=jnp.pad(up[:, D_FF:], ((0, 0), (0, fpad))).astype(BF16),
        w_conv=jnp.pad(w_conv[0], ((0, 0), (0, fpad))), b_conv=jnp.pad(b_conv, ((0, 0), (0, fpad))),
        w_down=jnp.pad(w_down[0], ((0, fpad), (0, 0))).astype(BF16), g_post_ffn=g_post_ffn,
    )


def _rope_tables(pos, dim):
    inv = ROPE_THETA ** (-jnp.arange(0, dim, 2, dtype=F32) / dim)
    ang = pos.astype(F32)[:, None] * inv[None, :]
    return jnp.cos(ang), jnp.sin(ang)


def _tables(pos, row, col):
    c1, s1 = _rope_tables(pos, MLA_ROPE)
    z = jnp.zeros_like(c1)
    cos_a = jnp.concatenate([c1, z, c1, z], axis=1)
    sin_a = jnp.concatenate([-s1, z, s1, z], axis=1)
    cr, sr = _rope_tables(row, HEAD_DIM // 2)
    cc, sc = _rope_tables(col, HEAD_DIM // 2)
    cos_b = jnp.concatenate([cr, cc, cr, cc], axis=1)
    sin_b = jnp.concatenate([-sr, -sc, sr, sc], axis=1)
    return cos_a, sin_a, cos_b, sin_b


def kernel(x_prompt, x_sample, meta_tokens, g_pre_mix, w_in, g_cq, w_uq, g_ckv, w_ukv, g_qn, g_kn,
           w_pa, w_pb, w_o, g_post_mix, g_pre_ffn, w_up, w_conv, b_conv, w_down, g_post_ffn):
    w = _prep_weights(g_pre_mix, w_in, g_cq, w_uq, g_ckv, w_ukv, g_qn, g_kn, w_pa, w_pb, w_o,
                      g_post_mix, g_pre_ffn, w_up, w_conv, b_conv, w_down, g_post_ffn)
    t = jnp.arange(SEQ, dtype=jnp.int32)
    tabs_real = _tables(t + N_META, t // GRID_W, t % GRID_W)
    zero = jnp.zeros((N_META,), jnp.int32)
    tabs_meta = _tables(jnp.arange(N_META), zero, zero)
    return (_trunk(x_prompt, meta_tokens, w, tabs_real, tabs_meta, "p"),
            _trunk(x_sample, meta_tokens, w, tabs_real, tabs_meta, "s"))


def _trunk(x, meta_tokens, w, tabs_real, tabs_meta, tag):
    batch = x.shape[0]
    xr = x.reshape(batch * SEQ, D_MODEL)
    xm = jnp.broadcast_to(meta_tokens[None], (batch, N_META, D_MODEL)).reshape(batch * N_META, D_MODEL)
    mrows = batch * N_META
    tabs_meta = tuple(jnp.tile(a, (batch, 1)) for a in tabs_meta)

    z = _matmul(_norm_rows(xr, w["g_pre_mix"], 256, "norm_" + tag), w["w_in"], 1024, 1024, BF16,
                "in_proj_" + tag)
    zm = _matmul(_norm_rows(xm, w["g_pre_mix"], mrows, "norm_meta_" + tag), w["w_in"], mrows, 1024, BF16,
                 "in_proj_meta_" + tag)
    qa, ka, va, qb, kb = _heads(z, w, tabs_real, 256, SEQ // 256, "heads_" + tag)
    qam, kam, vam, qbm, kbm = _heads(zm, w, tabs_meta, mrows, 1, "heads_meta_" + tag)

    cfg_a = dict(batch=batch, heads=MLA_HEADS, group=1, dk=MLA_QK, v_col0=0)
    cfg_b = dict(batch=batch, heads=GQA_HEADS, group=GQA_GROUP, dk=HEAD_DIM, v_col0=OFF_GV // HEAD_DIM)
    oa = _attention(qa, ka, va, kam, vam, lq=SEQ, tq=1024, tk=1024, name="attn_a_" + tag, **cfg_a)
    ob = _attention(qb, kb, z, kbm, zm, lq=SEQ, tq=1024, tk=1024, name="attn_b_" + tag, **cfg_b)
    oam = _attention(qam, ka, va, kam, vam, lq=N_META, tq=N_META, tk=1024, name="attn_a_meta_" + tag, **cfg_a)
    obm = _attention(qbm, kb, z, kbm, zm, lq=N_META, tq=N_META, tk=1024, name="attn_b_meta_" + tag, **cfg_b)

    mg = _merge(oa, ob, z, w["w_pa"], w["w_pb"], 1024, 1024, "merge_" + tag)
    mgm = _merge(oam, obm, zm, w["w_pa"], w["w_pb"], mrows, 1024, "merge_meta_" + tag)
    h1, hn = _proj_residual(mg, w["w_o"], xr, w["g_post_mix"], w["g_pre_ffn"], 512, 512, "out_proj_" + tag)
    _, hnm = _proj_residual(mgm, w["w_o"], xm, w["g_post_mix"], w["g_pre_ffn"], mrows, 512,
                            "out_proj_meta_" + tag)

    am = _matmul(hnm, w["w_up_a"], mrows, 1024, F32, "ffn_up_meta_" + tag)
    am_last = am.reshape(batch, N_META, D_FF_PAD)[:, N_META - 1:, :]
    am_last = jnp.pad(am_last, ((0, 0), (0, 7), (0, 0))).reshape(batch * 8, D_FF_PAD)
    f = _ffn_up(hn, am_last, w["w_up_a"], w["w_up_g"], w["w_conv"], w["b_conv"], 1024, 512, "ffn_up_" + tag)
    y = _proj_residual(f, w["w_down"], h1, w["g_post_ffn"], None, 512, 1024, "ffn_down_" + tag)
    return y.reshape(batch, SEQ, D_MODEL)
```

```python
import functools
import math

import jax
import jax.numpy as jnp
import numpy as np
from jax import lax
from jax.experimental import pallas as pl
from jax.experimental.pallas import tpu as pltpu

F32 = jnp.float32
BF16 = jnp.bfloat16

D_MODEL = 4096
SEQ = 4096
N_META = 16
GRID_W = 64
HEAD_DIM = 128
MLA_HEADS = 16
MLA_Q_LORA = 1024
MLA_KV_LORA = 512
MLA_NOPE = 128
MLA_ROPE = 64
MLA_V = 128
MLA_QK = 256
GQA_HEADS = 16
GQA_KV_HEADS = 4
GQA_GROUP = GQA_HEADS // GQA_KV_HEADS
D_FF = 11008
D_FF_PAD = 11264
ROPE_THETA = 10000.0
EPS = 1e-6
LOG2E = math.log2(math.e)

OFF_GA = 0
OFF_GB = 4096
OFF_GQ = 8192
OFF_CQ = 10240
OFF_CKV = 11264
OFF_GK = 11776
OFF_GV = 12288
OFF_KR = 12800
D_IN_PAD = 13312

VMEM_LIMIT = 60 * 1024 * 1024


def _params(sem, flags=None):
    return pltpu.CompilerParams(dimension_semantics=sem, vmem_limit_bytes=VMEM_LIMIT, flags=flags)


def _rms(x, g):
    return x * lax.rsqrt(jnp.mean(x * x, axis=-1, keepdims=True) + EPS) * g


def _norm_kernel(x_ref, g_ref, o_ref):
    o_ref[...] = _rms(x_ref[...], g_ref[...]).astype(o_ref.dtype)


def _norm_rows(x, g, tm, name):
    m = x.shape[0]
    return pl.pallas_call(
        _norm_kernel,
        grid=(m // tm,),
        in_specs=[pl.BlockSpec((tm, D_MODEL), lambda i: (i, 0)), pl.BlockSpec((1, D_MODEL), lambda i: (0, 0))],
        out_specs=pl.BlockSpec((tm, D_MODEL), lambda i: (i, 0)),
        out_shape=jax.ShapeDtypeStruct((m, D_MODEL), BF16),
        compiler_params=_params(("parallel",)),
        name=name,
    )(x, g)


def _mm_kernel(a_ref, b_ref, o_ref):
    o_ref[...] = jnp.dot(a_ref[...], b_ref[...], preferred_element_type=F32).astype(o_ref.dtype)


def _matmul(a, b, tm, tn, out_dtype, name):
    m, k = a.shape
    n = b.shape[1]
    return pl.pallas_call(
        _mm_kernel,
        grid=(m // tm, n // tn),
        in_specs=[pl.BlockSpec((tm, k), lambda i, j: (i, 0)), pl.BlockSpec((k, tn), lambda i, j: (0, j))],
        out_specs=pl.BlockSpec((tm, tn), lambda i, j: (i, j)),
        out_shape=jax.ShapeDtypeStruct((m, n), out_dtype),
        compiler_params=_params(("parallel", "parallel")),
        name=name,
    )(a, b)


def _rope(x, cos, sin_signed):
    return x * cos + pltpu.roll(x, 64, 1) * sin_signed


def _heads_kernel(cq_ref, ckv_ref, kr_ref, gq_ref, gk_ref, wuq_ref, wuk_ref, wuv_ref,
                  gcq_ref, gckv_ref, gqn_ref, gkn_ref, cosa_ref, sina_ref, cosb_ref, sinb_ref,
                  qa_ref, ka_ref, va_ref, qb_ref, kb_ref):
    cosa, sina = cosa_ref[...], sina_ref[...]
    cosb, sinb = cosb_ref[...], sinb_ref[...]
    scale_a = LOG2E / math.sqrt(MLA_NOPE + MLA_ROPE)
    scale_b = LOG2E / math.sqrt(HEAD_DIM)

    cq = _rms(cq_ref[...].astype(F32), gcq_ref[...]).astype(BF16)
    for h in range(MLA_HEADS):
        q = jnp.dot(cq, wuq_ref[:, h * MLA_QK:(h + 1) * MLA_QK], preferred_element_type=F32)
        qa_ref[:, h * MLA_QK:h * MLA_QK + MLA_NOPE] = (q[:, :MLA_NOPE] * scale_a).astype(BF16)
        qa_ref[:, h * MLA_QK + MLA_NOPE:(h + 1) * MLA_QK] = (
            _rope(q[:, MLA_NOPE:], cosa, sina) * scale_a).astype(BF16)

    ckv = _rms(ckv_ref[...].astype(F32), gckv_ref[...]).astype(BF16)
    k_rope = _rope(kr_ref[...].astype(F32), cosa, sina).astype(BF16)
    k_nope = jnp.dot(ckv, wuk_ref[...], preferred_element_type=F32).astype(BF16)
    for h in range(MLA_HEADS):
        ka_ref[:, h * MLA_QK:h * MLA_QK + MLA_NOPE] = k_nope[:, h * MLA_NOPE:(h + 1) * MLA_NOPE]
        ka_ref[:, h * MLA_QK + MLA_NOPE:(h + 1) * MLA_QK] = k_rope
    va_ref[...] = jnp.dot(ckv, wuv_ref[...], preferred_element_type=F32).astype(BF16)

    gqn, gkn = gqn_ref[...], gkn_ref[...]
    for h in range(GQA_HEADS):
        x = gq_ref[:, h * HEAD_DIM:(h + 1) * HEAD_DIM].astype(F32)
        qb_ref[:, h * HEAD_DIM:(h + 1) * HEAD_DIM] = (_rope(_rms(x, gqn), cosb, sinb) * scale_b).astype(BF16)
    for h in range(GQA_KV_HEADS):
        x = gk_ref[:, h * HEAD_DIM:(h + 1) * HEAD_DIM].astype(F32)
        kb_ref[:, h * HEAD_DIM:(h + 1) * HEAD_DIM] = _rope(_rms(x, gkn), cosb, sinb).astype(BF16)


def _heads(z, w, tabs, tm, tab_blocks, name):
    m = z.shape[0]

    def zspec(width, off):
        return pl.BlockSpec((tm, width), lambda i: (i, off // width))

    def full(a):
        return pl.BlockSpec(a.shape, lambda i: (0,) * a.ndim)

    tab_spec = pl.BlockSpec((tm, 128), lambda i: (i % tab_blocks, 0))
    out_widths = (MLA_HEADS * MLA_QK, MLA_HEADS * MLA_QK, MLA_HEADS * MLA_V, GQA_HEADS * HEAD_DIM,
                  GQA_KV_HEADS * HEAD_DIM)
    consts = (w["w_uq"], w["w_uk"], w["w_uv"], w["g_cq"], w["g_ckv"], w["g_qn"], w["g_kn"])
    return pl.pallas_call(
        _heads_kernel,
        grid=(m // tm,),
        in_specs=[zspec(MLA_Q_LORA, OFF_CQ), zspec(MLA_KV_LORA, OFF_CKV), zspec(128, OFF_KR),
                  zspec(GQA_HEADS * HEAD_DIM, OFF_GQ), zspec(GQA_KV_HEADS * HEAD_DIM, OFF_GK)]
        + [full(a) for a in consts] + [tab_spec] * 4,
        out_specs=[pl.BlockSpec((tm, n), lambda i: (i, 0)) for n in out_widths],
        out_shape=[jax.ShapeDtypeStruct((m, n), BF16) for n in out_widths],
        compiler_params=_params(("parallel",)),
        name=name,
    )(z, z, z, z, z, *consts, *tabs)


def _attn_kernel(tk, q_ref, k_ref, v_ref, km_ref, vm_ref, o_ref):
    nt = (((1,), (1,)), ((), ()))
    q = q_ref[...]
    s = lax.dot_general(q, km_ref[...], nt, preferred_element_type=F32)
    m = jnp.max(s, axis=-1, keepdims=True)
    p = jnp.exp2(s - m)
    l = jnp.sum(p, axis=-1, keepdims=True)
    acc = jnp.dot(p.astype(BF16), vm_ref[...], preferred_element_type=F32)
    for c in range(k_ref.shape[0] // tk):
        s = lax.dot_general(q, k_ref[c * tk:(c + 1) * tk, :], nt, preferred_element_type=F32)
        m_new = jnp.maximum(m, jnp.max(s, axis=-1, keepdims=True))
        alpha = jnp.exp2(m - m_new)
        p = jnp.exp2(s - m_new)
        l = alpha * l + jnp.sum(p, axis=-1, keepdims=True)
        acc = alpha * acc + jnp.dot(p.astype(BF16), v_ref[c * tk:(c + 1) * tk, :], preferred_element_type=F32)
        m = m_new
    o_ref[...] = (acc / l).astype(o_ref.dtype)


def _attention(q, k, v, km, vm, *, batch, heads, group, dk, v_col0, lq, tq, tk, name):
    nq = lq // tq
    dv = HEAD_DIM
    return pl.pallas_call(
        functools.partial(_attn_kernel, tk),
        grid=(batch, heads, nq),
        in_specs=[
            pl.BlockSpec((tq, dk), lambda b, h, i: (b * nq + i, h)),
            pl.BlockSpec((SEQ, dk), lambda b, h, i: (b, h // group)),
            pl.BlockSpec((SEQ, dv), lambda b, h, i: (b, v_col0 + h // group)),
            pl.BlockSpec((N_META, dk), lambda b, h, i: (b, h // group)),
            pl.BlockSpec((N_META, dv), lambda b, h, i: (b, v_col0 + h // group)),
        ],
        out_specs=pl.BlockSpec((tq, dv), lambda b, h, i: (b * nq + i, h)),
        out_shape=jax.ShapeDtypeStruct((batch * lq, heads * dv), BF16),
        compiler_params=_params(("parallel", "parallel", "parallel")),
        name=name,
    )(q, k, v, km, vm)


def _merge_kernel(oa_ref, ob_ref, ga_ref, gb_ref, wpa_ref, wpb_ref, o_ref):
    a = jnp.dot(oa_ref[...], wpa_ref[...], preferred_element_type=F32)
    b = jnp.dot(ob_ref[...], wpb_ref[...], preferred_element_type=F32)
    ga = jax.nn.sigmoid(ga_ref[...].astype(F32))
    gb = jax.nn.sigmoid(gb_ref[...].astype(F32))
    o_ref[...] = (ga * a + gb * b).astype(o_ref.dtype)


def _merge(oa, ob, z, wpa, wpb, tm, tn, name):
    m, k = oa.shape
    return pl.pallas_call(
        _merge_kernel,
        grid=(m // tm, D_MODEL // tn),
        in_specs=[
            pl.BlockSpec((tm, k), lambda i, j: (i, 0)),
            pl.BlockSpec((tm, k), lambda i, j: (i, 0)),
            pl.BlockSpec((tm, tn), lambda i, j: (i, OFF_GA // tn + j)),
            pl.BlockSpec((tm, tn), lambda i, j: (i, OFF_GB // tn + j)),
            pl.BlockSpec((k, tn), lambda i, j: (0, j)),
            pl.BlockSpec((k, tn), lambda i, j: (0, j)),
        ],
        out_specs=pl.BlockSpec((tm, tn), lambda i, j: (i, j)),
        out_shape=jax.ShapeDtypeStruct((m, D_MODEL), BF16),
        compiler_params=_params(("parallel", "parallel")),
        name=name,
    )(oa, ob, z, z, wpa, wpb)


PROJ_CHUNK = 512
NORM_ROWS = 128


def _proj_residual_kernel(nk, a_ref, w_ref, res_ref, g_ref, g2_ref, h_ref, hn_ref):
    kk = pl.program_id(1)

    def accumulate(first):
        a = a_ref[...]
        for c in range(0, D_MODEL, PROJ_CHUNK):
            part = jnp.dot(a, w_ref[:, c:c + PROJ_CHUNK], preferred_element_type=F32)
            if first:
                h_ref[:, c:c + PROJ_CHUNK] = part
            else:
                h_ref[:, c:c + PROJ_CHUNK] += part

    pl.when(kk == 0)(functools.partial(accumulate, True))
    pl.when(kk > 0)(functools.partial(accumulate, False))

    @pl.when(kk == nk - 1)
    def _():
        for r in range(0, h_ref.shape[0], NORM_ROWS):
            rows = slice(r, min(r + NORM_ROWS, h_ref.shape[0]))
            h = res_ref[rows, :] + _rms(h_ref[rows, :], g_ref[...])
            h_ref[rows, :] = h
            if hn_ref is not None:
                hn_ref[rows, :] = _rms(h, g2_ref[...]).astype(hn_ref.dtype)


def _proj_residual_nohn_kernel(nk, a_ref, w_ref, res_ref, g_ref, h_ref):
    _proj_residual_kernel(nk, a_ref, w_ref, res_ref, g_ref, None, h_ref, None)


def _proj_residual(a, w, res, g, g2, tm, tk, name):
    m, k = a.shape
    nk = k // tk
    gspec = pl.BlockSpec((1, D_MODEL), lambda i, kk: (0, 0))
    row_spec = pl.BlockSpec((tm, D_MODEL), lambda i, kk: (i, 0))
    in_specs = [pl.BlockSpec((tm, tk), lambda i, kk: (i, kk)),
                pl.BlockSpec((tk, D_MODEL), lambda i, kk: (kk, 0)), row_spec, gspec]
    if g2 is None:
        return pl.pallas_call(
            functools.partial(_proj_residual_nohn_kernel, nk),
            grid=(m // tm, nk), in_specs=in_specs, out_specs=row_spec,
            out_shape=jax.ShapeDtypeStruct((m, D_MODEL), F32),
            compiler_params=_params(("parallel", "arbitrary")), name=name,
        )(a, w, res, g)
    return pl.pallas_call(
        functools.partial(_proj_residual_kernel, nk),
        grid=(m // tm, nk), in_specs=in_specs + [gspec], out_specs=[row_spec, row_spec],
        out_shape=[jax.ShapeDtypeStruct((m, D_MODEL), F32), jax.ShapeDtypeStruct((m, D_MODEL), BF16)],
        compiler_params=_params(("parallel", "arbitrary")), name=name,
    )(a, w, res, g, g2)


HALO = 16
LANES = 128
MXU_N = 256


def _ffn_up_kernel(tm, tiles_per_seq, hn_ref, prev_ref, next_ref, hnm_ref, wa_ref, wg_ref, wc_ref, bc_ref,
                   f_ref, ext_ref):
    i = pl.program_id(0)

    @pl.when(pl.program_id(1) == 0)
    def _():
        first = i % tiles_per_seq == 0
        last = i % tiles_per_seq == tiles_per_seq - 1
        ext_ref[0:HALO, :] = jnp.where(first, hnm_ref[...], prev_ref[...])
        ext_ref[HALO:HALO + tm, :] = hn_ref[...]
        ext_ref[HALO + tm:, :] = jnp.where(last, jnp.zeros_like(next_ref), next_ref[...])

    wc = wc_ref[...]
    bc = bc_ref[...]
    for c in range(0, f_ref.shape[1], MXU_N):
        cols = slice(c, c + MXU_N)
        a = jnp.dot(ext_ref[...], wa_ref[:, cols], preferred_element_type=F32)
        gate = jnp.dot(ext_ref[HALO:HALO + tm, :], wg_ref[:, cols], preferred_element_type=F32)
        conv = (a[HALO - 1:HALO - 1 + tm] * wc[0:1, cols] + a[HALO:HALO + tm] * wc[1:2, cols]
                + a[HALO + 1:HALO + 1 + tm] * wc[2:3, cols] + bc[:, cols])
        f_ref[:, cols] = (jax.nn.gelu(conv) * gate).astype(f_ref.dtype)


def _ffn_up(hn, hn_meta, wa, wg, wc, bc, tm, tf, name):
    m = hn.shape[0]
    tiles_per_seq = SEQ // tm
    hb = tm // HALO
    last = m // HALO - 1
    return pl.pallas_call(
        functools.partial(_ffn_up_kernel, tm, tiles_per_seq),
        grid=(m // tm, D_FF_PAD // tf),
        in_specs=[
            pl.BlockSpec((tm, D_MODEL), lambda i, j: (i, 0)),
            pl.BlockSpec((HALO, D_MODEL), lambda i, j: (jnp.maximum(i * hb - 1, 0), 0)),
            pl.BlockSpec((HALO, D_MODEL), lambda i, j: (jnp.minimum((i + 1) * hb, last), 0)),
            pl.BlockSpec((N_META, D_MODEL), lambda i, j: (i // tiles_per_seq, 0)),
            pl.BlockSpec((D_MODEL, tf), lambda i, j: (0, j)),
            pl.BlockSpec((D_MODEL, tf), lambda i, j: (0, j)),
            pl.BlockSpec((3, tf), lambda i, j: (0, j)),
            pl.BlockSpec((1, tf), lambda i, j: (0, j)),
        ],
        out_specs=pl.BlockSpec((tm, tf), lambda i, j: (i, j)),
        out_shape=jax.ShapeDtypeStruct((m, D_FF_PAD), BF16),
        scratch_shapes=[pltpu.VMEM((tm + 2 * HALO, D_MODEL), BF16)],
        compiler_params=_params(("parallel", "arbitrary")),
        name=name,
    )(hn, hn, hn, hn_meta, wa, wg, wc, bc)


def _prep_weights(g_pre_mix, w_in, g_cq, w_uq, g_ckv, w_ukv, g_qn, g_kn, w_pa, w_pb, w_o,
                  g_post_mix, g_pre_ffn, w_up, w_conv, b_conv, w_down, g_post_ffn):
    perm_b = np.concatenate([np.arange(0, 32), np.arange(64, 96), np.arange(32, 64), np.arange(96, 128)])
    o = np.cumsum((0, MLA_Q_LORA, MLA_KV_LORA, MLA_ROPE, GQA_HEADS * HEAD_DIM, GQA_KV_HEADS * HEAD_DIM,
                   GQA_KV_HEADS * HEAD_DIM, D_MODEL, D_MODEL))
    w = w_in[0]
    cq, ckv, kr, gq, gk, gv, ga, gb = (w[:, o[i]:o[i + 1]] for i in range(8))
    gq = gq.reshape(D_MODEL, GQA_HEADS, HEAD_DIM)[:, :, perm_b].reshape(D_MODEL, -1)
    gk = gk.reshape(D_MODEL, GQA_KV_HEADS, HEAD_DIM)[:, :, perm_b].reshape(D_MODEL, -1)
    z32 = jnp.zeros((D_MODEL, 32), w.dtype)
    kr = jnp.concatenate([kr[:, :32], z32, kr[:, 32:], z32], axis=1)
    tail = jnp.zeros((D_MODEL, D_IN_PAD - OFF_KR - 128), w.dtype)
    w_in_p = jnp.concatenate([ga, gb, gq, cq, ckv, gk, gv, kr, tail], axis=1).astype(BF16)

    uq = w_uq[0].reshape(MLA_Q_LORA, MLA_HEADS, MLA_NOPE + MLA_ROPE)
    zq = jnp.zeros((MLA_Q_LORA, MLA_HEADS, 32), uq.dtype)
    uq = jnp.concatenate([uq[:, :, :MLA_NOPE], uq[:, :, MLA_NOPE:MLA_NOPE + 32], zq,
                          uq[:, :, MLA_NOPE + 32:], zq], axis=2).reshape(MLA_Q_LORA, -1).astype(BF16)
    ukv = w_ukv[0].reshape(MLA_KV_LORA, MLA_HEADS, MLA_NOPE + MLA_V)
    uk = ukv[:, :, :MLA_NOPE].reshape(MLA_KV_LORA, -1).astype(BF16)
    uv = ukv[:, :, MLA_NOPE:].reshape(MLA_KV_LORA, -1).astype(BF16)

    fpad = D_FF_PAD - D_FF
    up = w_up[0]
    return dict(
        g_pre_mix=g_pre_mix, w_in=w_in_p, g_cq=g_cq, w_uq=uq, g_ckv=g_ckv, w_uk=uk, w_uv=uv,
        g_qn=g_qn[:, perm_b], g_kn=g_kn[:, perm_b],
        w_pa=w_pa[0].astype(BF16), w_pb=w_pb[0].astype(BF16), w_o=w_o[0].astype(BF16),
        g_post_mix=g_post_mix, g_pre_ffn=g_pre_ffn,
        w_up_a=jnp.pad(up[:, :D_FF], ((0, 0), (0, fpad))).astype(BF16),
        w_up_g=jnp.pad(up[:, D_FF:], ((0, 0), (0, fpad))).astype(BF16),
        w_conv=jnp.pad(w_conv[0], ((0, 0), (0, fpad))), b_conv=jnp.pad(b_conv, ((0, 0), (0, fpad))),
        w_down=jnp.pad(w_down[0], ((0, fpad), (0, 0))).astype(BF16), g_post_ffn=g_post_ffn,
    )


def _rope_tables(pos, dim):
    inv = ROPE_THETA ** (-jnp.arange(0, dim, 2, dtype=F32) / dim)
    ang = pos.astype(F32)[:, None] * inv[None, :]
    return jnp.cos(ang), jnp.sin(ang)


def _tables(pos, row, col):
    c1, s1 = _rope_tables(pos, MLA_ROPE)
    z = jnp.zeros_like(c1)
    cos_a = jnp.concatenate([c1, z, c1, z], axis=1)
    sin_a = jnp.concatenate([-s1, z, s1, z], axis=1)
    cr, sr = _rope_tables(row, HEAD_DIM // 2)
    cc, sc = _rope_tables(col, HEAD_DIM // 2)
    cos_b = jnp.concatenate([cr, cc, cr, cc], axis=1)
    sin_b = jnp.concatenate([-sr, -sc, sr, sc], axis=1)
    return cos_a, sin_a, cos_b, sin_b


def kernel(x_prompt, x_sample, meta_tokens, g_pre_mix, w_in, g_cq, w_uq, g_ckv, w_ukv, g_qn, g_kn,
           w_pa, w_pb, w_o, g_post_mix, g_pre_ffn, w_up, w_conv, b_conv, w_down, g_post_ffn):
    w = _prep_weights(g_pre_mix, w_in, g_cq, w_uq, g_ckv, w_ukv, g_qn, g_kn, w_pa, w_pb, w_o,
                      g_post_mix, g_pre_ffn, w_up, w_conv, b_conv, w_down, g_post_ffn)
    t = jnp.arange(SEQ, dtype=jnp.int32)
    tabs_real = _tables(t + N_META, t // GRID_W, t % GRID_W)
    zero = jnp.zeros((N_META,), jnp.int32)
    tabs_meta = _tables(jnp.arange(N_META), zero, zero)
    return (_trunk(x_prompt, meta_tokens, w, tabs_real, tabs_meta, "p"),
            _trunk(x_sample, meta_tokens, w, tabs_real, tabs_meta, "s"))


def _trunk(x, meta_tokens, w, tabs_real, tabs_meta, tag):
    batch = x.shape[0]
    xr = x.reshape(batch * SEQ, D_MODEL)
    xm = jnp.broadcast_to(meta_tokens[None], (batch, N_META, D_MODEL)).reshape(batch * N_META, D_MODEL)
    mrows = batch * N_META
    tabs_meta = tuple(jnp.tile(a, (batch, 1)) for a in tabs_meta)

    z = _matmul(_norm_rows(xr, w["g_pre_mix"], 256, "norm_" + tag), w["w_in"], 1024, 1024, BF16,
                "in_proj_" + tag)
    zm = _matmul(_norm_rows(xm, w["g_pre_mix"], mrows, "norm_meta_" + tag), w["w_in"], mrows, 1024, BF16,
                 "in_proj_meta_" + tag)
    qa, ka, va, qb, kb = _heads(z, w, tabs_real, 256, SEQ // 256, "heads_" + tag)
    qam, kam, vam, qbm, kbm = _heads(zm, w, tabs_meta, mrows, 1, "heads_meta_" + tag)

    cfg_a = dict(batch=batch, heads=MLA_HEADS, group=1, dk=MLA_QK, v_col0=0)
    cfg_b = dict(batch=batch, heads=GQA_HEADS, group=GQA_GROUP, dk=HEAD_DIM, v_col0=OFF_GV // HEAD_DIM)
    oa = _attention(qa, ka, va, kam, vam, lq=SEQ, tq=1024, tk=1024, name="attn_a_" + tag, **cfg_a)
    ob = _attention(qb, kb, z, kbm, zm, lq=SEQ, tq=1024, tk=1024, name="attn_b_" + tag, **cfg_b)
    oam = _attention(qam, ka, va, kam, vam, lq=N_META, tq=N_META, tk=1024, name="attn_a_meta_" + tag, **cfg_a)
    obm = _attention(qbm, kb, z, kbm, zm, lq=N_META, tq=N_META, tk=1024, name="attn_b_meta_" + tag, **cfg_b)

    mg = _merge(oa, ob, z, w["w_pa"], w["w_pb"], 1024, 1024, "merge_" + tag)
    mgm = _merge(oam, obm, zm, w["w_pa"], w["w_pb"], mrows, 1024, "merge_meta_" + tag)
    h1, hn = _proj_residual(mg, w["w_o"], xr, w["g_post_mix"], w["g_pre_ffn"], 512, 512, "out_proj_" + tag)
    _, hnm = _proj_residual(mgm, w["w_o"], xm, w["g_post_mix"], w["g_pre_ffn"], mrows, 512,
                            "out_proj_meta_" + tag)

    f = _ffn_up(hn, hnm, w["w_up_a"], w["w_up_g"], w["w_conv"], w["b_conv"], 1024, 512, "ffn_up_" + tag)
    y = _proj_residual(f, w["w_down"], h1, w["g_post_ffn"], None, 512, 1024, "ffn_down_" + tag)
    return y.reshape(batch, SEQ, D_MODEL)
```

```python
import functools
import math

import jax
import jax.numpy as jnp
import numpy as np
from jax import lax
from jax.experimental import pallas as pl
from jax.experimental.pallas import tpu as pltpu

F32 = jnp.float32
BF16 = jnp.bfloat16

D_MODEL = 4096
SEQ = 4096
N_META = 16
GRID_W = 64
HEAD_DIM = 128
MLA_HEADS = 16
MLA_Q_LORA = 1024
MLA_KV_LORA = 512
MLA_NOPE = 128
MLA_ROPE = 64
MLA_V = 128
MLA_QK = 256
GQA_HEADS = 16
GQA_KV_HEADS = 4
GQA_GROUP = GQA_HEADS // GQA_KV_HEADS
D_FF = 11008
D_FF_PAD = 11264
ROPE_THETA = 10000.0
EPS = 1e-6
LOG2E = math.log2(math.e)

OFF_GA = 0
OFF_GB = 4096
OFF_GQ = 8192
OFF_CQ = 10240
OFF_CKV = 11264
OFF_GK = 11776
OFF_GV = 12288
OFF_KR = 12800
D_IN_PAD = 13312

VMEM_LIMIT = 60 * 1024 * 1024


def _params(sem, flags=None):
    return pltpu.CompilerParams(dimension_semantics=sem, vmem_limit_bytes=VMEM_LIMIT, flags=flags)


def _rms(x, g):
    return x * lax.rsqrt(jnp.mean(x * x, axis=-1, keepdims=True) + EPS) * g


def _norm_kernel(x_ref, g_ref, o_ref):
    o_ref[...] = _rms(x_ref[...], g_ref[...]).astype(o_ref.dtype)


def _norm_rows(x, g, tm, name):
    m = x.shape[0]
    return pl.pallas_call(
        _norm_kernel,
        grid=(m // tm,),
        in_specs=[pl.BlockSpec((tm, D_MODEL), lambda i: (i, 0)), pl.BlockSpec((1, D_MODEL), lambda i: (0, 0))],
        out_specs=pl.BlockSpec((tm, D_MODEL), lambda i: (i, 0)),
        out_shape=jax.ShapeDtypeStruct((m, D_MODEL), BF16),
        compiler_params=_params(("parallel",)),
        name=name,
    )(x, g)


def _mm_kernel(a_ref, b_ref, o_ref):
    o_ref[...] = jnp.dot(a_ref[...], b_ref[...], preferred_element_type=F32).astype(o_ref.dtype)


def _matmul(a, b, tm, tn, out_dtype, name):
    m, k = a.shape
    n = b.shape[1]
    return pl.pallas_call(
        _mm_kernel,
        grid=(m // tm, n // tn),
        in_specs=[pl.BlockSpec((tm, k), lambda i, j: (i, 0)), pl.BlockSpec((k, tn), lambda i, j: (0, j))],
        out_specs=pl.BlockSpec((tm, tn), lambda i, j: (i, j)),
        out_shape=jax.ShapeDtypeStruct((m, n), out_dtype),
        compiler_params=_params(("parallel", "parallel")),
        name=name,
    )(a, b)


def _rope(x, cos, sin_signed):
    return x * cos + pltpu.roll(x, 64, 1) * sin_signed


NT_DIMS = (((1,), (1,)), ((), ()))


def _heads_kernel(cq_ref, ckv_ref, kr_ref, gq_ref, gk_ref, gv_ref, wuq_ref, wuk_ref, wuvt_ref, eye_ref,
                  gcq_ref, gckv_ref, gqn_ref, gkn_ref, cosa_ref, sina_ref, cosb_ref, sinb_ref,
                  qa_ref, ka_ref, vat_ref, qb_ref, kb_ref, vbt_ref):
    cosa, sina = cosa_ref[...], sina_ref[...]
    cosb, sinb = cosb_ref[...], sinb_ref[...]
    scale_a = LOG2E / math.sqrt(MLA_NOPE + MLA_ROPE)
    scale_b = LOG2E / math.sqrt(HEAD_DIM)

    cq = _rms(cq_ref[...].astype(F32), gcq_ref[...]).astype(BF16)
    for h in range(MLA_HEADS):
        q = jnp.dot(cq, wuq_ref[:, h * MLA_QK:(h + 1) * MLA_QK], preferred_element_type=F32)
        qa_ref[:, h * MLA_QK:h * MLA_QK + MLA_NOPE] = (q[:, :MLA_NOPE] * scale_a).astype(BF16)
        qa_ref[:, h * MLA_QK + MLA_NOPE:(h + 1) * MLA_QK] = (
            _rope(q[:, MLA_NOPE:], cosa, sina) * scale_a).astype(BF16)

    ckv = _rms(ckv_ref[...].astype(F32), gckv_ref[...]).astype(BF16)
    k_rope = _rope(kr_ref[...].astype(F32), cosa, sina).astype(BF16)
    k_nope = jnp.dot(ckv, wuk_ref[...], preferred_element_type=F32).astype(BF16)
    for h in range(MLA_HEADS):
        ka_ref[:, h * MLA_QK:h * MLA_QK + MLA_NOPE] = k_nope[:, h * MLA_NOPE:(h + 1) * MLA_NOPE]
        ka_ref[:, h * MLA_QK + MLA_NOPE:(h + 1) * MLA_QK] = k_rope
    vat_ref[...] = lax.dot_general(wuvt_ref[...], ckv, NT_DIMS, preferred_element_type=F32).astype(BF16)
    vbt_ref[...] = lax.dot_general(eye_ref[...], gv_ref[...], NT_DIMS, preferred_element_type=F32).astype(BF16)

    gqn, gkn = gqn_ref[...], gkn_ref[...]
    for h in range(GQA_HEADS):
        x = gq_ref[:, h * HEAD_DIM:(h + 1) * HEAD_DIM].astype(F32)
        qb_ref[:, h * HEAD_DIM:(h + 1) * HEAD_DIM] = (_rope(_rms(x, gqn), cosb, sinb) * scale_b).astype(BF16)
    for h in range(GQA_KV_HEADS):
        x = gk_ref[:, h * HEAD_DIM:(h + 1) * HEAD_DIM].astype(F32)
        kb_ref[:, h * HEAD_DIM:(h + 1) * HEAD_DIM] = _rope(_rms(x, gkn), cosb, sinb).astype(BF16)


def _heads(z, w, tabs, tm, tab_blocks, name):
    m = z.shape[0]

    def zspec(width, off):
        return pl.BlockSpec((tm, width), lambda i: (i, off // width))

    def full(a):
        return pl.BlockSpec(a.shape, lambda i: (0,) * a.ndim)

    tab_spec = pl.BlockSpec((tm, 128), lambda i: (i % tab_blocks, 0))
    outs = ((MLA_HEADS * MLA_QK, True), (MLA_HEADS * MLA_QK, True), (MLA_HEADS * MLA_V, False),
            (GQA_HEADS * HEAD_DIM, True), (GQA_KV_HEADS * HEAD_DIM, True), (GQA_KV_HEADS * HEAD_DIM, False))
    consts = (w["w_uq"], w["w_uk"], w["w_uvt"], w["eye_kv"], w["g_cq"], w["g_ckv"], w["g_qn"], w["g_kn"])
    return pl.pallas_call(
        _heads_kernel,
        grid=(m // tm,),
        in_specs=[zspec(MLA_Q_LORA, OFF_CQ), zspec(MLA_KV_LORA, OFF_CKV), zspec(128, OFF_KR),
                  zspec(GQA_HEADS * HEAD_DIM, OFF_GQ), zspec(GQA_KV_HEADS * HEAD_DIM, OFF_GK),
                  zspec(GQA_KV_HEADS * HEAD_DIM, OFF_GV)]
        + [full(a) for a in consts] + [tab_spec] * 4,
        out_specs=[pl.BlockSpec((tm, n), lambda i: (i, 0)) if rows else pl.BlockSpec((n, tm), lambda i: (0, i))
                   for n, rows in outs],
        out_shape=[jax.ShapeDtypeStruct((m, n) if rows else (n, m), BF16) for n, rows in outs],
        compiler_params=_params(("parallel",)),
        name=name,
    )(z, z, z, z, z, z, *consts, *tabs)


def _attn_kernel(tk, transpose_out, q_ref, k_ref, vt_ref, km_ref, vmt_ref, o_ref):
    q = q_ref[...]
    s = lax.dot_general(km_ref[...], q, NT_DIMS, preferred_element_type=F32)
    m = jnp.max(s, axis=0, keepdims=True)
    p = jnp.exp2(s - m)
    l = jnp.sum(p, axis=0, keepdims=True)
    acc = jnp.dot(vmt_ref[...], p.astype(BF16), preferred_element_type=F32)
    def scores(c):
        return lax.dot_general(k_ref[c * tk:(c + 1) * tk, :], q, NT_DIMS, preferred_element_type=F32)

    n_chunks = k_ref.shape[0] // tk
    s_next = scores(0)
    for c in range(n_chunks):
        s = s_next
        if c + 1 < n_chunks:
            s_next = scores(c + 1)
        m_new = jnp.maximum(m, jnp.max(s, axis=0, keepdims=True))
        alpha = jnp.exp2(m - m_new)
        p = jnp.exp2(s - m_new)
        l = alpha * l + jnp.sum(p, axis=0, keepdims=True)
        acc = alpha * acc + jnp.dot(vt_ref[:, c * tk:(c + 1) * tk], p.astype(BF16), preferred_element_type=F32)
        m = m_new
    o = acc / l
    o_ref[...] = (o.T if transpose_out else o).astype(o_ref.dtype)


def _attention(q, k, vt, km, vmt, *, batch, heads, group, dk, lq, tq, tk, name):
    nq = lq // tq
    dv = HEAD_DIM
    transpose_out = tq % LANES == 0
    if transpose_out:
        out_spec = pl.BlockSpec((tq, dv), lambda b, h, i: (b * nq + i, h))
        out_shape = jax.ShapeDtypeStruct((batch * lq, heads * dv), BF16)
    else:
        out_spec = pl.BlockSpec((None, dv, tq), lambda b, h, i: (b, h, i))
        out_shape = jax.ShapeDtypeStruct((batch, heads * dv, lq), BF16)
    return pl.pallas_call(
        functools.partial(_attn_kernel, tk, transpose_out),
        grid=(batch, heads, nq),
        in_specs=[
            pl.BlockSpec((tq, dk), lambda b, h, i: (b * nq + i, h)),
            pl.BlockSpec((SEQ, dk), lambda b, h, i: (b, h // group)),
            pl.BlockSpec((dv, SEQ), lambda b, h, i: (h // group, b)),
            pl.BlockSpec((N_META, dk), lambda b, h, i: (b, h // group)),
            pl.BlockSpec((None, dv, N_META), lambda b, h, i: (b, h // group, 0)),
        ],
        out_specs=out_spec,
        out_shape=out_shape,
        compiler_params=_params(("parallel", "parallel", "parallel")),
        name=name,
    )(q, k, vt, km, vmt)


def _merge_kernel(oa_ref, ob_ref, ga_ref, gb_ref, wpa_ref, wpb_ref, o_ref):
    a = jnp.dot(oa_ref[...], wpa_ref[...], preferred_element_type=F32)
    b = jnp.dot(ob_ref[...], wpb_ref[...], preferred_element_type=F32)
    ga = jax.nn.sigmoid(ga_ref[...].astype(F32))
    gb = jax.nn.sigmoid(gb_ref[...].astype(F32))
    o_ref[...] = (ga * a + gb * b).astype(o_ref.dtype)


def _merge(oa, ob, z, wpa, wpb, tm, tn, name):
    m, k = oa.shape
    return pl.pallas_call(
        _merge_kernel,
        grid=(m // tm, D_MODEL // tn),
        in_specs=[
            pl.BlockSpec((tm, k), lambda i, j: (i, 0)),
            pl.BlockSpec((tm, k), lambda i, j: (i, 0)),
            pl.BlockSpec((tm, tn), lambda i, j: (i, OFF_GA // tn + j)),
            pl.BlockSpec((tm, tn), lambda i, j: (i, OFF_GB // tn + j)),
            pl.BlockSpec((k, tn), lambda i, j: (0, j)),
            pl.BlockSpec((k, tn), lambda i, j: (0, j)),
        ],
        out_specs=pl.BlockSpec((tm, tn), lambda i, j: (i, j)),
        out_shape=jax.ShapeDtypeStruct((m, D_MODEL), BF16),
        compiler_params=_params(("parallel", "parallel")),
        name=name,
    )(oa, ob, z, z, wpa, wpb)


PROJ_CHUNK = 512
NORM_ROWS = 128


def _proj_residual_kernel(nk, a_ref, w_ref, res_ref, g_ref, g2_ref, h_ref, hn_ref):
    kk = pl.program_id(1)

    def accumulate(first):
        a = a_ref[...]
        for c in range(0, D_MODEL, PROJ_CHUNK):
            part = jnp.dot(a, w_ref[:, c:c + PROJ_CHUNK], preferred_element_type=F32)
            if first:
                h_ref[:, c:c + PROJ_CHUNK] = part
            else:
                h_ref[:, c:c + PROJ_CHUNK] += part

    pl.when(kk == 0)(functools.partial(accumulate, True))
    pl.when(kk > 0)(functools.partial(accumulate, False))

    @pl.when(kk == nk - 1)
    def _():
        for r in range(0, h_ref.shape[0], NORM_ROWS):
            rows = slice(r, min(r + NORM_ROWS, h_ref.shape[0]))
            h = res_ref[rows, :] + _rms(h_ref[rows, :], g_ref[...])
            h_ref[rows, :] = h
            if hn_ref is not None:
                hn_ref[rows, :] = _rms(h, g2_ref[...]).astype(hn_ref.dtype)


def _proj_residual_nohn_kernel(nk, a_ref, w_ref, res_ref, g_ref, h_ref):
    _proj_residual_kernel(nk, a_ref, w_ref, res_ref, g_ref, None, h_ref, None)


def _proj_residual(a, w, res, g, g2, tm, tk, name):
    m, k = a.shape
    nk = k // tk
    gspec = pl.BlockSpec((1, D_MODEL), lambda i, kk: (0, 0))
    row_spec = pl.BlockSpec((tm, D_MODEL), lambda i, kk: (i, 0))
    in_specs = [pl.BlockSpec((tm, tk), lambda i, kk: (i, kk)),
                pl.BlockSpec((tk, D_MODEL), lambda i, kk: (kk, 0)), row_spec, gspec]
    if g2 is None:
        return pl.pallas_call(
            functools.partial(_proj_residual_nohn_kernel, nk),
            grid=(m // tm, nk), in_specs=in_specs, out_specs=row_spec,
            out_shape=jax.ShapeDtypeStruct((m, D_MODEL), F32),
            compiler_params=_params(("parallel", "arbitrary")), name=name,
        )(a, w, res, g)
    return pl.pallas_call(
        functools.partial(_proj_residual_kernel, nk),
        grid=(m // tm, nk), in_specs=in_specs + [gspec], out_specs=[row_spec, row_spec],
        out_shape=[jax.ShapeDtypeStruct((m, D_MODEL), F32), jax.ShapeDtypeStruct((m, D_MODEL), BF16)],
        compiler_params=_params(("parallel", "arbitrary")), name=name,
    )(a, w, res, g, g2)


HALO = 16
LANES = 128
MXU_N = 256


def _ffn_up_kernel(tm, tiles_per_seq, hn_ref, prev_ref, next_ref, hnm_ref, wa_ref, wg_ref, wc_ref, bc_ref,
                   f_ref, ext_ref):
    i = pl.program_id(0)

    @pl.when(pl.program_id(1) == 0)
    def _():
        first = i % tiles_per_seq == 0
        last = i % tiles_per_seq == tiles_per_seq - 1
        ext_ref[0:HALO, :] = jnp.where(first, hnm_ref[...], prev_ref[...])
        ext_ref[HALO:HALO + tm, :] = hn_ref[...]
        ext_ref[HALO + tm:, :] = jnp.where(last, jnp.zeros_like(next_ref), next_ref[...])

    wc = wc_ref[...]
    bc = bc_ref[...]
    for c in range(0, f_ref.shape[1], MXU_N):
        cols = slice(c, c + MXU_N)
        a = jnp.dot(ext_ref[...], wa_ref[:, cols], preferred_element_type=F32)
        gate = jnp.dot(ext_ref[HALO:HALO + tm, :], wg_ref[:, cols], preferred_element_type=F32)
        conv = (a[HALO - 1:HALO - 1 + tm] * wc[0:1, cols] + a[HALO:HALO + tm] * wc[1:2, cols]
                + a[HALO + 1:HALO + 1 + tm] * wc[2:3, cols] + bc[:, cols])
        f_ref[:, cols] = (jax.nn.gelu(conv) * gate).astype(f_ref.dtype)


def _ffn_up(hn, hn_meta, wa, wg, wc, bc, tm, tf, name):
    m = hn.shape[0]
    tiles_per_seq = SEQ // tm
    hb = tm // HALO
    last = m // HALO - 1
    return pl.pallas_call(
        functools.partial(_ffn_up_kernel, tm, tiles_per_seq),
        grid=(m // tm, D_FF_PAD // tf),
        in_specs=[
            pl.BlockSpec((tm, D_MODEL), lambda i, j: (i, 0)),
            pl.BlockSpec((HALO, D_MODEL), lambda i, j: (jnp.maximum(i * hb - 1, 0), 0)),
            pl.BlockSpec((HALO, D_MODEL), lambda i, j: (jnp.minimum((i + 1) * hb, last), 0)),
            pl.BlockSpec((N_META, D_MODEL), lambda i, j: (i // tiles_per_seq, 0)),
            pl.BlockSpec((D_MODEL, tf), lambda i, j: (0, j)),
            pl.BlockSpec((D_MODEL, tf), lambda i, j: (0, j)),
            pl.BlockSpec((3, tf), lambda i, j: (0, j)),
            pl.BlockSpec((1, tf), lambda i, j: (0, j)),
        ],
        out_specs=pl.BlockSpec((tm, tf), lambda i, j: (i, j)),
        out_shape=jax.ShapeDtypeStruct((m, D_FF_PAD), BF16),
        scratch_shapes=[pltpu.VMEM((tm + 2 * HALO, D_MODEL), BF16)],
        compiler_params=_params(("parallel", "arbitrary")),
        name=name,
    )(hn, hn, hn, hn_meta, wa, wg, wc, bc)


def _prep_weights(g_pre_mix, w_in, g_cq, w_uq, g_ckv, w_ukv, g_qn, g_kn, w_pa, w_pb, w_o,
                  g_post_mix, g_pre_ffn, w_up, w_conv, b_conv, w_down, g_post_ffn):
    perm_b = np.concatenate([np.arange(0, 32), np.arange(64, 96), np.arange(32, 64), np.arange(96, 128)])
    o = np.cumsum((0, MLA_Q_LORA, MLA_KV_LORA, MLA_ROPE, GQA_HEADS * HEAD_DIM, GQA_KV_HEADS * HEAD_DIM,
                   GQA_KV_HEADS * HEAD_DIM, D_MODEL, D_MODEL))
    w = w_in[0]
    cq, ckv, kr, gq, gk, gv, ga, gb = (w[:, o[i]:o[i + 1]] for i in range(8))
    gq = gq.reshape(D_MODEL, GQA_HEADS, HEAD_DIM)[:, :, perm_b].reshape(D_MODEL, -1)
    gk = gk.reshape(D_MODEL, GQA_KV_HEADS, HEAD_DIM)[:, :, perm_b].reshape(D_MODEL, -1)
    z32 = jnp.zeros((D_MODEL, 32), w.dtype)
    kr = jnp.concatenate([kr[:, :32], z32, kr[:, 32:], z32], axis=1)
    tail = jnp.zeros((D_MODEL, D_IN_PAD - OFF_KR - 128), w.dtype)
    w_in_p = jnp.concatenate([ga, gb, gq, cq, ckv, gk, gv, kr, tail], axis=1).astype(BF16)

    uq = w_uq[0].reshape(MLA_Q_LORA, MLA_HEADS, MLA_NOPE + MLA_ROPE)
    zq = jnp.zeros((MLA_Q_LORA, MLA_HEADS, 32), uq.dtype)
    uq = jnp.concatenate([uq[:, :, :MLA_NOPE], uq[:, :, MLA_NOPE:MLA_NOPE + 32], zq,
                          uq[:, :, MLA_NOPE + 32:], zq], axis=2).reshape(MLA_Q_LORA, -1).astype(BF16)
    ukv = w_ukv[0].reshape(MLA_KV_LORA, MLA_HEADS, MLA_NOPE + MLA_V)
    uk = ukv[:, :, :MLA_NOPE].reshape(MLA_KV_LORA, -1).astype(BF16)
    uv = ukv[:, :, MLA_NOPE:].reshape(MLA_KV_LORA, -1).astype(BF16)

    fpad = D_FF_PAD - D_FF
    up = w_up[0]
    return dict(
        g_pre_mix=g_pre_mix, w_in=w_in_p, g_cq=g_cq, w_uq=uq, g_ckv=g_ckv, w_uk=uk, w_uvt=uv.T,
        eye_kv=jnp.eye(GQA_KV_HEADS * HEAD_DIM, dtype=BF16),
        g_qn=g_qn[:, perm_b], g_kn=g_kn[:, perm_b],
        w_pa=w_pa[0].astype(BF16), w_pb=w_pb[0].astype(BF16), w_o=w_o[0].astype(BF16),
        g_post_mix=g_post_mix, g_pre_ffn=g_pre_ffn,
        w_up_a=jnp.pad(up[:, :D_FF], ((0, 0), (0, fpad))).astype(BF16),
        w_up_g=jnp.pad(up[:, D_FF:], ((0, 0), (0, fpad))).astype(BF16),
        w_conv=jnp.pad(w_conv[0], ((0, 0), (0, fpad))), b_conv=jnp.pad(b_conv, ((0, 0), (0, fpad))),
        w_down=jnp.pad(w_down[0], ((0, fpad), (0, 0))).astype(BF16), g_post_ffn=g_post_ffn,
    )


def _rope_tables(pos, dim):
    inv = ROPE_THETA ** (-jnp.arange(0, dim, 2, dtype=F32) / dim)
    ang = pos.astype(F32)[:, None] * inv[None, :]
    return jnp.cos(ang), jnp.sin(ang)


def _tables(pos, row, col):
    c1, s1 = _rope_tables(pos, MLA_ROPE)
    z = jnp.zeros_like(c1)
    cos_a = jnp.concatenate([c1, z, c1, z], axis=1)
    sin_a = jnp.concatenate([-s1, z, s1, z], axis=1)
    cr, sr = _rope_tables(row, HEAD_DIM // 2)
    cc, sc = _rope_tables(col, HEAD_DIM // 2)
    cos_b = jnp.concatenate([cr, cc, cr, cc], axis=1)
    sin_b = jnp.concatenate([-sr, -sc, sr, sc], axis=1)
    return cos_a, sin_a, cos_b, sin_b


def kernel(x_prompt, x_sample, meta_tokens, g_pre_mix, w_in, g_cq, w_uq, g_ckv, w_ukv, g_qn, g_kn,
           w_pa, w_pb, w_o, g_post_mix, g_pre_ffn, w_up, w_conv, b_conv, w_down, g_post_ffn):
    w = _prep_weights(g_pre_mix, w_in, g_cq, w_uq, g_ckv, w_ukv, g_qn, g_kn, w_pa, w_pb, w_o,
                      g_post_mix, g_pre_ffn, w_up, w_conv, b_conv, w_down, g_post_ffn)
    t = jnp.arange(SEQ, dtype=jnp.int32)
    tabs_real = _tables(t + N_META, t // GRID_W, t % GRID_W)
    zero = jnp.zeros((N_META,), jnp.int32)
    tabs_meta = _tables(jnp.arange(N_META), zero, zero)
    return (_trunk(x_prompt, meta_tokens, w, tabs_real, tabs_meta, "p"),
            _trunk(x_sample, meta_tokens, w, tabs_real, tabs_meta, "s"))


def _trunk(x, meta_tokens, w, tabs_real, tabs_meta, tag):
    batch = x.shape[0]
    xr = x.reshape(batch * SEQ, D_MODEL)
    xm = jnp.broadcast_to(meta_tokens[None], (batch, N_META, D_MODEL)).reshape(batch * N_META, D_MODEL)
    mrows = batch * N_META
    tabs_meta = tuple(jnp.tile(a, (batch, 1)) for a in tabs_meta)

    z = _matmul(_norm_rows(xr, w["g_pre_mix"], 256, "norm_" + tag), w["w_in"], 1024, 1024, BF16,
                "in_proj_" + tag)
    zm = _matmul(_norm_rows(xm, w["g_pre_mix"], mrows, "norm_meta_" + tag), w["w_in"], mrows, 1024, BF16,
                 "in_proj_meta_" + tag)
    qa, ka, vat, qb, kb, vbt = _heads(z, w, tabs_real, 256, SEQ // 256, "heads_" + tag)
    qam, kam, vatm, qbm, kbm, vbtm = _heads(zm, w, tabs_meta, mrows, 1, "heads_meta_" + tag)

    def per_batch(t):
        return t.reshape(t.shape[0], batch, N_META).transpose(1, 0, 2)

    def token_major(t):
        return t.transpose(0, 2, 1).reshape(batch * N_META, t.shape[1])

    vatm, vbtm = per_batch(vatm), per_batch(vbtm)

    cfg_a = dict(batch=batch, heads=MLA_HEADS, group=1, dk=MLA_QK)
    cfg_b = dict(batch=batch, heads=GQA_HEADS, group=GQA_GROUP, dk=HEAD_DIM)
    oa = _attention(qa, ka, vat, kam, vatm, lq=SEQ, tq=1024, tk=2048, name="attn_a_" + tag, **cfg_a)
    ob = _attention(qb, kb, vbt, kbm, vbtm, lq=SEQ, tq=1024, tk=2048, name="attn_b_" + tag, **cfg_b)
    oam = token_major(_attention(qam, ka, vat, kam, vatm, lq=N_META, tq=N_META, tk=1024,
                                 name="attn_a_meta_" + tag, **cfg_a))
    obm = token_major(_attention(qbm, kb, vbt, kbm, vbtm, lq=N_META, tq=N_META, tk=1024,
                                 name="attn_b_meta_" + tag, **cfg_b))

    mg = _merge(oa, ob, z, w["w_pa"], w["w_pb"], 1024, 1024, "merge_" + tag)
    mgm = _merge(oam, obm, zm, w["w_pa"], w["w_pb"], mrows, 1024, "merge_meta_" + tag)
    h1, hn = _proj_residual(mg, w["w_o"], xr, w["g_post_mix"], w["g_pre_ffn"], 512, 512, "out_proj_" + tag)
    _, hnm = _proj_residual(mgm, w["w_o"], xm, w["g_post_mix"], w["g_pre_ffn"], mrows, 512,
                            "out_proj_meta_" + tag)

    f = _ffn_up(hn, hnm, w["w_up_a"], w["w_up_g"], w["w_conv"], w["b_conv"], 1024, 512, "ffn_up_" + tag)
    y = _proj_residual(f, w["w_down"], h1, w["g_post_ffn"], None, 512, 1024, "ffn_down_" + tag)
    return y.reshape(batch, SEQ, D_MODEL)
```

```python
import functools
import math

import jax
import jax.numpy as jnp
import numpy as np
from jax import lax
from jax.experimental import pallas as pl
from jax.experimental.pallas import tpu as pltpu

F32 = jnp.float32
BF16 = jnp.bfloat16

D_MODEL = 4096
SEQ = 4096
N_META = 16
GRID_W = 64
HEAD_DIM = 128
MLA_HEADS = 16
MLA_Q_LORA = 1024
MLA_KV_LORA = 512
MLA_NOPE = 128
MLA_ROPE = 64
MLA_V = 128
MLA_QK = 256
GQA_HEADS = 16
GQA_KV_HEADS = 4
GQA_GROUP = GQA_HEADS // GQA_KV_HEADS
D_FF = 11008
D_FF_PAD = 11264
ROPE_THETA = 10000.0
EPS = 1e-6
LOG2E = math.log2(math.e)

OFF_GA = 0
OFF_GB = 4096
OFF_GQ = 8192
OFF_CQ = 10240
OFF_CKV = 11264
OFF_GK = 11776
OFF_GV = 12288
OFF_KR = 12800
D_IN_PAD = 13312

VMEM_LIMIT = 60 * 1024 * 1024


def _params(sem, flags=None):
    return pltpu.CompilerParams(dimension_semantics=sem, vmem_limit_bytes=VMEM_LIMIT, flags=flags)


def _rms(x, g):
    return x * lax.rsqrt(jnp.mean(x * x, axis=-1, keepdims=True) + EPS) * g


def _norm_kernel(x_ref, g_ref, o_ref):
    o_ref[...] = _rms(x_ref[...], g_ref[...]).astype(o_ref.dtype)


def _norm_rows(x, g, tm, name):
    m = x.shape[0]
    return pl.pallas_call(
        _norm_kernel,
        grid=(m // tm,),
        in_specs=[pl.BlockSpec((tm, D_MODEL), lambda i: (i, 0)), pl.BlockSpec((1, D_MODEL), lambda i: (0, 0))],
        out_specs=pl.BlockSpec((tm, D_MODEL), lambda i: (i, 0)),
        out_shape=jax.ShapeDtypeStruct((m, D_MODEL), BF16),
        compiler_params=_params(("parallel",)),
        name=name,
    )(x, g)


def _mm_kernel(a_ref, b_ref, o_ref):
    o_ref[...] = jnp.dot(a_ref[...], b_ref[...], preferred_element_type=F32).astype(o_ref.dtype)


def _matmul(a, b, tm, tn, out_dtype, name):
    m, k = a.shape
    n = b.shape[1]
    return pl.pallas_call(
        _mm_kernel,
        grid=(m // tm, n // tn),
        in_specs=[pl.BlockSpec((tm, k), lambda i, j: (i, 0)), pl.BlockSpec((k, tn), lambda i, j: (0, j))],
        out_specs=pl.BlockSpec((tm, tn), lambda i, j: (i, j)),
        out_shape=jax.ShapeDtypeStruct((m, n), out_dtype),
        compiler_params=_params(("parallel", "parallel")),
        name=name,
    )(a, b)


def _rope(x, cos, sin_signed):
    lane = lax.broadcasted_iota(jnp.int32, x.shape, 1)
    partner = jnp.where((lane & 32) == 0, pltpu.roll(x, 96, 1), pltpu.roll(x, 32, 1))
    return x * cos + partner * sin_signed


NT_DIMS = (((1,), (1,)), ((), ()))


def _heads_kernel(cq_ref, ckv_ref, kr_ref, gq_ref, gk_ref, gv_ref, wuq_ref, wuk_ref, wuvt_ref, eye_ref,
                  gcq_ref, gckv_ref, gqn_ref, gkn_ref, cosa_ref, sina_ref, cosb_ref, sinb_ref,
                  qa_ref, ka_ref, vat_ref, qb_ref, kb_ref, vbt_ref):
    cosa, sina = cosa_ref[...], sina_ref[...]
    cosb, sinb = cosb_ref[...], sinb_ref[...]
    scale_a = LOG2E / math.sqrt(MLA_NOPE + MLA_ROPE)
    scale_b = LOG2E / math.sqrt(HEAD_DIM)

    cq = _rms(cq_ref[...].astype(F32), gcq_ref[...]).astype(BF16)
    for h in range(MLA_HEADS):
        q = jnp.dot(cq, wuq_ref[:, h * MLA_QK:(h + 1) * MLA_QK], preferred_element_type=F32)
        qa_ref[:, h * MLA_QK:h * MLA_QK + MLA_NOPE] = (q[:, :MLA_NOPE] * scale_a).astype(BF16)
        qa_ref[:, h * MLA_QK + MLA_NOPE:(h + 1) * MLA_QK] = (
            _rope(q[:, MLA_NOPE:], cosa, sina) * scale_a).astype(BF16)

    ckv = _rms(ckv_ref[...].astype(F32), gckv_ref[...]).astype(BF16)
    k_rope = _rope(kr_ref[...].astype(F32), cosa, sina).astype(BF16)
    k_nope = jnp.dot(ckv, wuk_ref[...], preferred_element_type=F32).astype(BF16)
    for h in range(MLA_HEADS):
        ka_ref[:, h * MLA_QK:h * MLA_QK + MLA_NOPE] = k_nope[:, h * MLA_NOPE:(h + 1) * MLA_NOPE]
        ka_ref[:, h * MLA_QK + MLA_NOPE:(h + 1) * MLA_QK] = k_rope
    vat_ref[...] = lax.dot_general(wuvt_ref[...], ckv, NT_DIMS, preferred_element_type=F32).astype(BF16)
    vbt_ref[...] = lax.dot_general(eye_ref[...], gv_ref[...], NT_DIMS, preferred_element_type=F32).astype(BF16)

    gqn, gkn = gqn_ref[...], gkn_ref[...]
    for h in range(GQA_HEADS):
        x = gq_ref[:, h * HEAD_DIM:(h + 1) * HEAD_DIM].astype(F32)
        qb_ref[:, h * HEAD_DIM:(h + 1) * HEAD_DIM] = (_rope(_rms(x, gqn), cosb, sinb) * scale_b).astype(BF16)
    for h in range(GQA_KV_HEADS):
        x = gk_ref[:, h * HEAD_DIM:(h + 1) * HEAD_DIM].astype(F32)
        kb_ref[:, h * HEAD_DIM:(h + 1) * HEAD_DIM] = _rope(_rms(x, gkn), cosb, sinb).astype(BF16)


def _heads(z, w, tabs, tm, tab_blocks, name):
    m = z.shape[0]

    def zspec(width, off):
        return pl.BlockSpec((tm, width), lambda i: (i, off // width))

    def full(a):
        return pl.BlockSpec(a.shape, lambda i: (0,) * a.ndim)

    tab_spec = pl.BlockSpec((tm, 128), lambda i: (i % tab_blocks, 0))
    outs = ((MLA_HEADS * MLA_QK, True), (MLA_HEADS * MLA_QK, True), (MLA_HEADS * MLA_V, False),
            (GQA_HEADS * HEAD_DIM, True), (GQA_KV_HEADS * HEAD_DIM, True), (GQA_KV_HEADS * HEAD_DIM, False))
    consts = (w["w_uq"], w["w_uk"], w["w_uvt"], w["eye_kv"], w["g_cq"], w["g_ckv"], w["g_qn"], w["g_kn"])
    return pl.pallas_call(
        _heads_kernel,
        grid=(m // tm,),
        in_specs=[zspec(MLA_Q_LORA, OFF_CQ), zspec(MLA_KV_LORA, OFF_CKV), zspec(128, OFF_KR),
                  zspec(GQA_HEADS * HEAD_DIM, OFF_GQ), zspec(GQA_KV_HEADS * HEAD_DIM, OFF_GK),
                  zspec(GQA_KV_HEADS * HEAD_DIM, OFF_GV)]
        + [full(a) for a in consts] + [tab_spec] * 4,
        out_specs=[pl.BlockSpec((tm, n), lambda i: (i, 0)) if rows else pl.BlockSpec((n, tm), lambda i: (0, i))
                   for n, rows in outs],
        out_shape=[jax.ShapeDtypeStruct((m, n) if rows else (n, m), BF16) for n, rows in outs],
        compiler_params=_params(("parallel",)),
        name=name,
    )(z, z, z, z, z, z, *consts, *tabs)


def _attn_kernel(tk, transpose_out, q_ref, k_ref, vt_ref, km_ref, vmt_ref, o_ref):
    q = q_ref[...]
    s = lax.dot_general(km_ref[...], q, NT_DIMS, preferred_element_type=F32)
    m = jnp.max(s, axis=0, keepdims=True)
    p = jnp.exp2(s - m)
    l = jnp.sum(p, axis=0, keepdims=True)
    acc = jnp.dot(vmt_ref[...], p.astype(BF16), preferred_element_type=F32)

    def scores(c):
        return lax.dot_general(k_ref[c * tk:(c + 1) * tk, :], q, NT_DIMS, preferred_element_type=F32)

    n_chunks = k_ref.shape[0] // tk
    s_next = scores(0)
    for c in range(n_chunks):
        s = s_next
        if c + 1 < n_chunks:
            s_next = scores(c + 1)
        m_new = jnp.maximum(m, jnp.max(s, axis=0, keepdims=True))
        alpha = jnp.exp2(m - m_new)
        p = jnp.exp2(s - m_new)
        l = alpha * l + jnp.sum(p, axis=0, keepdims=True)
        acc = alpha * acc + jnp.dot(vt_ref[:, c * tk:(c + 1) * tk], p.astype(BF16), preferred_element_type=F32)
        m = m_new
    o = acc / l
    o_ref[...] = (o.T if transpose_out else o).astype(o_ref.dtype)


def _attention(q, k, vt, km, vmt, *, batch, heads, group, dk, lq, tq, tk, name):
    nq = lq // tq
    dv = HEAD_DIM
    transpose_out = tq % LANES == 0
    if transpose_out:
        out_spec = pl.BlockSpec((tq, dv), lambda b, h, i: (b * nq + i, h))
        out_shape = jax.ShapeDtypeStruct((batch * lq, heads * dv), BF16)
    else:
        out_spec = pl.BlockSpec((None, dv, tq), lambda b, h, i: (b, h, i))
        out_shape = jax.ShapeDtypeStruct((batch, heads * dv, lq), BF16)
    return pl.pallas_call(
        functools.partial(_attn_kernel, tk, transpose_out),
        grid=(batch, heads, nq),
        in_specs=[
            pl.BlockSpec((tq, dk), lambda b, h, i: (b * nq + i, h)),
            pl.BlockSpec((SEQ, dk), lambda b, h, i: (b, h // group)),
            pl.BlockSpec((dv, SEQ), lambda b, h, i: (h // group, b)),
            pl.BlockSpec((N_META, dk), lambda b, h, i: (b, h // group)),
            pl.BlockSpec((None, dv, N_META), lambda b, h, i: (b, h // group, 0)),
        ],
        out_specs=out_spec,
        out_shape=out_shape,
        compiler_params=_params(("parallel", "parallel", "parallel")),
        name=name,
    )(q, k, vt, km, vmt)


def _merge_kernel(oa_ref, ob_ref, ga_ref, gb_ref, wpa_ref, wpb_ref, o_ref):
    a = jnp.dot(oa_ref[...], wpa_ref[...], preferred_element_type=F32)
    b = jnp.dot(ob_ref[...], wpb_ref[...], preferred_element_type=F32)
    ga = jax.nn.sigmoid(ga_ref[...].astype(F32))
    gb = jax.nn.sigmoid(gb_ref[...].astype(F32))
    o_ref[...] = (ga * a + gb * b).astype(o_ref.dtype)


def _merge(oa, ob, z, wpa, wpb, tm, tn, name):
    m, k = oa.shape
    return pl.pallas_call(
        _merge_kernel,
        grid=(m // tm, D_MODEL // tn),
        in_specs=[
            pl.BlockSpec((tm, k), lambda i, j: (i, 0)),
            pl.BlockSpec((tm, k), lambda i, j: (i, 0)),
            pl.BlockSpec((tm, tn), lambda i, j: (i, OFF_GA // tn + j)),
            pl.BlockSpec((tm, tn), lambda i, j: (i, OFF_GB // tn + j)),
            pl.BlockSpec((k, tn), lambda i, j: (0, j)),
            pl.BlockSpec((k, tn), lambda i, j: (0, j)),
        ],
        out_specs=pl.BlockSpec((tm, tn), lambda i, j: (i, j)),
        out_shape=jax.ShapeDtypeStruct((m, D_MODEL), BF16),
        compiler_params=_params(("parallel", "parallel")),
        name=name,
    )(oa, ob, z, z, wpa, wpb)


PROJ_CHUNK = 512
NORM_ROWS = 128


def _proj_residual_kernel(nk, a_ref, w_ref, res_ref, g_ref, g2_ref, h_ref, hn_ref):
    kk = pl.program_id(1)

    def accumulate(first):
        a = a_ref[...]
        for c in range(0, D_MODEL, PROJ_CHUNK):
            part = jnp.dot(a, w_ref[:, c:c + PROJ_CHUNK], preferred_element_type=F32)
            if first:
                h_ref[:, c:c + PROJ_CHUNK] = part
            else:
                h_ref[:, c:c + PROJ_CHUNK] += part

    pl.when(kk == 0)(functools.partial(accumulate, True))
    pl.when(kk > 0)(functools.partial(accumulate, False))

    @pl.when(kk == nk - 1)
    def _():
        for r in range(0, h_ref.shape[0], NORM_ROWS):
            rows = slice(r, min(r + NORM_ROWS, h_ref.shape[0]))
            h = res_ref[rows, :] + _rms(h_ref[rows, :], g_ref[...])
            h_ref[rows, :] = h
            if hn_ref is not None:
                hn_ref[rows, :] = _rms(h, g2_ref[...]).astype(hn_ref.dtype)


def _proj_residual_nohn_kernel(nk, a_ref, w_ref, res_ref, g_ref, h_ref):
    _proj_residual_kernel(nk, a_ref, w_ref, res_ref, g_ref, None, h_ref, None)


def _proj_residual(a, w, res, g, g2, tm, tk, name):
    m, k = a.shape
    nk = k // tk
    gspec = pl.BlockSpec((1, D_MODEL), lambda i, kk: (0, 0))
    row_spec = pl.BlockSpec((tm, D_MODEL), lambda i, kk: (i, 0))
    in_specs = [pl.BlockSpec((tm, tk), lambda i, kk: (i, kk)),
                pl.BlockSpec((tk, D_MODEL), lambda i, kk: (kk, 0)), row_spec, gspec]
    if g2 is None:
        return pl.pallas_call(
            functools.partial(_proj_residual_nohn_kernel, nk),
            grid=(m // tm, nk), in_specs=in_specs, out_specs=row_spec,
            out_shape=jax.ShapeDtypeStruct((m, D_MODEL), F32),
            compiler_params=_params(("parallel", "arbitrary")), name=name,
        )(a, w, res, g)
    return pl.pallas_call(
        functools.partial(_proj_residual_kernel, nk),
        grid=(m // tm, nk), in_specs=in_specs + [gspec], out_specs=[row_spec, row_spec],
        out_shape=[jax.ShapeDtypeStruct((m, D_MODEL), F32), jax.ShapeDtypeStruct((m, D_MODEL), BF16)],
        compiler_params=_params(("parallel", "arbitrary")), name=name,
    )(a, w, res, g, g2)


HALO = 16
LANES = 128
MXU_N = 256


def _ffn_up_kernel(tm, tiles_per_seq, hn_ref, prev_ref, next_ref, hnm_ref, wa_ref, wg_ref, wc_ref, bc_ref,
                   f_ref, ext_ref):
    i = pl.program_id(0)

    @pl.when(pl.program_id(1) == 0)
    def _():
        first = i % tiles_per_seq == 0
        last = i % tiles_per_seq == tiles_per_seq - 1
        ext_ref[0:HALO, :] = jnp.where(first, hnm_ref[...], prev_ref[...])
        ext_ref[HALO:HALO + tm, :] = hn_ref[...]
        ext_ref[HALO + tm:, :] = jnp.where(last, jnp.zeros_like(next_ref), next_ref[...])

    wc = wc_ref[...]
    bc = bc_ref[...]
    for c in range(0, f_ref.shape[1], MXU_N):
        cols = slice(c, c + MXU_N)
        a = jnp.dot(ext_ref[...], wa_ref[:, cols], preferred_element_type=F32)
        gate = jnp.dot(ext_ref[HALO:HALO + tm, :], wg_ref[:, cols], preferred_element_type=F32)
        conv = (a[HALO - 1:HALO - 1 + tm] * wc[0:1, cols] + a[HALO:HALO + tm] * wc[1:2, cols]
                + a[HALO + 1:HALO + 1 + tm] * wc[2:3, cols] + bc[:, cols])
        f_ref[:, cols] = (jax.nn.gelu(conv) * gate).astype(f_ref.dtype)


def _ffn_up(hn, hn_meta, wa, wg, wc, bc, tm, tf, name):
    m = hn.shape[0]
    tiles_per_seq = SEQ // tm
    hb = tm // HALO
    last = m // HALO - 1
    return pl.pallas_call(
        functools.partial(_ffn_up_kernel, tm, tiles_per_seq),
        grid=(m // tm, D_FF_PAD // tf),
        in_specs=[
            pl.BlockSpec((tm, D_MODEL), lambda i, j: (i, 0)),
            pl.BlockSpec((HALO, D_MODEL), lambda i, j: (jnp.maximum(i * hb - 1, 0), 0)),
            pl.BlockSpec((HALO, D_MODEL), lambda i, j: (jnp.minimum((i + 1) * hb, last), 0)),
            pl.BlockSpec((N_META, D_MODEL), lambda i, j: (i // tiles_per_seq, 0)),
            pl.BlockSpec((D_MODEL, tf), lambda i, j: (0, j)),
            pl.BlockSpec((D_MODEL, tf), lambda i, j: (0, j)),
            pl.BlockSpec((3, tf), lambda i, j: (0, j)),
            pl.BlockSpec((1, tf), lambda i, j: (0, j)),
        ],
        out_specs=pl.BlockSpec((tm, tf), lambda i, j: (i, j)),
        out_shape=jax.ShapeDtypeStruct((m, D_FF_PAD), BF16),
        scratch_shapes=[pltpu.VMEM((tm + 2 * HALO, D_MODEL), BF16)],
        compiler_params=_params(("parallel", "arbitrary")),
        name=name,
    )(hn, hn, hn, hn_meta, wa, wg, wc, bc)


def _prep_weights(g_pre_mix, w_in, g_cq, w_uq, g_ckv, w_ukv, g_qn, g_kn, w_pa, w_pb, w_o,
                  g_post_mix, g_pre_ffn, w_up, w_conv, b_conv, w_down, g_post_ffn):
    o = np.cumsum((0, MLA_Q_LORA, MLA_KV_LORA, MLA_ROPE, GQA_HEADS * HEAD_DIM, GQA_KV_HEADS * HEAD_DIM,
                   GQA_KV_HEADS * HEAD_DIM, D_MODEL, D_MODEL))
    w = w_in[0]
    cq, ckv, kr, gq, gk, gv, ga, gb = (w[:, o[i]:o[i + 1]] for i in range(8))
    tail = jnp.zeros((D_MODEL, D_IN_PAD - OFF_KR - MLA_ROPE), w.dtype)
    w_in_p = jnp.concatenate([ga, gb, gq, cq, ckv, gk, gv, kr, tail], axis=1).astype(BF16)

    uq = w_uq[0].reshape(MLA_Q_LORA, MLA_HEADS, MLA_NOPE + MLA_ROPE)
    zq = jnp.zeros((MLA_Q_LORA, MLA_HEADS, MLA_QK - MLA_NOPE - MLA_ROPE), uq.dtype)
    uq = jnp.concatenate([uq, zq], axis=2).reshape(MLA_Q_LORA, -1).astype(BF16)
    ukv = w_ukv[0].reshape(MLA_KV_LORA, MLA_HEADS, MLA_NOPE + MLA_V)
    uk = ukv[:, :, :MLA_NOPE].reshape(MLA_KV_LORA, -1).astype(BF16)
    uv = ukv[:, :, MLA_NOPE:].reshape(MLA_KV_LORA, -1).astype(BF16)

    fpad = D_FF_PAD - D_FF
    up = w_up[0]
    return dict(
        g_pre_mix=g_pre_mix, w_in=w_in_p, g_cq=g_cq, w_uq=uq, g_ckv=g_ckv, w_uk=uk, w_uvt=uv.T,
        eye_kv=jnp.eye(GQA_KV_HEADS * HEAD_DIM, dtype=BF16),
        g_qn=g_qn, g_kn=g_kn,
        w_pa=w_pa[0].astype(BF16), w_pb=w_pb[0].astype(BF16), w_o=w_o[0].astype(BF16),
        g_post_mix=g_post_mix, g_pre_ffn=g_pre_ffn,
        w_up_a=jnp.concatenate([up[:, :D_FF], jnp.zeros((D_MODEL, fpad), up.dtype)], axis=1).astype(BF16),
        w_up_g=jnp.concatenate([up[:, D_FF:], jnp.zeros((D_MODEL, fpad), up.dtype)], axis=1).astype(BF16),
        w_conv=jnp.pad(w_conv[0], ((0, 0), (0, fpad))), b_conv=jnp.pad(b_conv, ((0, 0), (0, fpad))),
        w_down=jnp.concatenate([w_down[0], jnp.zeros((fpad, D_MODEL), up.dtype)], axis=0).astype(BF16),
        g_post_ffn=g_post_ffn,
    )


def _rope_tables(pos, dim):
    inv = ROPE_THETA ** (-jnp.arange(0, dim, 2, dtype=F32) / dim)
    ang = pos.astype(F32)[:, None] * inv[None, :]
    return jnp.cos(ang), jnp.sin(ang)


def _tables(pos, row, col):
    c1, s1 = _rope_tables(pos, MLA_ROPE)
    z = jnp.zeros_like(c1)
    cos_a = jnp.concatenate([c1, c1, z, z], axis=1)
    sin_a = jnp.concatenate([-s1, s1, z, z], axis=1)
    cr, sr = _rope_tables(row, HEAD_DIM // 2)
    cc, sc = _rope_tables(col, HEAD_DIM // 2)
    cos_b = jnp.concatenate([cr, cr, cc, cc], axis=1)
    sin_b = jnp.concatenate([-sr, sr, -sc, sc], axis=1)
    return cos_a, sin_a, cos_b, sin_b


def kernel(x_prompt, x_sample, meta_tokens, g_pre_mix, w_in, g_cq, w_uq, g_ckv, w_ukv, g_qn, g_kn,
           w_pa, w_pb, w_o, g_post_mix, g_pre_ffn, w_up, w_conv, b_conv, w_down, g_post_ffn):
    w = _prep_weights(g_pre_mix, w_in, g_cq, w_uq, g_ckv, w_ukv, g_qn, g_kn, w_pa, w_pb, w_o,
                      g_post_mix, g_pre_ffn, w_up, w_conv, b_conv, w_down, g_post_ffn)
    t = jnp.arange(SEQ, dtype=jnp.int32)
    tabs_real = _tables(t + N_META, t // GRID_W, t % GRID_W)
    zero = jnp.zeros((N_META,), jnp.int32)
    tabs_meta = _tables(jnp.arange(N_META), zero, zero)
    return (_trunk(x_prompt, meta_tokens, w, tabs_real, tabs_meta, "p"),
            _trunk(x_sample, meta_tokens, w, tabs_real, tabs_meta, "s"))


def _trunk(x, meta_tokens, w, tabs_real, tabs_meta, tag):
    batch = x.shape[0]
    xr = x.reshape(batch * SEQ, D_MODEL)
    xm = jnp.broadcast_to(meta_tokens[None], (batch, N_META, D_MODEL)).reshape(batch * N_META, D_MODEL)
    mrows = batch * N_META
    tabs_meta = tuple(jnp.tile(a, (batch, 1)) for a in tabs_meta)

    z = _matmul(_norm_rows(xr, w["g_pre_mix"], 256, "norm_" + tag), w["w_in"], 1024, 1024, BF16,
                "in_proj_" + tag)
    zm = _matmul(_norm_rows(xm, w["g_pre_mix"], mrows, "norm_meta_" + tag), w["w_in"], mrows, 1024, BF16,
                 "in_proj_meta_" + tag)
    qa, ka, vat, qb, kb, vbt = _heads(z, w, tabs_real, 256, SEQ // 256, "heads_" + tag)
    qam, kam, vatm, qbm, kbm, vbtm = _heads(zm, w, tabs_meta, mrows, 1, "heads_meta_" + tag)

    def per_batch(t):
        return t.reshape(t.shape[0], batch, N_META).transpose(1, 0, 2)

    def token_major(t):
        return t.transpose(0, 2, 1).reshape(batch * N_META, t.shape[1])

    vatm, vbtm = per_batch(vatm), per_batch(vbtm)

    cfg_a = dict(batch=batch, heads=MLA_HEADS, group=1, dk=MLA_QK)
    cfg_b = dict(batch=batch, heads=GQA_HEADS, group=GQA_GROUP, dk=HEAD_DIM)
    oa = _attention(qa, ka, vat, kam, vatm, lq=SEQ, tq=2048, tk=1024, name="attn_a_" + tag, **cfg_a)
    ob = _attention(qb, kb, vbt, kbm, vbtm, lq=SEQ, tq=2048, tk=1024, name="attn_b_" + tag, **cfg_b)
    oam = token_major(_attention(qam, ka, vat, kam, vatm, lq=N_META, tq=N_META, tk=1024,
                                 name="attn_a_meta_" + tag, **cfg_a))
    obm = token_major(_attention(qbm, kb, vbt, kbm, vbtm, lq=N_META, tq=N_META, tk=1024,
                                 name="attn_b_meta_" + tag, **cfg_b))

    mg = _merge(oa, ob, z, w["w_pa"], w["w_pb"], 1024, 1024, "merge_" + tag)
    mgm = _merge(oam, obm, zm, w["w_pa"], w["w_pb"], mrows, 1024, "merge_meta_" + tag)
    h1, hn = _proj_residual(mg, w["w_o"], xr, w["g_post_mix"], w["g_pre_ffn"], 512, 512, "out_proj_" + tag)
    _, hnm = _proj_residual(mgm, w["w_o"], xm, w["g_post_mix"], w["g_pre_ffn"], mrows, 512,
                            "out_proj_meta_" + tag)

    f = _ffn_up(hn, hnm, w["w_up_a"], w["w_up_g"], w["w_conv"], w["b_conv"], 1024, 512, "ffn_up_" + tag)
    y = _proj_residual(f, w["w_down"], h1, w["g_post_ffn"], None, 512, 1024, "ffn_down_" + tag)
    return y.reshape(batch, SEQ, D_MODEL)
```

```python
import functools
import math

import jax
import jax.numpy as jnp
import numpy as np
from jax import lax
from jax.experimental import pallas as pl
from jax.experimental.pallas import tpu as pltpu

F32 = jnp.float32
BF16 = jnp.bfloat16

D_MODEL = 4096
SEQ = 4096
N_META = 16
GRID_W = 64
HEAD_DIM = 128
MLA_HEADS = 16
MLA_Q_LORA = 1024
MLA_KV_LORA = 512
MLA_NOPE = 128
MLA_ROPE = 64
MLA_V = 128
MLA_QK = 256
GQA_HEADS = 16
GQA_KV_HEADS = 4
GQA_GROUP = GQA_HEADS // GQA_KV_HEADS
D_FF = 11008
D_FF_PAD = 11264
ROPE_THETA = 10000.0
EPS = 1e-6
LOG2E = math.log2(math.e)

OFF_GA = 0
OFF_GB = 4096
OFF_GQ = 8192
OFF_CQ = 10240
OFF_CKV = 11264
OFF_GK = 11776
OFF_GV = 12288
OFF_KR = 12800
D_IN_PAD = 13312

LANES = 128
MXU_N = 256
VMEM_LIMIT = 60 * 1024 * 1024


def _params(sem):
    return pltpu.CompilerParams(dimension_semantics=sem, vmem_limit_bytes=VMEM_LIMIT)


def _rms(x, g):
    return x * lax.rsqrt(jnp.mean(x * x, axis=-1, keepdims=True) + EPS) * g


def _norm_kernel(x_ref, g_ref, o_ref):
    o_ref[...] = _rms(x_ref[...], g_ref[...]).astype(o_ref.dtype)


def _norm_rows(x, g, tm, name):
    m = x.shape[0]
    return pl.pallas_call(
        _norm_kernel,
        grid=(m // tm,),
        in_specs=[pl.BlockSpec((tm, D_MODEL), lambda i: (i, 0)), pl.BlockSpec((1, D_MODEL), lambda i: (0, 0))],
        out_specs=pl.BlockSpec((tm, D_MODEL), lambda i: (i, 0)),
        out_shape=jax.ShapeDtypeStruct((m, D_MODEL), BF16),
        compiler_params=_params(("parallel",)),
        name=name,
    )(x, g)


def _mm_kernel(a_ref, b_ref, o_ref):
    o_ref[...] = jnp.dot(a_ref[...], b_ref[...], preferred_element_type=F32).astype(o_ref.dtype)


def _matmul(a, b, tm, tn, out_dtype, name):
    m, k = a.shape
    n = b.shape[1]
    return pl.pallas_call(
        _mm_kernel,
        grid=(m // tm, n // tn),
        in_specs=[pl.BlockSpec((tm, k), lambda i, j: (i, 0)), pl.BlockSpec((k, tn), lambda i, j: (0, j))],
        out_specs=pl.BlockSpec((tm, tn), lambda i, j: (i, j)),
        out_shape=jax.ShapeDtypeStruct((m, n), out_dtype),
        compiler_params=_params(("parallel", "parallel")),
        name=name,
    )(a, b)


NT_DIMS = (((1,), (1,)), ((), ()))


def _rope(x, cos, sin_signed):
    return x * cos + pltpu.roll(x, 64, 1) * sin_signed


def _heads_kernel(cq_ref, ckv_ref, kr_ref, gq_ref, gk_ref, gv_ref, wuq_ref, wuk_ref, wuvt_ref, eye_ref,
                  gcq_ref, gckv_ref, gqn_ref, gkn_ref, cosa_ref, sina_ref, cosb_ref, sinb_ref,
                  qa_ref, ka_ref, vat_ref, qb_ref, kb_ref, vbt_ref):
    cosa, sina = cosa_ref[...], sina_ref[...]
    cosb, sinb = cosb_ref[...], sinb_ref[...]
    scale_a = LOG2E / math.sqrt(MLA_NOPE + MLA_ROPE)
    scale_b = LOG2E / math.sqrt(HEAD_DIM)

    cq = _rms(cq_ref[...].astype(F32), gcq_ref[...]).astype(BF16)
    for h in range(MLA_HEADS):
        q = jnp.dot(cq, wuq_ref[:, h * MLA_QK:(h + 1) * MLA_QK], preferred_element_type=F32)
        qa_ref[:, h * MLA_QK:h * MLA_QK + MLA_NOPE] = (q[:, :MLA_NOPE] * scale_a).astype(BF16)
        qa_ref[:, h * MLA_QK + MLA_NOPE:(h + 1) * MLA_QK] = (
            _rope(q[:, MLA_NOPE:], cosa, sina) * scale_a).astype(BF16)

    ckv = _rms(ckv_ref[...].astype(F32), gckv_ref[...]).astype(BF16)
    k_rope = _rope(kr_ref[...].astype(F32), cosa, sina).astype(BF16)
    k_nope = jnp.dot(ckv, wuk_ref[...], preferred_element_type=F32).astype(BF16)
    for h in range(MLA_HEADS):
        ka_ref[:, h * MLA_QK:h * MLA_QK + MLA_NOPE] = k_nope[:, h * MLA_NOPE:(h + 1) * MLA_NOPE]
        ka_ref[:, h * MLA_QK + MLA_NOPE:(h + 1) * MLA_QK] = k_rope
    vat_ref[...] = lax.dot_general(wuvt_ref[...], ckv, NT_DIMS, preferred_element_type=F32).astype(BF16)
    vbt_ref[...] = lax.dot_general(eye_ref[...], gv_ref[...], NT_DIMS, preferred_element_type=F32).astype(BF16)

    gqn, gkn = gqn_ref[...], gkn_ref[...]
    for h in range(GQA_HEADS):
        x = gq_ref[:, h * HEAD_DIM:(h + 1) * HEAD_DIM].astype(F32)
        qb_ref[:, h * HEAD_DIM:(h + 1) * HEAD_DIM] = (_rope(_rms(x, gqn), cosb, sinb) * scale_b).astype(BF16)
    for h in range(GQA_KV_HEADS):
        x = gk_ref[:, h * HEAD_DIM:(h + 1) * HEAD_DIM].astype(F32)
        kb_ref[:, h * HEAD_DIM:(h + 1) * HEAD_DIM] = _rope(_rms(x, gkn), cosb, sinb).astype(BF16)


def _heads(z, w, tabs, tm, tab_blocks, name):
    m = z.shape[0]

    def zspec(width, off):
        return pl.BlockSpec((tm, width), lambda i: (i, off // width))

    def full(a):
        return pl.BlockSpec(a.shape, lambda i: (0,) * a.ndim)

    tab_spec = pl.BlockSpec((tm, 128), lambda i: (i % tab_blocks, 0))
    outs = ((MLA_HEADS * MLA_QK, True), (MLA_HEADS * MLA_QK, True), (MLA_HEADS * MLA_V, False),
            (GQA_HEADS * HEAD_DIM, True), (GQA_KV_HEADS * HEAD_DIM, True), (GQA_KV_HEADS * HEAD_DIM, False))
    consts = (w["w_uq"], w["w_uk"], w["w_uvt"], w["eye_kv"], w["g_cq"], w["g_ckv"], w["g_qn"], w["g_kn"])
    return pl.pallas_call(
        _heads_kernel,
        grid=(m // tm,),
        in_specs=[zspec(MLA_Q_LORA, OFF_CQ), zspec(MLA_KV_LORA, OFF_CKV), zspec(128, OFF_KR),
                  zspec(GQA_HEADS * HEAD_DIM, OFF_GQ), zspec(GQA_KV_HEADS * HEAD_DIM, OFF_GK),
                  zspec(GQA_KV_HEADS * HEAD_DIM, OFF_GV)]
        + [full(a) for a in consts] + [tab_spec] * 4,
        out_specs=[pl.BlockSpec((tm, n), lambda i: (i, 0)) if rows else pl.BlockSpec((n, tm), lambda i: (0, i))
                   for n, rows in outs],
        out_shape=[jax.ShapeDtypeStruct((m, n) if rows else (n, m), BF16) for n, rows in outs],
        compiler_params=_params(("parallel",)),
        name=name,
    )(z, z, z, z, z, z, *consts, *tabs)


def _attn_kernel(tk, transpose_out, q_ref, k_ref, vt_ref, km_ref, vmt_ref, o_ref):
    q = q_ref[...]
    s = lax.dot_general(km_ref[...], q, NT_DIMS, preferred_element_type=F32)
    m = jnp.max(s, axis=0, keepdims=True)
    p = jnp.exp2(s - m)
    l = jnp.sum(p, axis=0, keepdims=True)
    acc = jnp.dot(vmt_ref[...], p.astype(BF16), preferred_element_type=F32)

    def scores(c):
        return lax.dot_general(k_ref[c * tk:(c + 1) * tk, :], q, NT_DIMS, preferred_element_type=F32)

    n_chunks = k_ref.shape[0] // tk
    s_next = scores(0)
    for c in range(n_chunks):
        s = s_next
        if c + 1 < n_chunks:
            s_next = scores(c + 1)
        m_new = jnp.maximum(m, jnp.max(s, axis=0, keepdims=True))
        alpha = jnp.exp2(m - m_new)
        p = jnp.exp2(s - m_new)
        l = alpha * l + jnp.sum(p, axis=0, keepdims=True)
        acc = alpha * acc + jnp.dot(vt_ref[:, c * tk:(c + 1) * tk], p.astype(BF16), preferred_element_type=F32)
        m = m_new
    o = acc / l
    o_ref[...] = (o.T if transpose_out else o).astype(o_ref.dtype)


def _attention(q, k, vt, km, vmt, *, batch, heads, group, dk, lq, tq, tk, name):
    nq = lq // tq
    dv = HEAD_DIM
    transpose_out = tq % LANES == 0
    if transpose_out:
        out_spec = pl.BlockSpec((tq, dv), lambda b, h, i: (b * nq + i, h))
        out_shape = jax.ShapeDtypeStruct((batch * lq, heads * dv), BF16)
    else:
        out_spec = pl.BlockSpec((None, dv, tq), lambda b, h, i: (b, h, i))
        out_shape = jax.ShapeDtypeStruct((batch, heads * dv, lq), BF16)
    return pl.pallas_call(
        functools.partial(_attn_kernel, tk, transpose_out),
        grid=(batch, heads, nq),
        in_specs=[
            pl.BlockSpec((tq, dk), lambda b, h, i: (b * nq + i, h)),
            pl.BlockSpec((SEQ, dk), lambda b, h, i: (b, h // group)),
            pl.BlockSpec((dv, SEQ), lambda b, h, i: (h // group, b)),
            pl.BlockSpec((N_META, dk), lambda b, h, i: (b, h // group)),
            pl.BlockSpec((None, dv, N_META), lambda b, h, i: (b, h // group, 0)),
        ],
        out_specs=out_spec,
        out_shape=out_shape,
        compiler_params=_params(("parallel", "parallel", "parallel")),
        name=name,
    )(q, k, vt, km, vmt)


def _merge_kernel(oa_ref, ob_ref, ga_ref, gb_ref, wpa_ref, wpb_ref, o_ref):
    a = jnp.dot(oa_ref[...], wpa_ref[...], preferred_element_type=F32)
    b = jnp.dot(ob_ref[...], wpb_ref[...], preferred_element_type=F32)
    ga = jax.nn.sigmoid(ga_ref[...].astype(F32))
    gb = jax.nn.sigmoid(gb_ref[...].astype(F32))
    o_ref[...] = (ga * a + gb * b).astype(o_ref.dtype)


def _merge(oa, ob, z, wpa, wpb, tm, tn, name):
    m, k = oa.shape
    return pl.pallas_call(
        _merge_kernel,
        grid=(m // tm, D_MODEL // tn),
        in_specs=[
            pl.BlockSpec((tm, k), lambda i, j: (i, 0)),
            pl.BlockSpec((tm, k), lambda i, j: (i, 0)),
            pl.BlockSpec((tm, tn), lambda i, j: (i, OFF_GA // tn + j)),
            pl.BlockSpec((tm, tn), lambda i, j: (i, OFF_GB // tn + j)),
            pl.BlockSpec((k, tn), lambda i, j: (0, j)),
            pl.BlockSpec((k, tn), lambda i, j: (0, j)),
        ],
        out_specs=pl.BlockSpec((tm, tn), lambda i, j: (i, j)),
        out_shape=jax.ShapeDtypeStruct((m, D_MODEL), BF16),
        compiler_params=_params(("parallel", "parallel")),
        name=name,
    )(oa, ob, z, z, wpa, wpb)


PROJ_CHUNK = 512
NORM_ROWS = 128


def _norm_residual(res_ref, g_ref, g2_ref, h_ref, hn_ref):
    for r in range(0, h_ref.shape[0], NORM_ROWS):
        rows = slice(r, min(r + NORM_ROWS, h_ref.shape[0]))
        h = res_ref[rows, :] + _rms(h_ref[rows, :], g_ref[...])
        h_ref[rows, :] = h
        if hn_ref is not None:
            hn_ref[rows, :] = _rms(h, g2_ref[...]).astype(hn_ref.dtype)


def _proj_residual_kernel(nk, a_ref, w_ref, res_ref, g_ref, g2_ref, h_ref, hn_ref):
    kk = pl.program_id(1)

    def accumulate(first):
        a = a_ref[...]
        for c in range(0, D_MODEL, PROJ_CHUNK):
            part = jnp.dot(a, w_ref[:, c:c + PROJ_CHUNK], preferred_element_type=F32)
            if first:
                h_ref[:, c:c + PROJ_CHUNK] = part
            else:
                h_ref[:, c:c + PROJ_CHUNK] += part

    pl.when(kk == 0)(functools.partial(accumulate, True))
    pl.when(kk > 0)(functools.partial(accumulate, False))
    pl.when(kk == nk - 1)(functools.partial(_norm_residual, res_ref, g_ref, g2_ref, h_ref, hn_ref))


def _proj_residual_nohn_kernel(nk, a_ref, w_ref, res_ref, g_ref, h_ref):
    _proj_residual_kernel(nk, a_ref, w_ref, res_ref, g_ref, None, h_ref, None)


def _proj_residual(a, w, res, g, g2, tm, tk, name):
    m, k = a.shape
    nk = k // tk
    gspec = pl.BlockSpec((1, D_MODEL), lambda i, kk: (0, 0))
    row_spec = pl.BlockSpec((tm, D_MODEL), lambda i, kk: (i, 0))
    in_specs = [pl.BlockSpec((tm, tk), lambda i, kk: (i, kk)),
                pl.BlockSpec((tk, D_MODEL), lambda i, kk: (kk, 0)), row_spec, gspec]
    if g2 is None:
        return pl.pallas_call(
            functools.partial(_proj_residual_nohn_kernel, nk),
            grid=(m // tm, nk), in_specs=in_specs, out_specs=row_spec,
            out_shape=jax.ShapeDtypeStruct((m, D_MODEL), F32),
            compiler_params=_params(("parallel", "arbitrary")), name=name,
        )(a, w, res, g)
    return pl.pallas_call(
        functools.partial(_proj_residual_kernel, nk),
        grid=(m // tm, nk), in_specs=in_specs + [gspec], out_specs=[row_spec, row_spec],
        out_shape=[jax.ShapeDtypeStruct((m, D_MODEL), F32), jax.ShapeDtypeStruct((m, D_MODEL), BF16)],
        compiler_params=_params(("parallel", "arbitrary")), name=name,
    )(a, w, res, g, g2)


def _proj_resident_kernel(a_ref, w_ref, res_ref, g_ref, g2_ref, h_ref, hn_ref):
    a = a_ref[...]
    for c in range(0, D_MODEL, PROJ_CHUNK):
        h_ref[:, c:c + PROJ_CHUNK] = jnp.dot(a, w_ref[:, c:c + PROJ_CHUNK], preferred_element_type=F32)
    _norm_residual(res_ref, g_ref, g2_ref, h_ref, hn_ref)


def _proj_resident(a, w, res, g, g2, tm, name):
    m, k = a.shape
    gspec = pl.BlockSpec((1, D_MODEL), lambda i: (0, 0))
    row_spec = pl.BlockSpec((tm, D_MODEL), lambda i: (i, 0))
    return pl.pallas_call(
        _proj_resident_kernel,
        grid=(m // tm,),
        in_specs=[pl.BlockSpec((tm, k), lambda i: (i, 0)),
                  pl.BlockSpec((k, D_MODEL), lambda i: (0, 0), pipeline_mode=pl.Buffered(1)),
                  row_spec, gspec, gspec],
        out_specs=[row_spec, row_spec],
        out_shape=[jax.ShapeDtypeStruct((m, D_MODEL), F32), jax.ShapeDtypeStruct((m, D_MODEL), BF16)],
        compiler_params=_params(("parallel",)), name=name,
    )(a, w, res, g, g2)


HALO = N_META


def _ffn_up_kernel(tm, tiles_per_seq, hn_ref, prev_ref, next_ref, hnm_ref, wa_ref, wg_ref, wc_ref, bc_ref,
                   f_ref, ext_ref):
    i = pl.program_id(0)

    @pl.when(pl.program_id(1) == 0)
    def _():
        first = i % tiles_per_seq == 0
        last = i % tiles_per_seq == tiles_per_seq - 1
        ext_ref[0:HALO, :] = jnp.where(first, hnm_ref[...], prev_ref[...])
        ext_ref[HALO:HALO + tm, :] = hn_ref[...]
        ext_ref[HALO + tm:, :] = jnp.where(last, jnp.zeros_like(next_ref), next_ref[...])

    wc = wc_ref[...]
    bc = bc_ref[...]
    for c in range(0, f_ref.shape[1], MXU_N):
        cols = slice(c, c + MXU_N)
        a = jnp.dot(ext_ref[...], wa_ref[:, cols], preferred_element_type=F32)
        gate = jnp.dot(ext_ref[HALO:HALO + tm, :], wg_ref[:, cols], preferred_element_type=F32)
        conv = (a[HALO - 1:HALO - 1 + tm] * wc[0:1, cols] + a[HALO:HALO + tm] * wc[1:2, cols]
                + a[HALO + 1:HALO + 1 + tm] * wc[2:3, cols] + bc[:, cols])
        f_ref[:, cols] = (jax.nn.gelu(conv) * gate).astype(f_ref.dtype)


def _ffn_up(hn, hn_meta, wa, wg, wc, bc, tm, tf, name):
    m = hn.shape[0]
    tiles_per_seq = SEQ // tm
    hb = tm // HALO
    last = m // HALO - 1
    return pl.pallas_call(
        functools.partial(_ffn_up_kernel, tm, tiles_per_seq),
        grid=(m // tm, D_FF_PAD // tf),
        in_specs=[
            pl.BlockSpec((tm, D_MODEL), lambda i, j: (i, 0)),
            pl.BlockSpec((HALO, D_MODEL), lambda i, j: (jnp.maximum(i * hb - 1, 0), 0)),
            pl.BlockSpec((HALO, D_MODEL), lambda i, j: (jnp.minimum((i + 1) * hb, last), 0)),
            pl.BlockSpec((N_META, D_MODEL), lambda i, j: (i // tiles_per_seq, 0)),
            pl.BlockSpec((D_MODEL, tf), lambda i, j: (0, j)),
            pl.BlockSpec((D_MODEL, tf), lambda i, j: (0, j)),
            pl.BlockSpec((3, tf), lambda i, j: (0, j)),
            pl.BlockSpec((1, tf), lambda i, j: (0, j)),
        ],
        out_specs=pl.BlockSpec((tm, tf), lambda i, j: (i, j)),
        out_shape=jax.ShapeDtypeStruct((m, D_FF_PAD), BF16),
        scratch_shapes=[pltpu.VMEM((tm + 2 * HALO, D_MODEL), BF16)],
        compiler_params=_params(("parallel", "arbitrary")),
        name=name,
    )(hn, hn, hn, hn_meta, wa, wg, wc, bc)


def _prep_weights(g_pre_mix, w_in, g_cq, w_uq, g_ckv, w_ukv, g_qn, g_kn, w_pa, w_pb, w_o,
                  g_post_mix, g_pre_ffn, w_up, w_conv, b_conv, w_down, g_post_ffn):
    perm_b = np.concatenate([np.arange(0, 32), np.arange(64, 96), np.arange(32, 64), np.arange(96, 128)])
    o = np.cumsum((0, MLA_Q_LORA, MLA_KV_LORA, MLA_ROPE, GQA_HEADS * HEAD_DIM, GQA_KV_HEADS * HEAD_DIM,
                   GQA_KV_HEADS * HEAD_DIM, D_MODEL, D_MODEL))
    w = w_in[0]
    cq, ckv, kr, gq, gk, gv, ga, gb = (w[:, o[i]:o[i + 1]] for i in range(8))
    gq = gq.reshape(D_MODEL, GQA_HEADS, HEAD_DIM)[:, :, perm_b].reshape(D_MODEL, -1)
    gk = gk.reshape(D_MODEL, GQA_KV_HEADS, HEAD_DIM)[:, :, perm_b].reshape(D_MODEL, -1)
    z32 = jnp.zeros((D_MODEL, 32), w.dtype)
    kr = jnp.concatenate([kr[:, :32], z32, kr[:, 32:], z32], axis=1)
    tail = jnp.zeros((D_MODEL, D_IN_PAD - OFF_KR - 128), w.dtype)
    w_in_p = jnp.concatenate([ga, gb, gq, cq, ckv, gk, gv, kr, tail], axis=1).astype(BF16)

    uq = w_uq[0].reshape(MLA_Q_LORA, MLA_HEADS, MLA_NOPE + MLA_ROPE)
    zq = jnp.zeros((MLA_Q_LORA, MLA_HEADS, 32), uq.dtype)
    uq = jnp.concatenate([uq[:, :, :MLA_NOPE], uq[:, :, MLA_NOPE:MLA_NOPE + 32], zq,
                          uq[:, :, MLA_NOPE + 32:], zq], axis=2).reshape(MLA_Q_LORA, -1).astype(BF16)
    ukv = w_ukv[0].reshape(MLA_KV_LORA, MLA_HEADS, MLA_NOPE + MLA_V)
    uk = ukv[:, :, :MLA_NOPE].reshape(MLA_KV_LORA, -1).astype(BF16)
    uv = ukv[:, :, MLA_NOPE:].reshape(MLA_KV_LORA, -1).astype(BF16)

    fpad = D_FF_PAD - D_FF
    up = w_up[0]
    return dict(
        g_pre_mix=g_pre_mix, w_in=w_in_p, g_cq=g_cq, w_uq=uq, g_ckv=g_ckv, w_uk=uk, w_uvt=uv.T,
        eye_kv=jnp.eye(GQA_KV_HEADS * HEAD_DIM, dtype=BF16),
        g_qn=g_qn[:, perm_b], g_kn=g_kn[:, perm_b],
        w_pa=w_pa[0].astype(BF16), w_pb=w_pb[0].astype(BF16), w_o=w_o[0].astype(BF16),
        g_post_mix=g_post_mix, g_pre_ffn=g_pre_ffn,
        w_up_a=jnp.pad(up[:, :D_FF], ((0, 0), (0, fpad))).astype(BF16),
        w_up_g=jnp.pad(up[:, D_FF:], ((0, 0), (0, fpad))).astype(BF16),
        w_conv=jnp.pad(w_conv[0], ((0, 0), (0, fpad))), b_conv=jnp.pad(b_conv, ((0, 0), (0, fpad))),
        w_down=jnp.pad(w_down[0], ((0, fpad), (0, 0))).astype(BF16), g_post_ffn=g_post_ffn,
    )


def _rope_tables(pos, dim):
    inv = ROPE_THETA ** (-jnp.arange(0, dim, 2, dtype=F32) / dim)
    ang = pos.astype(F32)[:, None] * inv[None, :]
    return jnp.cos(ang), jnp.sin(ang)


def _tables(pos, row, col):
    c1, s1 = _rope_tables(pos, MLA_ROPE)
    z = jnp.zeros_like(c1)
    cos_a = jnp.concatenate([c1, z, c1, z], axis=1)
    sin_a = jnp.concatenate([-s1, z, s1, z], axis=1)
    cr, sr = _rope_tables(row, HEAD_DIM // 2)
    cc, sc = _rope_tables(col, HEAD_DIM // 2)
    cos_b = jnp.concatenate([cr, cc, cr, cc], axis=1)
    sin_b = jnp.concatenate([-sr, -sc, sr, sc], axis=1)
    return cos_a, sin_a, cos_b, sin_b


def kernel(x_prompt, x_sample, meta_tokens, g_pre_mix, w_in, g_cq, w_uq, g_ckv, w_ukv, g_qn, g_kn,
           w_pa, w_pb, w_o, g_post_mix, g_pre_ffn, w_up, w_conv, b_conv, w_down, g_post_ffn):
    w = _prep_weights(g_pre_mix, w_in, g_cq, w_uq, g_ckv, w_ukv, g_qn, g_kn, w_pa, w_pb, w_o,
                      g_post_mix, g_pre_ffn, w_up, w_conv, b_conv, w_down, g_post_ffn)
    t = jnp.arange(SEQ, dtype=jnp.int32)
    tabs_real = _tables(t + N_META, t // GRID_W, t % GRID_W)
    zero = jnp.zeros((N_META,), jnp.int32)
    tabs_meta = _tables(jnp.arange(N_META), zero, zero)
    return (_trunk(x_prompt, meta_tokens, w, tabs_real, tabs_meta, "p"),
            _trunk(x_sample, meta_tokens, w, tabs_real, tabs_meta, "s"))


def _trunk(x, meta_tokens, w, tabs_real, tabs_meta, tag):
    batch = x.shape[0]
    xr = x.reshape(batch * SEQ, D_MODEL)
    xm = jnp.broadcast_to(meta_tokens[None], (batch, N_META, D_MODEL)).reshape(batch * N_META, D_MODEL)
    mrows = batch * N_META
    tabs_meta = tuple(jnp.tile(a, (batch, 1)) for a in tabs_meta)

    z = _matmul(_norm_rows(xr, w["g_pre_mix"], 256, "norm_" + tag), w["w_in"], 1024, 1024, BF16,
                "in_proj_" + tag)
    zm = _matmul(_norm_rows(xm, w["g_pre_mix"], mrows, "norm_meta_" + tag), w["w_in"], mrows, 1024, BF16,
                 "in_proj_meta_" + tag)
    qa, ka, vat, qb, kb, vbt = _heads(z, w, tabs_real, 256, SEQ // 256, "heads_" + tag)
    qam, kam, vatm, qbm, kbm, vbtm = _heads(zm, w, tabs_meta, mrows, 1, "heads_meta_" + tag)

    def per_batch(t):
        return t.reshape(t.shape[0], batch, N_META).transpose(1, 0, 2)

    def token_major(t):
        return t.transpose(0, 2, 1).reshape(batch * N_META, t.shape[1])

    vatm, vbtm = per_batch(vatm), per_batch(vbtm)

    cfg_a = dict(batch=batch, heads=MLA_HEADS, group=1, dk=MLA_QK)
    cfg_b = dict(batch=batch, heads=GQA_HEADS, group=GQA_GROUP, dk=HEAD_DIM)
    oa = _attention(qa, ka, vat, kam, vatm, lq=SEQ, tq=1024, tk=2048, name="attn_a_" + tag, **cfg_a)
    ob = _attention(qb, kb, vbt, kbm, vbtm, lq=SEQ, tq=1024, tk=2048, name="attn_b_" + tag, **cfg_b)
    oam = token_major(_attention(qam, ka, vat, kam, vatm, lq=N_META, tq=N_META, tk=1024,
                                 name="attn_a_meta_" + tag, **cfg_a))
    obm = token_major(_attention(qbm, kb, vbt, kbm, vbtm, lq=N_META, tq=N_META, tk=1024,
                                 name="attn_b_meta_" + tag, **cfg_b))

    mg = _merge(oa, ob, z, w["w_pa"], w["w_pb"], 1024, 1024, "merge_" + tag)
    mgm = _merge(oam, obm, zm, w["w_pa"], w["w_pb"], mrows, 1024, "merge_meta_" + tag)
    h1, hn = _proj_resident(mg, w["w_o"], xr, w["g_post_mix"], w["g_pre_ffn"], 256, "out_proj_" + tag)
    _, hnm = _proj_residual(mgm, w["w_o"], xm, w["g_post_mix"], w["g_pre_ffn"], mrows, 512,
                            "out_proj_meta_" + tag)

    f = _ffn_up(hn, hnm, w["w_up_a"], w["w_up_g"], w["w_conv"], w["b_conv"], 1024, 512, "ffn_up_" + tag)
    y = _proj_residual(f, w["w_down"], h1, w["g_post_ffn"], None, 512, 1024, "ffn_down_" + tag)
    return y.reshape(batch, SEQ, D_MODEL)
```

```python
import functools
import math

import jax
import jax.numpy as jnp
import numpy as np
from jax import lax
from jax.experimental import pallas as pl
from jax.experimental.pallas import tpu as pltpu

F32 = jnp.float32
BF16 = jnp.bfloat16

D_MODEL = 4096
SEQ = 4096
N_META = 16
GRID_W = 64
HEAD_DIM = 128
MLA_HEADS = 16
MLA_Q_LORA = 1024
MLA_KV_LORA = 512
MLA_NOPE = 128
MLA_ROPE = 64
MLA_V = 128
MLA_QK = 256
GQA_HEADS = 16
GQA_KV_HEADS = 4
GQA_GROUP = GQA_HEADS // GQA_KV_HEADS
D_FF = 11008
D_FF_PAD = 11264
ROPE_THETA = 10000.0
EPS = 1e-6
LOG2E = math.log2(math.e)

OFF_GA = 0
OFF_GB = 4096
OFF_GQ = 8192
OFF_CQ = 10240
OFF_CKV = 11264
OFF_GK = 11776
OFF_GV = 12288
OFF_KR = 12800
D_IN_PAD = 13312

LANES = 128
MXU_N = 256
VMEM_LIMIT = 60 * 1024 * 1024


def _params(sem):
    return pltpu.CompilerParams(dimension_semantics=sem, vmem_limit_bytes=VMEM_LIMIT)


def _rms(x, g):
    return x * lax.rsqrt(jnp.mean(x * x, axis=-1, keepdims=True) + EPS) * g


def _norm_kernel(x_ref, g_ref, o_ref):
    o_ref[...] = _rms(x_ref[...], g_ref[...]).astype(o_ref.dtype)


def _norm_rows(x, g, tm, name):
    m = x.shape[0]
    return pl.pallas_call(
        _norm_kernel,
        grid=(m // tm,),
        in_specs=[pl.BlockSpec((tm, D_MODEL), lambda i: (i, 0)), pl.BlockSpec((1, D_MODEL), lambda i: (0, 0))],
        out_specs=pl.BlockSpec((tm, D_MODEL), lambda i: (i, 0)),
        out_shape=jax.ShapeDtypeStruct((m, D_MODEL), BF16),
        compiler_params=_params(("parallel",)),
        name=name,
    )(x, g)


def _mm_kernel(a_ref, b_ref, o_ref):
    o_ref[...] = jnp.dot(a_ref[...], b_ref[...], preferred_element_type=F32).astype(o_ref.dtype)


def _matmul(a, b, tm, tn, out_dtype, name):
    m, k = a.shape
    n = b.shape[1]
    return pl.pallas_call(
        _mm_kernel,
        grid=(m // tm, n // tn),
        in_specs=[pl.BlockSpec((tm, k), lambda i, j: (i, 0)), pl.BlockSpec((k, tn), lambda i, j: (0, j))],
        out_specs=pl.BlockSpec((tm, tn), lambda i, j: (i, j)),
        out_shape=jax.ShapeDtypeStruct((m, n), out_dtype),
        compiler_params=_params(("parallel", "parallel")),
        name=name,
    )(a, b)


NT_DIMS = (((1,), (1,)), ((), ()))


def _rope(x, cos, sin_signed):
    return x * cos + pltpu.roll(x, 64, 1) * sin_signed


def _heads_kernel(cq_ref, ckv_ref, kr_ref, gq_ref, gk_ref, gv_ref, wuq_ref, wuk_ref, wuvt_ref, eye_ref,
                  gcq_ref, gckv_ref, gqn_ref, gkn_ref, cosa_ref, sina_ref, cosb_ref, sinb_ref,
                  qa_ref, ka_ref, vat_ref, qb_ref, kb_ref, vbt_ref):
    cosa, sina = cosa_ref[...], sina_ref[...]
    cosb, sinb = cosb_ref[...], sinb_ref[...]
    scale_a = LOG2E / math.sqrt(MLA_NOPE + MLA_ROPE)
    scale_b = LOG2E / math.sqrt(HEAD_DIM)

    cq = _rms(cq_ref[...].astype(F32), gcq_ref[...]).astype(BF16)
    for h in range(MLA_HEADS):
        q = jnp.dot(cq, wuq_ref[:, h * MLA_QK:(h + 1) * MLA_QK], preferred_element_type=F32)
        qa_ref[:, h * MLA_QK:h * MLA_QK + MLA_NOPE] = (q[:, :MLA_NOPE] * scale_a).astype(BF16)
        qa_ref[:, h * MLA_QK + MLA_NOPE:(h + 1) * MLA_QK] = (
            _rope(q[:, MLA_NOPE:], cosa, sina) * scale_a).astype(BF16)

    ckv = _rms(ckv_ref[...].astype(F32), gckv_ref[...]).astype(BF16)
    k_rope = _rope(kr_ref[...].astype(F32), cosa, sina).astype(BF16)
    k_nope = jnp.dot(ckv, wuk_ref[...], preferred_element_type=F32).astype(BF16)
    for h in range(MLA_HEADS):
        ka_ref[:, h * MLA_QK:h * MLA_QK + MLA_NOPE] = k_nope[:, h * MLA_NOPE:(h + 1) * MLA_NOPE]
        ka_ref[:, h * MLA_QK + MLA_NOPE:(h + 1) * MLA_QK] = k_rope
    vat_ref[...] = lax.dot_general(wuvt_ref[...], ckv, NT_DIMS, preferred_element_type=F32).astype(BF16)
    vbt_ref[...] = lax.dot_general(eye_ref[...], gv_ref[...], NT_DIMS, preferred_element_type=F32).astype(BF16)

    gqn, gkn = gqn_ref[...], gkn_ref[...]
    for h in range(GQA_HEADS):
        x = gq_ref[:, h * HEAD_DIM:(h + 1) * HEAD_DIM].astype(F32)
        qb_ref[:, h * HEAD_DIM:(h + 1) * HEAD_DIM] = (_rope(_rms(x, gqn), cosb, sinb) * scale_b).astype(BF16)
    for h in range(GQA_KV_HEADS):
        x = gk_ref[:, h * HEAD_DIM:(h + 1) * HEAD_DIM].astype(F32)
        kb_ref[:, h * HEAD_DIM:(h + 1) * HEAD_DIM] = _rope(_rms(x, gkn), cosb, sinb).astype(BF16)


def _heads(z, w, tabs, tm, tab_blocks, name):
    m = z.shape[0]

    def zspec(width, off):
        return pl.BlockSpec((tm, width), lambda i: (i, off // width))

    def full(a):
        return pl.BlockSpec(a.shape, lambda i: (0,) * a.ndim)

    tab_spec = pl.BlockSpec((tm, 128), lambda i: (i % tab_blocks, 0))
    outs = ((MLA_HEADS * MLA_QK, True), (MLA_HEADS * MLA_QK, True), (MLA_HEADS * MLA_V, False),
            (GQA_HEADS * HEAD_DIM, True), (GQA_KV_HEADS * HEAD_DIM, True), (GQA_KV_HEADS * HEAD_DIM, False))
    consts = (w["w_uq"], w["w_uk"], w["w_uvt"], w["eye_kv"], w["g_cq"], w["g_ckv"], w["g_qn"], w["g_kn"])
    return pl.pallas_call(
        _heads_kernel,
        grid=(m // tm,),
        in_specs=[zspec(MLA_Q_LORA, OFF_CQ), zspec(MLA_KV_LORA, OFF_CKV), zspec(128, OFF_KR),
                  zspec(GQA_HEADS * HEAD_DIM, OFF_GQ), zspec(GQA_KV_HEADS * HEAD_DIM, OFF_GK),
                  zspec(GQA_KV_HEADS * HEAD_DIM, OFF_GV)]
        + [full(a) for a in consts] + [tab_spec] * 4,
        out_specs=[pl.BlockSpec((tm, n), lambda i: (i, 0)) if rows else pl.BlockSpec((n, tm), lambda i: (0, i))
                   for n, rows in outs],
        out_shape=[jax.ShapeDtypeStruct((m, n) if rows else (n, m), BF16) for n, rows in outs],
        compiler_params=_params(("parallel",)),
        name=name,
    )(z, z, z, z, z, z, *consts, *tabs)


def _attn_kernel(tk, transpose_out, q_ref, k_ref, vt_ref, km_ref, vmt_ref, o_ref):
    q = q_ref[...]
    s = lax.dot_general(km_ref[...], q, NT_DIMS, preferred_element_type=F32)
    m = jnp.max(s, axis=0, keepdims=True)
    p = jnp.exp2(s - m)
    l = jnp.sum(p, axis=0, keepdims=True)
    acc = jnp.dot(vmt_ref[...], p.astype(BF16), preferred_element_type=F32)

    def scores(c):
        return lax.dot_general(k_ref[c * tk:(c + 1) * tk, :], q, NT_DIMS, preferred_element_type=F32)

    n_chunks = k_ref.shape[0] // tk
    s_next = scores(0)
    for c in range(n_chunks):
        s = s_next
        if c + 1 < n_chunks:
            s_next = scores(c + 1)
        m_new = jnp.maximum(m, jnp.max(s, axis=0, keepdims=True))
        alpha = jnp.exp2(m - m_new)
        p = jnp.exp2(s - m_new)
        l = alpha * l + jnp.sum(p, axis=0, keepdims=True)
        acc = alpha * acc + jnp.dot(vt_ref[:, c * tk:(c + 1) * tk], p.astype(BF16), preferred_element_type=F32)
        m = m_new
    o = acc / l
    o_ref[...] = (o.T if transpose_out else o).astype(o_ref.dtype)


def _attention(q, k, vt, km, vmt, *, batch, heads, group, dk, lq, tq, tk, name):
    nq = lq // tq
    dv = HEAD_DIM
    transpose_out = tq % LANES == 0
    if transpose_out:
        out_spec = pl.BlockSpec((tq, dv), lambda b, h, i: (b * nq + i, h))
        out_shape = jax.ShapeDtypeStruct((batch * lq, heads * dv), BF16)
    else:
        out_spec = pl.BlockSpec((None, dv, tq), lambda b, h, i: (b, h, i))
        out_shape = jax.ShapeDtypeStruct((batch, heads * dv, lq), BF16)
    return pl.pallas_call(
        functools.partial(_attn_kernel, tk, transpose_out),
        grid=(batch, heads, nq),
        in_specs=[
            pl.BlockSpec((tq, dk), lambda b, h, i: (b * nq + i, h)),
            pl.BlockSpec((SEQ, dk), lambda b, h, i: (b, h // group)),
            pl.BlockSpec((dv, SEQ), lambda b, h, i: (h // group, b)),
            pl.BlockSpec((N_META, dk), lambda b, h, i: (b, h // group)),
            pl.BlockSpec((None, dv, N_META), lambda b, h, i: (b, h // group, 0)),
        ],
        out_specs=out_spec,
        out_shape=out_shape,
        compiler_params=_params(("parallel", "parallel", "parallel")),
        name=name,
    )(q, k, vt, km, vmt)


def _merge_kernel(oa_ref, ob_ref, ga_ref, gb_ref, wpa_ref, wpb_ref, o_ref):
    a = jnp.dot(oa_ref[...], wpa_ref[...], preferred_element_type=F32)
    b = jnp.dot(ob_ref[...], wpb_ref[...], preferred_element_type=F32)
    ga = jax.nn.sigmoid(ga_ref[...].astype(F32))
    gb = jax.nn.sigmoid(gb_ref[...].astype(F32))
    o_ref[...] = (ga * a + gb * b).astype(o_ref.dtype)


def _merge(oa, ob, z, wpa, wpb, tm, tn, name):
    m, k = oa.shape
    return pl.pallas_call(
        _merge_kernel,
        grid=(m // tm, D_MODEL // tn),
        in_specs=[
            pl.BlockSpec((tm, k), lambda i, j: (i, 0)),
            pl.BlockSpec((tm, k), lambda i, j: (i, 0)),
            pl.BlockSpec((tm, tn), lambda i, j: (i, OFF_GA // tn + j)),
            pl.BlockSpec((tm, tn), lambda i, j: (i, OFF_GB // tn + j)),
            pl.BlockSpec((k, tn), lambda i, j: (0, j)),
            pl.BlockSpec((k, tn), lambda i, j: (0, j)),
        ],
        out_specs=pl.BlockSpec((tm, tn), lambda i, j: (i, j)),
        out_shape=jax.ShapeDtypeStruct((m, D_MODEL), BF16),
        compiler_params=_params(("parallel", "parallel")),
        name=name,
    )(oa, ob, z, z, wpa, wpb)


PROJ_CHUNK = 512
NORM_ROWS = 128


def _norm_residual(res_ref, g_ref, g2_ref, h_ref, hn_ref):
    for r in range(0, h_ref.shape[0], NORM_ROWS):
        rows = slice(r, min(r + NORM_ROWS, h_ref.shape[0]))
        h = res_ref[rows, :] + _rms(h_ref[rows, :], g_ref[...])
        h_ref[rows, :] = h
        if hn_ref is not None:
            hn_ref[rows, :] = _rms(h, g2_ref[...]).astype(hn_ref.dtype)


def _proj_residual_kernel(nk, a_ref, w_ref, res_ref, g_ref, g2_ref, h_ref, hn_ref):
    kk = pl.program_id(1)

    def accumulate(first):
        a = a_ref[...]
        for c in range(0, D_MODEL, PROJ_CHUNK):
            part = jnp.dot(a, w_ref[:, c:c + PROJ_CHUNK], preferred_element_type=F32)
            if first:
                h_ref[:, c:c + PROJ_CHUNK] = part
            else:
                h_ref[:, c:c + PROJ_CHUNK] += part

    pl.when(kk == 0)(functools.partial(accumulate, True))
    pl.when(kk > 0)(functools.partial(accumulate, False))
    pl.when(kk == nk - 1)(functools.partial(_norm_residual, res_ref, g_ref, g2_ref, h_ref, hn_ref))


def _proj_residual_nohn_kernel(nk, a_ref, w_ref, res_ref, g_ref, h_ref):
    _proj_residual_kernel(nk, a_ref, w_ref, res_ref, g_ref, None, h_ref, None)


def _proj_residual(a, w, res, g, g2, tm, tk, name):
    m, k = a.shape
    nk = k // tk
    gspec = pl.BlockSpec((1, D_MODEL), lambda i, kk: (0, 0))
    row_spec = pl.BlockSpec((tm, D_MODEL), lambda i, kk: (i, 0))
    in_specs = [pl.BlockSpec((tm, tk), lambda i, kk: (i, kk)),
                pl.BlockSpec((tk, D_MODEL), lambda i, kk: (kk, 0)), row_spec, gspec]
    if g2 is None:
        return pl.pallas_call(
            functools.partial(_proj_residual_nohn_kernel, nk),
            grid=(m // tm, nk), in_specs=in_specs, out_specs=row_spec,
            out_shape=jax.ShapeDtypeStruct((m, D_MODEL), F32),
            compiler_params=_params(("parallel", "arbitrary")), name=name,
        )(a, w, res, g)
    return pl.pallas_call(
        functools.partial(_proj_residual_kernel, nk),
        grid=(m // tm, nk), in_specs=in_specs + [gspec], out_specs=[row_spec, row_spec],
        out_shape=[jax.ShapeDtypeStruct((m, D_MODEL), F32), jax.ShapeDtypeStruct((m, D_MODEL), BF16)],
        compiler_params=_params(("parallel", "arbitrary")), name=name,
    )(a, w, res, g, g2)


def _proj_resident_kernel(a_ref, w_ref, res_ref, g_ref, g2_ref, h_ref, hn_ref):
    a = a_ref[...]
    for c in range(0, D_MODEL, PROJ_CHUNK):
        h_ref[:, c:c + PROJ_CHUNK] = jnp.dot(a, w_ref[:, c:c + PROJ_CHUNK], preferred_element_type=F32)
    _norm_residual(res_ref, g_ref, g2_ref, h_ref, hn_ref)


def _proj_resident(a, w, res, g, g2, tm, name):
    m, k = a.shape
    gspec = pl.BlockSpec((1, D_MODEL), lambda i: (0, 0))
    row_spec = pl.BlockSpec((tm, D_MODEL), lambda i: (i, 0))
    return pl.pallas_call(
        _proj_resident_kernel,
        grid=(m // tm,),
        in_specs=[pl.BlockSpec((tm, k), lambda i: (i, 0)),
                  pl.BlockSpec((k, D_MODEL), lambda i: (0, 0), pipeline_mode=pl.Buffered(1)),
                  row_spec, gspec, gspec],
        out_specs=[row_spec, row_spec],
        out_shape=[jax.ShapeDtypeStruct((m, D_MODEL), F32), jax.ShapeDtypeStruct((m, D_MODEL), BF16)],
        compiler_params=_params(("parallel",)), name=name,
    )(a, w, res, g, g2)


HALO = N_META


def _ffn_up_kernel(tm, tiles_per_seq, hn_ref, prev_ref, next_ref, hnm_ref, wa_ref, wg_ref, wc_ref, bc_ref,
                   f_ref, ext_ref):
    i = pl.program_id(0)

    @pl.when(pl.program_id(1) == 0)
    def _():
        first = i % tiles_per_seq == 0
        last = i % tiles_per_seq == tiles_per_seq - 1
        ext_ref[0:HALO, :] = jnp.where(first, hnm_ref[...], prev_ref[...])
        ext_ref[HALO:HALO + tm, :] = hn_ref[...]
        ext_ref[HALO + tm:, :] = jnp.where(last, jnp.zeros_like(next_ref), next_ref[...])

    wc = wc_ref[...]
    bc = bc_ref[...]
    for c in range(0, f_ref.shape[1], MXU_N):
        cols = slice(c, c + MXU_N)
        a = jnp.dot(ext_ref[...], wa_ref[:, cols], preferred_element_type=F32)
        gate = jnp.dot(ext_ref[HALO:HALO + tm, :], wg_ref[:, cols], preferred_element_type=F32)
        conv = (a[HALO - 1:HALO - 1 + tm] * wc[0:1, cols] + a[HALO:HALO + tm] * wc[1:2, cols]
                + a[HALO + 1:HALO + 1 + tm] * wc[2:3, cols] + bc[:, cols])
        f_ref[:, cols] = (jax.nn.gelu(conv) * gate).astype(f_ref.dtype)


def _ffn_up(hn, hn_meta, wa, wg, wc, bc, tm, tf, name):
    m = hn.shape[0]
    tiles_per_seq = SEQ // tm
    hb = tm // HALO
    last = m // HALO - 1
    return pl.pallas_call(
        functools.partial(_ffn_up_kernel, tm, tiles_per_seq),
        grid=(m // tm, D_FF_PAD // tf),
        in_specs=[
            pl.BlockSpec((tm, D_MODEL), lambda i, j: (i, 0), pipeline_mode=pl.Buffered(1)),
            pl.BlockSpec((HALO, D_MODEL), lambda i, j: (jnp.maximum(i * hb - 1, 0), 0)),
            pl.BlockSpec((HALO, D_MODEL), lambda i, j: (jnp.minimum((i + 1) * hb, last), 0)),
            pl.BlockSpec((N_META, D_MODEL), lambda i, j: (i // tiles_per_seq, 0)),
            pl.BlockSpec((D_MODEL, tf), lambda i, j: (0, j)),
            pl.BlockSpec((D_MODEL, tf), lambda i, j: (0, j)),
            pl.BlockSpec((3, tf), lambda i, j: (0, j)),
            pl.BlockSpec((1, tf), lambda i, j: (0, j)),
        ],
        out_specs=pl.BlockSpec((tm, tf), lambda i, j: (i, j)),
        out_shape=jax.ShapeDtypeStruct((m, D_FF_PAD), BF16),
        scratch_shapes=[pltpu.VMEM((tm + 2 * HALO, D_MODEL), BF16)],
        compiler_params=_params(("parallel", "arbitrary")),
        name=name,
    )(hn, hn, hn, hn_meta, wa, wg, wc, bc)


def _prep_weights(g_pre_mix, w_in, g_cq, w_uq, g_ckv, w_ukv, g_qn, g_kn, w_pa, w_pb, w_o,
                  g_post_mix, g_pre_ffn, w_up, w_conv, b_conv, w_down, g_post_ffn):
    perm_b = np.concatenate([np.arange(0, 32), np.arange(64, 96), np.arange(32, 64), np.arange(96, 128)])
    o = np.cumsum((0, MLA_Q_LORA, MLA_KV_LORA, MLA_ROPE, GQA_HEADS * HEAD_DIM, GQA_KV_HEADS * HEAD_DIM,
                   GQA_KV_HEADS * HEAD_DIM, D_MODEL, D_MODEL))
    w = w_in[0]
    cq, ckv, kr, gq, gk, gv, ga, gb = (w[:, o[i]:o[i + 1]] for i in range(8))
    gq = gq.reshape(D_MODEL, GQA_HEADS, HEAD_DIM)[:, :, perm_b].reshape(D_MODEL, -1)
    gk = gk.reshape(D_MODEL, GQA_KV_HEADS, HEAD_DIM)[:, :, perm_b].reshape(D_MODEL, -1)
    z32 = jnp.zeros((D_MODEL, 32), w.dtype)
    kr = jnp.concatenate([kr[:, :32], z32, kr[:, 32:], z32], axis=1)
    tail = jnp.zeros((D_MODEL, D_IN_PAD - OFF_KR - 128), w.dtype)
    w_in_p = jnp.concatenate([ga, gb, gq, cq, ckv, gk, gv, kr, tail], axis=1).astype(BF16)

    uq = w_uq[0].reshape(MLA_Q_LORA, MLA_HEADS, MLA_NOPE + MLA_ROPE)
    zq = jnp.zeros((MLA_Q_LORA, MLA_HEADS, 32), uq.dtype)
    uq = jnp.concatenate([uq[:, :, :MLA_NOPE], uq[:, :, MLA_NOPE:MLA_NOPE + 32], zq,
                          uq[:, :, MLA_NOPE + 32:], zq], axis=2).reshape(MLA_Q_LORA, -1).astype(BF16)
    ukv = w_ukv[0].reshape(MLA_KV_LORA, MLA_HEADS, MLA_NOPE + MLA_V)
    uk = ukv[:, :, :MLA_NOPE].reshape(MLA_KV_LORA, -1).astype(BF16)
    uv = ukv[:, :, MLA_NOPE:].reshape(MLA_KV_LORA, -1).astype(BF16)

    fpad = D_FF_PAD - D_FF
    up = w_up[0]
    return dict(
        g_pre_mix=g_pre_mix, w_in=w_in_p, g_cq=g_cq, w_uq=uq, g_ckv=g_ckv, w_uk=uk, w_uvt=uv.T,
        eye_kv=jnp.eye(GQA_KV_HEADS * HEAD_DIM, dtype=BF16),
        g_qn=g_qn[:, perm_b], g_kn=g_kn[:, perm_b],
        w_pa=w_pa[0].astype(BF16), w_pb=w_pb[0].astype(BF16), w_o=w_o[0].astype(BF16),
        g_post_mix=g_post_mix, g_pre_ffn=g_pre_ffn,
        w_up_a=jnp.pad(up[:, :D_FF], ((0, 0), (0, fpad))).astype(BF16),
        w_up_g=jnp.pad(up[:, D_FF:], ((0, 0), (0, fpad))).astype(BF16),
        w_conv=jnp.pad(w_conv[0], ((0, 0), (0, fpad))), b_conv=jnp.pad(b_conv, ((0, 0), (0, fpad))),
        w_down=jnp.pad(w_down[0], ((0, fpad), (0, 0))).astype(BF16), g_post_ffn=g_post_ffn,
    )


def _rope_tables(pos, dim):
    inv = ROPE_THETA ** (-jnp.arange(0, dim, 2, dtype=F32) / dim)
    ang = pos.astype(F32)[:, None] * inv[None, :]
    return jnp.cos(ang), jnp.sin(ang)


def _tables(pos, row, col):
    c1, s1 = _rope_tables(pos, MLA_ROPE)
    z = jnp.zeros_like(c1)
    cos_a = jnp.concatenate([c1, z, c1, z], axis=1)
    sin_a = jnp.concatenate([-s1, z, s1, z], axis=1)
    cr, sr = _rope_tables(row, HEAD_DIM // 2)
    cc, sc = _rope_tables(col, HEAD_DIM // 2)
    cos_b = jnp.concatenate([cr, cc, cr, cc], axis=1)
    sin_b = jnp.concatenate([-sr, -sc, sr, sc], axis=1)
    return cos_a, sin_a, cos_b, sin_b


def kernel(x_prompt, x_sample, meta_tokens, g_pre_mix, w_in, g_cq, w_uq, g_ckv, w_ukv, g_qn, g_kn,
           w_pa, w_pb, w_o, g_post_mix, g_pre_ffn, w_up, w_conv, b_conv, w_down, g_post_ffn):
    w = _prep_weights(g_pre_mix, w_in, g_cq, w_uq, g_ckv, w_ukv, g_qn, g_kn, w_pa, w_pb, w_o,
                      g_post_mix, g_pre_ffn, w_up, w_conv, b_conv, w_down, g_post_ffn)
    t = jnp.arange(SEQ, dtype=jnp.int32)
    tabs_real = _tables(t + N_META, t // GRID_W, t % GRID_W)
    zero = jnp.zeros((N_META,), jnp.int32)
    tabs_meta = _tables(jnp.arange(N_META), zero, zero)
    return (_trunk(x_prompt, meta_tokens, w, tabs_real, tabs_meta, "p"),
            _trunk(x_sample, meta_tokens, w, tabs_real, tabs_meta, "s"))


def _trunk(x, meta_tokens, w, tabs_real, tabs_meta, tag):
    batch = x.shape[0]
    xr = x.reshape(batch * SEQ, D_MODEL)
    xm = jnp.broadcast_to(meta_tokens[None], (batch, N_META, D_MODEL)).reshape(batch * N_META, D_MODEL)
    mrows = batch * N_META
    tabs_meta = tuple(jnp.tile(a, (batch, 1)) for a in tabs_meta)

    z = _matmul(_norm_rows(xr, w["g_pre_mix"], 256, "norm_" + tag), w["w_in"], 1024, 1024, BF16,
                "in_proj_" + tag)
    zm = _matmul(_norm_rows(xm, w["g_pre_mix"], mrows, "norm_meta_" + tag), w["w_in"], mrows, 1024, BF16,
                 "in_proj_meta_" + tag)
    qa, ka, vat, qb, kb, vbt = _heads(z, w, tabs_real, 256, SEQ // 256, "heads_" + tag)
    qam, kam, vatm, qbm, kbm, vbtm = _heads(zm, w, tabs_meta, mrows, 1, "heads_meta_" + tag)

    def per_batch(t):
        return t.reshape(t.shape[0], batch, N_META).transpose(1, 0, 2)

    def token_major(t):
        return t.transpose(0, 2, 1).reshape(batch * N_META, t.shape[1])

    vatm, vbtm = per_batch(vatm), per_batch(vbtm)

    cfg_a = dict(batch=batch, heads=MLA_HEADS, group=1, dk=MLA_QK)
    cfg_b = dict(batch=batch, heads=GQA_HEADS, group=GQA_GROUP, dk=HEAD_DIM)
    oa = _attention(qa, ka, vat, kam, vatm, lq=SEQ, tq=1024, tk=2048, name="attn_a_" + tag, **cfg_a)
    ob = _attention(qb, kb, vbt, kbm, vbtm, lq=SEQ, tq=1024, tk=2048, name="attn_b_" + tag, **cfg_b)
    oam = token_major(_attention(qam, ka, vat, kam, vatm, lq=N_META, tq=N_META, tk=1024,
                                 name="attn_a_meta_" + tag, **cfg_a))
    obm = token_major(_attention(qbm, kb, vbt, kbm, vbtm, lq=N_META, tq=N_META, tk=1024,
                                 name="attn_b_meta_" + tag, **cfg_b))

    mg = _merge(oa, ob, z, w["w_pa"], w["w_pb"], 1024, 1024, "merge_" + tag)
    mgm = _merge(oam, obm, zm, w["w_pa"], w["w_pb"], mrows, 1024, "merge_meta_" + tag)
    h1, hn = _proj_resident(mg, w["w_o"], xr, w["g_post_mix"], w["g_pre_ffn"], 256, "out_proj_" + tag)
    _, hnm = _proj_residual(mgm, w["w_o"], xm, w["g_post_mix"], w["g_pre_ffn"], mrows, 512,
                            "out_proj_meta_" + tag)

    f = _ffn_up(hn, hnm, w["w_up_a"], w["w_up_g"], w["w_conv"], w["b_conv"], 1024, 1024, "ffn_up_" + tag)
    y = _proj_residual(f, w["w_down"], h1, w["g_post_ffn"], None, 512, 1024, "ffn_down_" + tag)
    return y.reshape(batch, SEQ, D_MODEL)
```

```python
import functools
import math

import jax
import jax.numpy as jnp
import numpy as np
from jax import lax
from jax.experimental import pallas as pl
from jax.experimental.pallas import tpu as pltpu

F32 = jnp.float32
BF16 = jnp.bfloat16

D_MODEL = 4096
SEQ = 4096
N_META = 16
GRID_W = 64
HEAD_DIM = 128
MLA_HEADS = 16
MLA_Q_LORA = 1024
MLA_KV_LORA = 512
MLA_NOPE = 128
MLA_ROPE = 64
MLA_V = 128
MLA_QK = 256
GQA_HEADS = 16
GQA_KV_HEADS = 4
GQA_GROUP = GQA_HEADS // GQA_KV_HEADS
D_FF = 11008
D_FF_PAD = 11264
ROPE_THETA = 10000.0
EPS = 1e-6
LOG2E = math.log2(math.e)

OFF_GA = 0
OFF_GB = 4096
OFF_GQ = 8192
OFF_CQ = 10240
OFF_CKV = 11264
OFF_GK = 11776
OFF_GV = 12288
OFF_KR = 12800
D_IN_PAD = 13312

LANES = 128
HALF_LANES = LANES // 2
MXU_N = 256
VMEM_LIMIT = 60 * 1024 * 1024

TM_NORM = 256
TM_MM, TN_MM = 1024, 1024
TM_HEADS = 256
TQ_ATTN, TK_ATTN = 1024, 2048
TM_OUT = 256
TM_UP, TF_UP = 1024, 512
TM_DOWN, TK_DOWN = 512, 1024
TK_META = 512


def _params(sem):
    return pltpu.CompilerParams(dimension_semantics=sem, vmem_limit_bytes=VMEM_LIMIT)


def _rms(x, g):
    return x * lax.rsqrt(jnp.mean(x * x, axis=-1, keepdims=True) + EPS) * g


def _norm_kernel(x_ref, g_ref, o_ref):
    o_ref[...] = _rms(x_ref[...], g_ref[...]).astype(o_ref.dtype)


def _norm_rows(x, g, tm, name):
    m = x.shape[0]
    return pl.pallas_call(
        _norm_kernel,
        grid=(m // tm,),
        in_specs=[pl.BlockSpec((tm, D_MODEL), lambda i: (i, 0)), pl.BlockSpec((1, D_MODEL), lambda i: (0, 0))],
        out_specs=pl.BlockSpec((tm, D_MODEL), lambda i: (i, 0)),
        out_shape=jax.ShapeDtypeStruct((m, D_MODEL), BF16),
        compiler_params=_params(("parallel",)),
        name=name,
    )(x, g)


def _mm_kernel(a_ref, b_ref, o_ref):
    o_ref[...] = jnp.dot(a_ref[...], b_ref[...], preferred_element_type=F32).astype(o_ref.dtype)


def _matmul(a, b, tm, tn, out_dtype, name):
    m, k = a.shape
    n = b.shape[1]
    return pl.pallas_call(
        _mm_kernel,
        grid=(m // tm, n // tn),
        in_specs=[pl.BlockSpec((tm, k), lambda i, j: (i, 0)), pl.BlockSpec((k, tn), lambda i, j: (0, j))],
        out_specs=pl.BlockSpec((tm, tn), lambda i, j: (i, j)),
        out_shape=jax.ShapeDtypeStruct((m, n), out_dtype),
        compiler_params=_params(("parallel", "parallel")),
        name=name,
    )(a, b)


NT_DIMS = (((1,), (1,)), ((), ()))


def _rope(x, cos, sin_signed):
    return x * cos + pltpu.roll(x, HALF_LANES, 1) * sin_signed


def _heads_kernel(cq_ref, ckv_ref, kr_ref, gq_ref, gk_ref, gv_ref, wuq_ref, wuk_ref, wuvt_ref, eye_ref,
                  gcq_ref, gckv_ref, gqn_ref, gkn_ref, cosa_ref, sina_ref, cosb_ref, sinb_ref,
                  qa_ref, ka_ref, vat_ref, qb_ref, kb_ref, vbt_ref):
    cosa, sina = cosa_ref[...], sina_ref[...]
    cosb, sinb = cosb_ref[...], sinb_ref[...]
    scale_a = LOG2E / math.sqrt(MLA_NOPE + MLA_ROPE)
    scale_b = LOG2E / math.sqrt(HEAD_DIM)

    cq = _rms(cq_ref[...].astype(F32), gcq_ref[...]).astype(BF16)
    for h in range(MLA_HEADS):
        q = jnp.dot(cq, wuq_ref[:, h * MLA_QK:(h + 1) * MLA_QK], preferred_element_type=F32)
        qa_ref[:, h * MLA_QK:h * MLA_QK + MLA_NOPE] = (q[:, :MLA_NOPE] * scale_a).astype(BF16)
        qa_ref[:, h * MLA_QK + MLA_NOPE:(h + 1) * MLA_QK] = (
            _rope(q[:, MLA_NOPE:], cosa, sina) * scale_a).astype(BF16)

    ckv = _rms(ckv_ref[...].astype(F32), gckv_ref[...]).astype(BF16)
    k_rope = _rope(kr_ref[...].astype(F32), cosa, sina).astype(BF16)
    k_nope = jnp.dot(ckv, wuk_ref[...], preferred_element_type=F32).astype(BF16)
    for h in range(MLA_HEADS):
        ka_ref[:, h * MLA_QK:h * MLA_QK + MLA_NOPE] = k_nope[:, h * MLA_NOPE:(h + 1) * MLA_NOPE]
        ka_ref[:, h * MLA_QK + MLA_NOPE:(h + 1) * MLA_QK] = k_rope
    vat_ref[...] = lax.dot_general(wuvt_ref[...], ckv, NT_DIMS, preferred_element_type=F32).astype(BF16)
    vbt_ref[...] = lax.dot_general(eye_ref[...], gv_ref[...], NT_DIMS, preferred_element_type=F32).astype(BF16)

    gqn, gkn = gqn_ref[...], gkn_ref[...]
    for h in range(GQA_HEADS):
        x = gq_ref[:, h * HEAD_DIM:(h + 1) * HEAD_DIM].astype(F32)
        qb_ref[:, h * HEAD_DIM:(h + 1) * HEAD_DIM] = (_rope(_rms(x, gqn), cosb, sinb) * scale_b).astype(BF16)
    for h in range(GQA_KV_HEADS):
        x = gk_ref[:, h * HEAD_DIM:(h + 1) * HEAD_DIM].astype(F32)
        kb_ref[:, h * HEAD_DIM:(h + 1) * HEAD_DIM] = _rope(_rms(x, gkn), cosb, sinb).astype(BF16)


def _heads(z, w, tabs, tm, tab_blocks, name):
    m = z.shape[0]

    def zspec(width, off):
        return pl.BlockSpec((tm, width), lambda i: (i, off // width))

    def full(a):
        return pl.BlockSpec(a.shape, lambda i: (0,) * a.ndim)

    tab_spec = pl.BlockSpec((tm, LANES), lambda i: (i % tab_blocks, 0))
    outs = ((MLA_HEADS * MLA_QK, True), (MLA_HEADS * MLA_QK, True), (MLA_HEADS * MLA_V, False),
            (GQA_HEADS * HEAD_DIM, True), (GQA_KV_HEADS * HEAD_DIM, True), (GQA_KV_HEADS * HEAD_DIM, False))
    consts = (w["w_uq"], w["w_uk"], w["w_uvt"], w["eye_kv"], w["g_cq"], w["g_ckv"], w["g_qn"], w["g_kn"])
    return pl.pallas_call(
        _heads_kernel,
        grid=(m // tm,),
        in_specs=[zspec(MLA_Q_LORA, OFF_CQ), zspec(MLA_KV_LORA, OFF_CKV), zspec(LANES, OFF_KR),
                  zspec(GQA_HEADS * HEAD_DIM, OFF_GQ), zspec(GQA_KV_HEADS * HEAD_DIM, OFF_GK),
                  zspec(GQA_KV_HEADS * HEAD_DIM, OFF_GV)]
        + [full(a) for a in consts] + [tab_spec] * 4,
        out_specs=[pl.BlockSpec((tm, n), lambda i: (i, 0)) if rows else pl.BlockSpec((n, tm), lambda i: (0, i))
                   for n, rows in outs],
        out_shape=[jax.ShapeDtypeStruct((m, n) if rows else (n, m), BF16) for n, rows in outs],
        compiler_params=_params(("parallel",)),
        name=name,
    )(z, z, z, z, z, z, *consts, *tabs)


def _attn_kernel(tk, transpose_out, q_ref, k_ref, vt_ref, km_ref, vmt_ref, o_ref):
    q = q_ref[...]
    s = lax.dot_general(km_ref[...], q, NT_DIMS, preferred_element_type=F32)
    m = jnp.max(s, axis=0, keepdims=True)
    p = jnp.exp2(s - m)
    l = jnp.sum(p, axis=0, keepdims=True)
    acc = jnp.dot(vmt_ref[...], p.astype(BF16), preferred_element_type=F32)

    def scores(c):
        return lax.dot_general(k_ref[c * tk:(c + 1) * tk, :], q, NT_DIMS, preferred_element_type=F32)

    n_chunks = k_ref.shape[0] // tk
    s_next = scores(0)
    for c in range(n_chunks):
        s = s_next
        if c + 1 < n_chunks:
            s_next = scores(c + 1)
        m_new = jnp.maximum(m, jnp.max(s, axis=0, keepdims=True))
        alpha = jnp.exp2(m - m_new)
        p = jnp.exp2(s - m_new)
        l = alpha * l + jnp.sum(p, axis=0, keepdims=True)
        acc = alpha * acc + jnp.dot(vt_ref[:, c * tk:(c + 1) * tk], p.astype(BF16), preferred_element_type=F32)
        m = m_new
    o = acc / l
    o_ref[...] = (o.T if transpose_out else o).astype(o_ref.dtype)


def _attention(q, k, vt, km, vmt, *, batch, heads, group, dk, lq, tq, tk, name):
    nq = lq // tq
    dv = HEAD_DIM
    transpose_out = tq % LANES == 0
    if transpose_out:
        out_spec = pl.BlockSpec((tq, dv), lambda b, h, i: (b * nq + i, h))
        out_shape = jax.ShapeDtypeStruct((batch * lq, heads * dv), BF16)
    else:
        out_spec = pl.BlockSpec((None, dv, tq), lambda b, h, i: (b, h, i))
        out_shape = jax.ShapeDtypeStruct((batch, heads * dv, lq), BF16)
    return pl.pallas_call(
        functools.partial(_attn_kernel, tk, transpose_out),
        grid=(batch, heads, nq),
        in_specs=[
            pl.BlockSpec((tq, dk), lambda b, h, i: (b * nq + i, h)),
            pl.BlockSpec((SEQ, dk), lambda b, h, i: (b, h // group)),
            pl.BlockSpec((dv, SEQ), lambda b, h, i: (h // group, b)),
            pl.BlockSpec((N_META, dk), lambda b, h, i: (b, h // group)),
            pl.BlockSpec((None, dv, N_META), lambda b, h, i: (b, h // group, 0)),
        ],
        out_specs=out_spec,
        out_shape=out_shape,
        compiler_params=_params(("parallel", "parallel", "parallel")),
        name=name,
    )(q, k, vt, km, vmt)


def _merge_kernel(oa_ref, ob_ref, ga_ref, gb_ref, wpa_ref, wpb_ref, o_ref):
    a = jnp.dot(oa_ref[...], wpa_ref[...], preferred_element_type=F32)
    b = jnp.dot(ob_ref[...], wpb_ref[...], preferred_element_type=F32)
    ga = jax.nn.sigmoid(ga_ref[...].astype(F32))
    gb = jax.nn.sigmoid(gb_ref[...].astype(F32))
    o_ref[...] = (ga * a + gb * b).astype(o_ref.dtype)


def _merge(oa, ob, z, wpa, wpb, tm, tn, name):
    m, k = oa.shape
    return pl.pallas_call(
        _merge_kernel,
        grid=(m // tm, D_MODEL // tn),
        in_specs=[
            pl.BlockSpec((tm, k), lambda i, j: (i, 0)),
            pl.BlockSpec((tm, k), lambda i, j: (i, 0)),
            pl.BlockSpec((tm, tn), lambda i, j: (i, OFF_GA // tn + j)),
            pl.BlockSpec((tm, tn), lambda i, j: (i, OFF_GB // tn + j)),
            pl.BlockSpec((k, tn), lambda i, j: (0, j)),
            pl.BlockSpec((k, tn), lambda i, j: (0, j)),
        ],
        out_specs=pl.BlockSpec((tm, tn), lambda i, j: (i, j)),
        out_shape=jax.ShapeDtypeStruct((m, D_MODEL), BF16),
        compiler_params=_params(("parallel", "parallel")),
        name=name,
    )(oa, ob, z, z, wpa, wpb)


PROJ_CHUNK = 512
NORM_ROWS = 128
NORM_PIECE = 16


def _inv_rms(x):
    return lax.rsqrt(jnp.mean(x * x, axis=-1, keepdims=True) + EPS)


def _norm_residual(res_ref, g_ref, g2_ref, h_ref, hn_ref):
    n = h_ref.shape[0]
    for r0 in range(0, n, NORM_ROWS):
        blk = min(NORM_ROWS, n - r0)
        pieces = [(r0 + s0, min(NORM_PIECE, blk - s0), s0) for s0 in range(0, blk, NORM_PIECE)]
        r = _inv_rms(h_ref[r0:r0 + blk, :])
        for lo, sz, s0 in pieces:
            h_ref[lo:lo + sz, :] = res_ref[lo:lo + sz, :] + h_ref[lo:lo + sz, :] * r[s0:s0 + sz] * g_ref[...]
        if hn_ref is not None:
            r2 = _inv_rms(h_ref[r0:r0 + blk, :])
            for lo, sz, s0 in pieces:
                hn_ref[lo:lo + sz, :] = (h_ref[lo:lo + sz, :] * r2[s0:s0 + sz] * g2_ref[...]).astype(hn_ref.dtype)


def _proj_residual_kernel(nk, a_ref, w_ref, res_ref, g_ref, g2_ref, h_ref, hn_ref):
    kk = pl.program_id(1)

    def accumulate(first):
        a = a_ref[...]
        for c in range(0, D_MODEL, PROJ_CHUNK):
            part = jnp.dot(a, w_ref[:, c:c + PROJ_CHUNK], preferred_element_type=F32)
            if first:
                h_ref[:, c:c + PROJ_CHUNK] = part
            else:
                h_ref[:, c:c + PROJ_CHUNK] += part

    pl.when(kk == 0)(functools.partial(accumulate, True))
    pl.when(kk > 0)(functools.partial(accumulate, False))
    pl.when(kk == nk - 1)(functools.partial(_norm_residual, res_ref, g_ref, g2_ref, h_ref, hn_ref))


def _proj_residual_nohn_kernel(nk, a_ref, w_ref, res_ref, g_ref, h_ref):
    _proj_residual_kernel(nk, a_ref, w_ref, res_ref, g_ref, None, h_ref, None)


def _proj_residual(a, w, res, g, g2, tm, tk, name):
    m, k = a.shape
    nk = k // tk
    gspec = pl.BlockSpec((1, D_MODEL), lambda i, kk: (0, 0))
    row_spec = pl.BlockSpec((tm, D_MODEL), lambda i, kk: (i, 0))
    in_specs = [pl.BlockSpec((tm, tk), lambda i, kk: (i, kk)),
                pl.BlockSpec((tk, D_MODEL), lambda i, kk: (kk, 0)), row_spec, gspec]
    if g2 is None:
        return pl.pallas_call(
            functools.partial(_proj_residual_nohn_kernel, nk),
            grid=(m // tm, nk), in_specs=in_specs, out_specs=row_spec,
            out_shape=jax.ShapeDtypeStruct((m, D_MODEL), F32),
            compiler_params=_params(("parallel", "arbitrary")), name=name,
        )(a, w, res, g)
    return pl.pallas_call(
        functools.partial(_proj_residual_kernel, nk),
        grid=(m // tm, nk), in_specs=in_specs + [gspec], out_specs=[row_spec, row_spec],
        out_shape=[jax.ShapeDtypeStruct((m, D_MODEL), F32), jax.ShapeDtypeStruct((m, D_MODEL), BF16)],
        compiler_params=_params(("parallel", "arbitrary")), name=name,
    )(a, w, res, g, g2)


def _proj_resident_kernel(a_ref, w_ref, res_ref, g_ref, g2_ref, h_ref, hn_ref):
    a = a_ref[...]
    for c in range(0, D_MODEL, PROJ_CHUNK):
        h_ref[:, c:c + PROJ_CHUNK] = jnp.dot(a, w_ref[:, c:c + PROJ_CHUNK], preferred_element_type=F32)
    _norm_residual(res_ref, g_ref, g2_ref, h_ref, hn_ref)


def _proj_resident(a, w, res, g, g2, tm, name):
    m, k = a.shape
    gspec = pl.BlockSpec((1, D_MODEL), lambda i: (0, 0))
    row_spec = pl.BlockSpec((tm, D_MODEL), lambda i: (i, 0))
    return pl.pallas_call(
        _proj_resident_kernel,
        grid=(m // tm,),
        in_specs=[pl.BlockSpec((tm, k), lambda i: (i, 0)),
                  pl.BlockSpec((k, D_MODEL), lambda i: (0, 0), pipeline_mode=pl.Buffered(1)),
                  row_spec, gspec, gspec],
        out_specs=[row_spec, row_spec],
        out_shape=[jax.ShapeDtypeStruct((m, D_MODEL), F32), jax.ShapeDtypeStruct((m, D_MODEL), BF16)],
        compiler_params=_params(("parallel",)), name=name,
    )(a, w, res, g, g2)


HALO = N_META
FFN_ROW_BLOCKS = 4


GELU_K1 = -2.0 * math.sqrt(2.0 / math.pi) * LOG2E
GELU_K2 = GELU_K1 * 0.044715


def _gelu_tanh(x):
    return x / (1.0 + jnp.exp2(x * (x * x * GELU_K2 + GELU_K1)))


def _ffn_up_kernel(tm, tiles_per_seq, hn_ref, prev_ref, next_ref, hnm_ref, wa_ref, wg_ref, wc_ref, bc_ref,
                   f_ref, ext_ref):
    i = pl.program_id(0)

    @pl.when(pl.program_id(1) == 0)
    def _():
        first = i % tiles_per_seq == 0
        last = i % tiles_per_seq == tiles_per_seq - 1
        ext_ref[0:HALO, :] = jnp.where(first, hnm_ref[...], prev_ref[...])
        ext_ref[HALO:HALO + tm, :] = hn_ref[...]
        ext_ref[HALO + tm:, :] = jnp.where(last, jnp.zeros_like(next_ref), next_ref[...])

    wc = wc_ref[...]
    bc = bc_ref[...]
    n_chunks = f_ref.shape[1] // MXU_N
    rows_g = tm // FFN_ROW_BLOCKS
    rows_e = tm // (2 * FFN_ROW_BLOCKS)
    pad = 8

    def a_block(c, r):
        lo = 0 if r == 0 else HALO + r * rows_g
        hi = tm + 2 * HALO if r == FFN_ROW_BLOCKS - 1 else HALO + (r + 1) * rows_g
        return jnp.dot(ext_ref[lo:hi, :], wa_ref[:, c * MXU_N:(c + 1) * MXU_N], preferred_element_type=F32)

    def g_block(c, r):
        return jnp.dot(ext_ref[HALO + r * rows_g:HALO + (r + 1) * rows_g, :],
                       wg_ref[:, c * MXU_N:(c + 1) * MXU_N], preferred_element_type=F32)

    def epilogue_piece(c, a, gate, k):
        cols = slice(c * MXU_N, (c + 1) * MXU_N)
        lo = HALO + k * rows_e
        win = a[lo - pad:lo + rows_e + pad]
        prev = pltpu.roll(win, 1, 0)[pad:pad + rows_e]
        nxt = pltpu.roll(win, win.shape[0] - 1, 0)[pad:pad + rows_e]
        conv = prev * wc[0:1, cols] + win[pad:pad + rows_e] * wc[1:2, cols] + nxt * wc[2:3, cols] + bc[:, cols]
        g = gate[k * rows_e:(k + 1) * rows_e]
        f_ref[k * rows_e:(k + 1) * rows_e, cols] = (_gelu_tanh(conv) * g).astype(f_ref.dtype)

    a_next = jnp.concatenate([a_block(0, r) for r in range(FFN_ROW_BLOCKS)], axis=0)
    g_next = jnp.concatenate([g_block(0, r) for r in range(FFN_ROW_BLOCKS)], axis=0)
    for c in range(n_chunks):
        a_cur, g_cur = a_next, g_next
        a_parts, g_parts = [], []
        for k in range(2 * FFN_ROW_BLOCKS):
            if c + 1 < n_chunks:
                if k < FFN_ROW_BLOCKS:
                    a_parts.append(a_block(c + 1, k))
                else:
                    g_parts.append(g_block(c + 1, k - FFN_ROW_BLOCKS))
            epilogue_piece(c, a_cur, g_cur, k)
        if c + 1 < n_chunks:
            a_next = jnp.concatenate(a_parts, axis=0)
            g_next = jnp.concatenate(g_parts, axis=0)


def _ffn_up(hn, hn_meta, wa, wg, wc, bc, tm, tf, name):
    m = hn.shape[0]
    tiles_per_seq = SEQ // tm
    hb = tm // HALO
    last = m // HALO - 1
    return pl.pallas_call(
        functools.partial(_ffn_up_kernel, tm, tiles_per_seq),
        grid=(m // tm, D_FF_PAD // tf),
        in_specs=[
            pl.BlockSpec((tm, D_MODEL), lambda i, j: (i, 0)),
            pl.BlockSpec((HALO, D_MODEL), lambda i, j: (jnp.maximum(i * hb - 1, 0), 0)),
            pl.BlockSpec((HALO, D_MODEL), lambda i, j: (jnp.minimum((i + 1) * hb, last), 0)),
            pl.BlockSpec((N_META, D_MODEL), lambda i, j: (i // tiles_per_seq, 0)),
            pl.BlockSpec((D_MODEL, tf), lambda i, j: (0, j)),
            pl.BlockSpec((D_MODEL, tf), lambda i, j: (0, j)),
            pl.BlockSpec((3, tf), lambda i, j: (0, j)),
            pl.BlockSpec((1, tf), lambda i, j: (0, j)),
        ],
        out_specs=pl.BlockSpec((tm, tf), lambda i, j: (i, j)),
        out_shape=jax.ShapeDtypeStruct((m, D_FF_PAD), BF16),
        scratch_shapes=[pltpu.VMEM((tm + 2 * HALO, D_MODEL), BF16)],
        compiler_params=_params(("parallel", "arbitrary")),
        name=name,
    )(hn, hn, hn, hn_meta, wa, wg, wc, bc)


def _prep_weights(g_pre_mix, w_in, g_cq, w_uq, g_ckv, w_ukv, g_qn, g_kn, w_pa, w_pb, w_o,
                  g_post_mix, g_pre_ffn, w_up, w_conv, b_conv, w_down, g_post_ffn):
    qtr = HEAD_DIM // 4
    perm_b = np.arange(HEAD_DIM).reshape(2, 2, qtr).transpose(1, 0, 2).reshape(-1)
    o = np.cumsum((0, MLA_Q_LORA, MLA_KV_LORA, MLA_ROPE, GQA_HEADS * HEAD_DIM, GQA_KV_HEADS * HEAD_DIM,
                   GQA_KV_HEADS * HEAD_DIM, D_MODEL, D_MODEL))
    w = w_in[0]
    cq, ckv, kr, gq, gk, gv, ga, gb = (w[:, o[i]:o[i + 1]] for i in range(8))
    gq = gq.reshape(D_MODEL, GQA_HEADS, HEAD_DIM)[:, :, perm_b].reshape(D_MODEL, -1)
    gk = gk.reshape(D_MODEL, GQA_KV_HEADS, HEAD_DIM)[:, :, perm_b].reshape(D_MODEL, -1)
    half = MLA_ROPE // 2
    zr = jnp.zeros((D_MODEL, HALF_LANES - half), w.dtype)
    kr = jnp.concatenate([kr[:, :half], zr, kr[:, half:], zr], axis=1)
    tail = jnp.zeros((D_MODEL, D_IN_PAD - OFF_KR - LANES), w.dtype)
    w_in_p = jnp.concatenate([ga, gb, gq, cq, ckv, gk, gv, kr, tail], axis=1).astype(BF16)

    uq = w_uq[0].reshape(MLA_Q_LORA, MLA_HEADS, MLA_NOPE + MLA_ROPE)
    zq = jnp.zeros((MLA_Q_LORA, MLA_HEADS, HALF_LANES - half), uq.dtype)
    uq = jnp.concatenate([uq[:, :, :MLA_NOPE], uq[:, :, MLA_NOPE:MLA_NOPE + half], zq,
                          uq[:, :, MLA_NOPE + half:], zq], axis=2).reshape(MLA_Q_LORA, -1).astype(BF16)
    ukv = w_ukv[0].reshape(MLA_KV_LORA, MLA_HEADS, MLA_NOPE + MLA_V)
    uk = ukv[:, :, :MLA_NOPE].reshape(MLA_KV_LORA, -1).astype(BF16)
    uv = ukv[:, :, MLA_NOPE:].reshape(MLA_KV_LORA, -1).astype(BF16)

    fpad = D_FF_PAD - D_FF
    up = w_up[0]
    return dict(
        g_pre_mix=g_pre_mix, w_in=w_in_p, g_cq=g_cq, w_uq=uq, g_ckv=g_ckv, w_uk=uk, w_uvt=uv.T,
        eye_kv=jnp.eye(GQA_KV_HEADS * HEAD_DIM, dtype=BF16),
        g_qn=g_qn[:, perm_b], g_kn=g_kn[:, perm_b],
        w_pa=w_pa[0].astype(BF16), w_pb=w_pb[0].astype(BF16), w_o=w_o[0].astype(BF16),
        g_post_mix=g_post_mix, g_pre_ffn=g_pre_ffn,
        w_up_a=jnp.pad(up[:, :D_FF], ((0, 0), (0, fpad))).astype(BF16),
        w_up_g=jnp.pad(up[:, D_FF:], ((0, 0), (0, fpad))).astype(BF16),
        w_conv=jnp.pad(w_conv[0], ((0, 0), (0, fpad))), b_conv=jnp.pad(b_conv, ((0, 0), (0, fpad))),
        w_down=jnp.pad(w_down[0], ((0, fpad), (0, 0))).astype(BF16), g_post_ffn=g_post_ffn,
    )


def _rope_tables(pos, dim):
    inv = ROPE_THETA ** (-jnp.arange(0, dim, 2, dtype=F32) / dim)
    ang = pos.astype(F32)[:, None] * inv[None, :]
    return jnp.cos(ang), jnp.sin(ang)


def _tables(pos, row, col):
    c1, s1 = _rope_tables(pos, MLA_ROPE)
    z = jnp.zeros((c1.shape[0], HALF_LANES - c1.shape[1]), F32)
    cos_a = jnp.concatenate([c1, z, c1, z], axis=1)
    sin_a = jnp.concatenate([-s1, z, s1, z], axis=1)
    cr, sr = _rope_tables(row, HEAD_DIM // 2)
    cc, sc = _rope_tables(col, HEAD_DIM // 2)
    cos_b = jnp.concatenate([cr, cc, cr, cc], axis=1)
    sin_b = jnp.concatenate([-sr, -sc, sr, sc], axis=1)
    return cos_a, sin_a, cos_b, sin_b


def kernel(x_prompt, x_sample, meta_tokens, g_pre_mix, w_in, g_cq, w_uq, g_ckv, w_ukv, g_qn, g_kn,
           w_pa, w_pb, w_o, g_post_mix, g_pre_ffn, w_up, w_conv, b_conv, w_down, g_post_ffn):
    w = _prep_weights(g_pre_mix, w_in, g_cq, w_uq, g_ckv, w_ukv, g_qn, g_kn, w_pa, w_pb, w_o,
                      g_post_mix, g_pre_ffn, w_up, w_conv, b_conv, w_down, g_post_ffn)
    t = jnp.arange(SEQ, dtype=jnp.int32)
    tabs_real = _tables(t + N_META, t // GRID_W, t % GRID_W)
    zero = jnp.zeros((N_META,), jnp.int32)
    tabs_meta = _tables(jnp.arange(N_META), zero, zero)
    return (_trunk(x_prompt, meta_tokens, w, tabs_real, tabs_meta, "p"),
            _trunk(x_sample, meta_tokens, w, tabs_real, tabs_meta, "s"))


def _trunk(x, meta_tokens, w, tabs_real, tabs_meta, tag):
    batch = x.shape[0]
    xr = x.reshape(batch * SEQ, D_MODEL)
    xm = jnp.broadcast_to(meta_tokens[None], (batch, N_META, D_MODEL)).reshape(batch * N_META, D_MODEL)
    mrows = batch * N_META
    tabs_meta = tuple(jnp.tile(a, (batch, 1)) for a in tabs_meta)

    z = _matmul(_norm_rows(xr, w["g_pre_mix"], TM_NORM, "norm_" + tag), w["w_in"], TM_MM, TN_MM, BF16,
                "in_proj_" + tag)
    zm = _matmul(_norm_rows(xm, w["g_pre_mix"], mrows, "norm_meta_" + tag), w["w_in"], mrows, TN_MM, BF16,
                 "in_proj_meta_" + tag)
    qa, ka, vat, qb, kb, vbt = _heads(z, w, tabs_real, TM_HEADS, SEQ // TM_HEADS, "heads_" + tag)
    qam, kam, vatm, qbm, kbm, vbtm = _heads(zm, w, tabs_meta, mrows, 1, "heads_meta_" + tag)

    def per_batch(t):
        return t.reshape(t.shape[0], batch, N_META).transpose(1, 0, 2)

    def token_major(t):
        return t.transpose(0, 2, 1).reshape(batch * N_META, t.shape[1])

    vatm, vbtm = per_batch(vatm), per_batch(vbtm)

    cfg_a = dict(batch=batch, heads=MLA_HEADS, group=1, dk=MLA_QK)
    cfg_b = dict(batch=batch, heads=GQA_HEADS, group=GQA_GROUP, dk=HEAD_DIM)
    oa = _attention(qa, ka, vat, kam, vatm, lq=SEQ, tq=TQ_ATTN, tk=TK_ATTN, name="attn_a_" + tag, **cfg_a)
    ob = _attention(qb, kb, vbt, kbm, vbtm, lq=SEQ, tq=TQ_ATTN, tk=TK_ATTN, name="attn_b_" + tag, **cfg_b)
    oam = token_major(_attention(qam, ka, vat, kam, vatm, lq=N_META, tq=N_META, tk=TK_ATTN,
                                 name="attn_a_meta_" + tag, **cfg_a))
    obm = token_major(_attention(qbm, kb, vbt, kbm, vbtm, lq=N_META, tq=N_META, tk=TK_ATTN,
                                 name="attn_b_meta_" + tag, **cfg_b))

    mg = _merge(oa, ob, z, w["w_pa"], w["w_pb"], TM_MM, TN_MM, "merge_" + tag)
    mgm = _merge(oam, obm, zm, w["w_pa"], w["w_pb"], mrows, TN_MM, "merge_meta_" + tag)
    h1, hn = _proj_resident(mg, w["w_o"], xr, w["g_post_mix"], w["g_pre_ffn"], TM_OUT, "out_proj_" + tag)
    _, hnm = _proj_residual(mgm, w["w_o"], xm, w["g_post_mix"], w["g_pre_ffn"], mrows, TK_META,
                            "out_proj_meta_" + tag)

    f = _ffn_up(hn, hnm, w["w_up_a"], w["w_up_g"], w["w_conv"], w["b_conv"], TM_UP, TF_UP, "ffn_up_" + tag)
    y = _proj_residual(f, w["w_down"], h1, w["g_post_ffn"], None, TM_DOWN, TK_DOWN, "ffn_down_" + tag)
    return y.reshape(batch, SEQ, D_MODEL)
```

```python
import functools
import math

import jax
import jax.numpy as jnp
import numpy as np
from jax import lax
from jax.experimental import pallas as pl
from jax.experimental.pallas import tpu as pltpu

F32 = jnp.float32
BF16 = jnp.bfloat16

D_MODEL = 4096
SEQ = 4096
N_META = 16
GRID_W = 64
HEAD_DIM = 128
MLA_HEADS = 16
MLA_Q_LORA = 1024
MLA_KV_LORA = 512
MLA_NOPE = 128
MLA_ROPE = 64
MLA_V = 128
MLA_QK = 256
GQA_HEADS = 16
GQA_KV_HEADS = 4
GQA_GROUP = GQA_HEADS // GQA_KV_HEADS
D_FF = 11008
D_FF_PAD = 11264
ROPE_THETA = 10000.0
EPS = 1e-6
LOG2E = math.log2(math.e)

OFF_GA = 0
OFF_GB = 4096
OFF_GQ = 8192
OFF_CQ = 10240
OFF_CKV = 11264
OFF_GK = 11776
OFF_GV = 12288
OFF_KR = 12800
D_IN_PAD = 13312

LANES = 128
HALF_LANES = LANES // 2
MXU_N = 256
VMEM_LIMIT = 60 * 1024 * 1024

TM_NORM = 256
TM_MM, TN_MM = 1024, 1024
TM_HEADS = 256
TQ_ATTN, TK_ATTN = 1024, 2048
TM_OUT = 256
TM_UP, TF_UP = 1024, 512
TM_DOWN, TK_DOWN = 512, 1024
TK_META = 512


def _params(sem):
    return pltpu.CompilerParams(dimension_semantics=sem, vmem_limit_bytes=VMEM_LIMIT)


def _rms(x, g):
    return x * lax.rsqrt(jnp.mean(x * x, axis=-1, keepdims=True) + EPS) * g


def _norm_kernel(x_ref, g_ref, o_ref):
    o_ref[...] = _rms(x_ref[...], g_ref[...]).astype(o_ref.dtype)


def _norm_rows(x, g, tm, name):
    m = x.shape[0]
    return pl.pallas_call(
        _norm_kernel,
        grid=(m // tm,),
        in_specs=[pl.BlockSpec((tm, D_MODEL), lambda i: (i, 0)), pl.BlockSpec((1, D_MODEL), lambda i: (0, 0))],
        out_specs=pl.BlockSpec((tm, D_MODEL), lambda i: (i, 0)),
        out_shape=jax.ShapeDtypeStruct((m, D_MODEL), BF16),
        compiler_params=_params(("parallel",)),
        name=name,
    )(x, g)


def _mm_kernel(a_ref, b_ref, o_ref):
    o_ref[...] = jnp.dot(a_ref[...], b_ref[...], preferred_element_type=F32).astype(o_ref.dtype)


def _matmul(a, b, tm, tn, out_dtype, name):
    m, k = a.shape
    n = b.shape[1]
    return pl.pallas_call(
        _mm_kernel,
        grid=(m // tm, n // tn),
        in_specs=[pl.BlockSpec((tm, k), lambda i, j: (i, 0)), pl.BlockSpec((k, tn), lambda i, j: (0, j))],
        out_specs=pl.BlockSpec((tm, tn), lambda i, j: (i, j)),
        out_shape=jax.ShapeDtypeStruct((m, n), out_dtype),
        compiler_params=_params(("parallel", "parallel")),
        name=name,
    )(a, b)


NT_DIMS = (((1,), (1,)), ((), ()))


def _rope(x, cos, sin_signed):
    return x * cos + pltpu.roll(x, HALF_LANES, 1) * sin_signed


def _heads_kernel(cq_ref, ckv_ref, kr_ref, gq_ref, gk_ref, gv_ref, wuq_ref, wuk_ref, wuvt_ref, eye_ref,
                  gcq_ref, gckv_ref, gqn_ref, gkn_ref, cosa_ref, sina_ref, cosb_ref, sinb_ref,
                  qa_ref, ka_ref, vat_ref, qb_ref, kb_ref, vbt_ref):
    cosa, sina = cosa_ref[...], sina_ref[...]
    cosb, sinb = cosb_ref[...], sinb_ref[...]
    scale_a = LOG2E / math.sqrt(MLA_NOPE + MLA_ROPE)
    scale_b = LOG2E / math.sqrt(HEAD_DIM)

    cq = _rms(cq_ref[...].astype(F32), gcq_ref[...]).astype(BF16)
    for h in range(MLA_HEADS):
        q = jnp.dot(cq, wuq_ref[:, h * MLA_QK:(h + 1) * MLA_QK], preferred_element_type=F32)
        qa_ref[:, h * MLA_QK:h * MLA_QK + MLA_NOPE] = (q[:, :MLA_NOPE] * scale_a).astype(BF16)
        qa_ref[:, h * MLA_QK + MLA_NOPE:(h + 1) * MLA_QK] = (
            _rope(q[:, MLA_NOPE:], cosa, sina) * scale_a).astype(BF16)

    ckv = _rms(ckv_ref[...].astype(F32), gckv_ref[...]).astype(BF16)
    k_rope = _rope(kr_ref[...].astype(F32), cosa, sina).astype(BF16)
    k_nope = jnp.dot(ckv, wuk_ref[...], preferred_element_type=F32).astype(BF16)
    for h in range(MLA_HEADS):
        ka_ref[:, h * MLA_QK:h * MLA_QK + MLA_NOPE] = k_nope[:, h * MLA_NOPE:(h + 1) * MLA_NOPE]
        ka_ref[:, h * MLA_QK + MLA_NOPE:(h + 1) * MLA_QK] = k_rope
    vat_ref[...] = lax.dot_general(wuvt_ref[...], ckv, NT_DIMS, preferred_element_type=F32).astype(BF16)
    vbt_ref[...] = lax.dot_general(eye_ref[...], gv_ref[...], NT_DIMS, preferred_element_type=F32).astype(BF16)

    gqn, gkn = gqn_ref[...], gkn_ref[...]
    for h in range(GQA_HEADS):
        x = gq_ref[:, h * HEAD_DIM:(h + 1) * HEAD_DIM].astype(F32)
        qb_ref[:, h * HEAD_DIM:(h + 1) * HEAD_DIM] = (_rope(_rms(x, gqn), cosb, sinb) * scale_b).astype(BF16)
    for h in range(GQA_KV_HEADS):
        x = gk_ref[:, h * HEAD_DIM:(h + 1) * HEAD_DIM].astype(F32)
        kb_ref[:, h * HEAD_DIM:(h + 1) * HEAD_DIM] = _rope(_rms(x, gkn), cosb, sinb).astype(BF16)


def _heads(z, w, tabs, tm, tab_blocks, name):
    m = z.shape[0]

    def zspec(width, off):
        return pl.BlockSpec((tm, width), lambda i: (i, off // width))

    def full(a):
        return pl.BlockSpec(a.shape, lambda i: (0,) * a.ndim)

    tab_spec = pl.BlockSpec((tm, LANES), lambda i: (i % tab_blocks, 0))
    outs = ((MLA_HEADS * MLA_QK, True), (MLA_HEADS * MLA_QK, True), (MLA_HEADS * MLA_V, False),
            (GQA_HEADS * HEAD_DIM, True), (GQA_KV_HEADS * HEAD_DIM, True), (GQA_KV_HEADS * HEAD_DIM, False))
    consts = (w["w_uq"], w["w_uk"], w["w_uvt"], w["eye_kv"], w["g_cq"], w["g_ckv"], w["g_qn"], w["g_kn"])
    return pl.pallas_call(
        _heads_kernel,
        grid=(m // tm,),
        in_specs=[zspec(MLA_Q_LORA, OFF_CQ), zspec(MLA_KV_LORA, OFF_CKV), zspec(LANES, OFF_KR),
                  zspec(GQA_HEADS * HEAD_DIM, OFF_GQ), zspec(GQA_KV_HEADS * HEAD_DIM, OFF_GK),
                  zspec(GQA_KV_HEADS * HEAD_DIM, OFF_GV)]
        + [full(a) for a in consts] + [tab_spec] * 4,
        out_specs=[pl.BlockSpec((tm, n), lambda i: (i, 0)) if rows else pl.BlockSpec((n, tm), lambda i: (0, i))
                   for n, rows in outs],
        out_shape=[jax.ShapeDtypeStruct((m, n) if rows else (n, m), BF16) for n, rows in outs],
        compiler_params=_params(("parallel",)),
        name=name,
    )(z, z, z, z, z, z, *consts, *tabs)


def _attn_kernel(tk, transpose_out, q_ref, k_ref, vt_ref, km_ref, vmt_ref, o_ref):
    q = q_ref[...]
    s = lax.dot_general(km_ref[...], q, NT_DIMS, preferred_element_type=F32)
    m = jnp.max(s, axis=0, keepdims=True)
    p = jnp.exp2(s - m)
    l = jnp.sum(p, axis=0, keepdims=True)
    acc = jnp.dot(vmt_ref[...], p.astype(BF16), preferred_element_type=F32)

    def scores(c):
        return lax.dot_general(k_ref[c * tk:(c + 1) * tk, :], q, NT_DIMS, preferred_element_type=F32)

    n_chunks = k_ref.shape[0] // tk
    s_next = scores(0)
    for c in range(n_chunks):
        s = s_next
        if c + 1 < n_chunks:
            s_next = scores(c + 1)
        m_new = jnp.maximum(m, jnp.max(s, axis=0, keepdims=True))
        alpha = jnp.exp2(m - m_new)
        p = jnp.exp2(s - m_new)
        l = alpha * l + jnp.sum(p, axis=0, keepdims=True)
        acc = alpha * acc + jnp.dot(vt_ref[:, c * tk:(c + 1) * tk], p.astype(BF16), preferred_element_type=F32)
        m = m_new
    o = acc / l
    o_ref[...] = (o.T if transpose_out else o).astype(o_ref.dtype)


def _attention(q, k, vt, km, vmt, *, batch, heads, group, dk, lq, tq, tk, name):
    nq = lq // tq
    dv = HEAD_DIM
    transpose_out = tq % LANES == 0
    if transpose_out:
        out_spec = pl.BlockSpec((tq, dv), lambda b, h, i: (b * nq + i, h))
        out_shape = jax.ShapeDtypeStruct((batch * lq, heads * dv), BF16)
    else:
        out_spec = pl.BlockSpec((None, dv, tq), lambda b, h, i: (b, h, i))
        out_shape = jax.ShapeDtypeStruct((batch, heads * dv, lq), BF16)
    return pl.pallas_call(
        functools.partial(_attn_kernel, tk, transpose_out),
        grid=(batch, heads, nq),
        in_specs=[
            pl.BlockSpec((tq, dk), lambda b, h, i: (b * nq + i, h)),
            pl.BlockSpec((SEQ, dk), lambda b, h, i: (b, h // group)),
            pl.BlockSpec((dv, SEQ), lambda b, h, i: (h // group, b)),
            pl.BlockSpec((N_META, dk), lambda b, h, i: (b, h // group)),
            pl.BlockSpec((None, dv, N_META), lambda b, h, i: (b, h // group, 0)),
        ],
        out_specs=out_spec,
        out_shape=out_shape,
        compiler_params=_params(("parallel", "parallel", "parallel")),
        name=name,
    )(q, k, vt, km, vmt)


def _merge_kernel(oa_ref, ob_ref, ga_ref, gb_ref, wpa_ref, wpb_ref, o_ref):
    a = jnp.dot(oa_ref[...], wpa_ref[...], preferred_element_type=F32)
    b = jnp.dot(ob_ref[...], wpb_ref[...], preferred_element_type=F32)
    ga = jax.nn.sigmoid(ga_ref[...].astype(F32))
    gb = jax.nn.sigmoid(gb_ref[...].astype(F32))
    o_ref[...] = (ga * a + gb * b).astype(o_ref.dtype)


def _merge(oa, ob, z, wpa, wpb, tm, tn, name):
    m, k = oa.shape
    return pl.pallas_call(
        _merge_kernel,
        grid=(m // tm, D_MODEL // tn),
        in_specs=[
            pl.BlockSpec((tm, k), lambda i, j: (i, 0)),
            pl.BlockSpec((tm, k), lambda i, j: (i, 0)),
            pl.BlockSpec((tm, tn), lambda i, j: (i, OFF_GA // tn + j)),
            pl.BlockSpec((tm, tn), lambda i, j: (i, OFF_GB // tn + j)),
            pl.BlockSpec((k, tn), lambda i, j: (0, j)),
            pl.BlockSpec((k, tn), lambda i, j: (0, j)),
        ],
        out_specs=pl.BlockSpec((tm, tn), lambda i, j: (i, j)),
        out_shape=jax.ShapeDtypeStruct((m, D_MODEL), BF16),
        compiler_params=_params(("parallel", "parallel")),
        name=name,
    )(oa, ob, z, z, wpa, wpb)


PROJ_CHUNK = 512
NORM_ROWS = 128
NORM_PIECE = 16


def _inv_rms(x):
    return lax.rsqrt(jnp.mean(x * x, axis=-1, keepdims=True) + EPS)


def _norm_residual_items(res_ref, g_ref, g2_ref, h_ref, hn_ref, row_lo, row_hi):
    items = []
    for r0 in range(row_lo, row_hi, NORM_ROWS):
        blk = min(NORM_ROWS, row_hi - r0)
        pieces = [(r0 + s0, min(NORM_PIECE, blk - s0), s0) for s0 in range(0, blk, NORM_PIECE)]
        inv = {}

        def stats(key, r0=r0, blk=blk, inv=inv):
            inv[key] = _inv_rms(h_ref[r0:r0 + blk, :])

        def scale(lo, sz, s0, inv=inv):
            r = inv["h"][s0:s0 + sz]
            h_ref[lo:lo + sz, :] = res_ref[lo:lo + sz, :] + h_ref[lo:lo + sz, :] * r * g_ref[...]

        def scale_next(lo, sz, s0, inv=inv):
            r = inv["hn"][s0:s0 + sz]
            hn_ref[lo:lo + sz, :] = (h_ref[lo:lo + sz, :] * r * g2_ref[...]).astype(hn_ref.dtype)

        items.append(functools.partial(stats, "h"))
        items += [functools.partial(scale, *p) for p in pieces]
        if hn_ref is not None:
            items.append(functools.partial(stats, "hn"))
            items += [functools.partial(scale_next, *p) for p in pieces]
    return items


def _alternate(leaders, followers):
    n = len(leaders)
    for i, lead in enumerate(leaders):
        lead()
        for f in followers[i * len(followers) // n:(i + 1) * len(followers) // n]:
            f()


def _project_and_normalise(dot_into, res_ref, g_ref, g2_ref, h_ref, hn_ref):
    tm = h_ref.shape[0]
    half = tm // 2 if tm % (2 * NORM_ROWS) == 0 else tm
    cols = range(0, D_MODEL, PROJ_CHUNK)
    norm = functools.partial(_norm_residual_items, res_ref, g_ref, g2_ref, h_ref, hn_ref)
    for c in cols:
        dot_into(slice(0, half), c)
    if half < tm:
        _alternate([functools.partial(dot_into, slice(half, tm), c) for c in cols], norm(0, half))
    for item in norm(half if half < tm else 0, tm):
        item()


def _proj_residual_kernel(nk, a_ref, w_ref, res_ref, g_ref, g2_ref, h_ref, hn_ref):
    kk = pl.program_id(1)

    def dot_into(first, rows, c):
        part = jnp.dot(a_ref[rows, :], w_ref[:, c:c + PROJ_CHUNK], preferred_element_type=F32)
        if first:
            h_ref[rows, c:c + PROJ_CHUNK] = part
        else:
            h_ref[rows, c:c + PROJ_CHUNK] += part

    def accumulate(first):
        for c in range(0, D_MODEL, PROJ_CHUNK):
            dot_into(first, slice(None), c)

    def finish(first):
        _project_and_normalise(functools.partial(dot_into, first), res_ref, g_ref, g2_ref, h_ref, hn_ref)

    if nk == 1:
        finish(True)
    else:
        pl.when(kk == 0)(functools.partial(accumulate, True))
        pl.when(jnp.logical_and(kk > 0, kk < nk - 1))(functools.partial(accumulate, False))
        pl.when(kk == nk - 1)(functools.partial(finish, False))


def _proj_residual_nohn_kernel(nk, a_ref, w_ref, res_ref, g_ref, h_ref):
    _proj_residual_kernel(nk, a_ref, w_ref, res_ref, g_ref, None, h_ref, None)


def _proj_residual(a, w, res, g, g2, tm, tk, name):
    m, k = a.shape
    nk = k // tk
    gspec = pl.BlockSpec((1, D_MODEL), lambda i, kk: (0, 0))
    row_spec = pl.BlockSpec((tm, D_MODEL), lambda i, kk: (i, 0))
    in_specs = [pl.BlockSpec((tm, tk), lambda i, kk: (i, kk)),
                pl.BlockSpec((tk, D_MODEL), lambda i, kk: (kk, 0)), row_spec, gspec]
    if g2 is None:
        return pl.pallas_call(
            functools.partial(_proj_residual_nohn_kernel, nk),
            grid=(m // tm, nk), in_specs=in_specs, out_specs=row_spec,
            out_shape=jax.ShapeDtypeStruct((m, D_MODEL), F32),
            compiler_params=_params(("parallel", "arbitrary")), name=name,
        )(a, w, res, g)
    return pl.pallas_call(
        functools.partial(_proj_residual_kernel, nk),
        grid=(m // tm, nk), in_specs=in_specs + [gspec], out_specs=[row_spec, row_spec],
        out_shape=[jax.ShapeDtypeStruct((m, D_MODEL), F32), jax.ShapeDtypeStruct((m, D_MODEL), BF16)],
        compiler_params=_params(("parallel", "arbitrary")), name=name,
    )(a, w, res, g, g2)


def _proj_resident_kernel(a_ref, w_ref, res_ref, g_ref, g2_ref, h_ref, hn_ref):
    def dot_into(rows, c):
        h_ref[rows, c:c + PROJ_CHUNK] = jnp.dot(a_ref[rows, :], w_ref[:, c:c + PROJ_CHUNK],
                                                preferred_element_type=F32)

    _project_and_normalise(dot_into, res_ref, g_ref, g2_ref, h_ref, hn_ref)


def _proj_resident(a, w, res, g, g2, tm, name):
    m, k = a.shape
    gspec = pl.BlockSpec((1, D_MODEL), lambda i: (0, 0))
    row_spec = pl.BlockSpec((tm, D_MODEL), lambda i: (i, 0))
    return pl.pallas_call(
        _proj_resident_kernel,
        grid=(m // tm,),
        in_specs=[pl.BlockSpec((tm, k), lambda i: (i, 0)),
                  pl.BlockSpec((k, D_MODEL), lambda i: (0, 0), pipeline_mode=pl.Buffered(1)),
                  row_spec, gspec, gspec],
        out_specs=[row_spec, row_spec],
        out_shape=[jax.ShapeDtypeStruct((m, D_MODEL), F32), jax.ShapeDtypeStruct((m, D_MODEL), BF16)],
        compiler_params=_params(("parallel",)), name=name,
    )(a, w, res, g, g2)


HALO = N_META
FFN_ROW_BLOCKS = 4


GELU_K1 = -2.0 * math.sqrt(2.0 / math.pi) * LOG2E
GELU_K2 = GELU_K1 * 0.044715


def _gelu_tanh(x):
    return x / (1.0 + jnp.exp2(x * (x * x * GELU_K2 + GELU_K1)))


def _ffn_up_kernel(tm, tiles_per_seq, hn_ref, prev_ref, next_ref, hnm_ref, wa_ref, wg_ref, wc_ref, bc_ref,
                   f_ref, ext_ref):
    i = pl.program_id(0)

    @pl.when(pl.program_id(1) == 0)
    def _():
        first = i % tiles_per_seq == 0
        last = i % tiles_per_seq == tiles_per_seq - 1
        ext_ref[0:HALO, :] = jnp.where(first, hnm_ref[...], prev_ref[...])
        ext_ref[HALO:HALO + tm, :] = hn_ref[...]
        ext_ref[HALO + tm:, :] = jnp.where(last, jnp.zeros_like(next_ref), next_ref[...])

    wc = wc_ref[...]
    bc = bc_ref[...]
    n_chunks = f_ref.shape[1] // MXU_N
    rows_g = tm // FFN_ROW_BLOCKS
    rows_e = tm // (2 * FFN_ROW_BLOCKS)
    pad = 8

    def a_block(c, r):
        lo = 0 if r == 0 else HALO + r * rows_g
        hi = tm + 2 * HALO if r == FFN_ROW_BLOCKS - 1 else HALO + (r + 1) * rows_g
        return jnp.dot(ext_ref[lo:hi, :], wa_ref[:, c * MXU_N:(c + 1) * MXU_N], preferred_element_type=F32)

    def g_block(c, r):
        return jnp.dot(ext_ref[HALO + r * rows_g:HALO + (r + 1) * rows_g, :],
                       wg_ref[:, c * MXU_N:(c + 1) * MXU_N], preferred_element_type=F32)

    def epilogue_piece(c, a, gate, k):
        cols = slice(c * MXU_N, (c + 1) * MXU_N)
        lo = HALO + k * rows_e
        win = a[lo - pad:lo + rows_e + pad]
        prev = pltpu.roll(win, 1, 0)[pad:pad + rows_e]
        nxt = pltpu.roll(win, win.shape[0] - 1, 0)[pad:pad + rows_e]
        conv = prev * wc[0:1, cols] + win[pad:pad + rows_e] * wc[1:2, cols] + nxt * wc[2:3, cols] + bc[:, cols]
        g = gate[k * rows_e:(k + 1) * rows_e]
        f_ref[k * rows_e:(k + 1) * rows_e, cols] = (_gelu_tanh(conv) * g).astype(f_ref.dtype)

    a_next = jnp.concatenate([a_block(0, r) for r in range(FFN_ROW_BLOCKS)], axis=0)
    g_next = jnp.concatenate([g_block(0, r) for r in range(FFN_ROW_BLOCKS)], axis=0)
    for c in range(n_chunks):
        a_cur, g_cur = a_next, g_next
        a_parts, g_parts = [], []
        for k in range(2 * FFN_ROW_BLOCKS):
            if c + 1 < n_chunks:
                if k < FFN_ROW_BLOCKS:
                    a_parts.append(a_block(c + 1, k))
                else:
                    g_parts.append(g_block(c + 1, k - FFN_ROW_BLOCKS))
            epilogue_piece(c, a_cur, g_cur, k)
        if c + 1 < n_chunks:
            a_next = jnp.concatenate(a_parts, axis=0)
            g_next = jnp.concatenate(g_parts, axis=0)


def _ffn_up(hn, hn_meta, wa, wg, wc, bc, tm, tf, name):
    m = hn.shape[0]
    tiles_per_seq = SEQ // tm
    hb = tm // HALO
    last = m // HALO - 1
    return pl.pallas_call(
        functools.partial(_ffn_up_kernel, tm, tiles_per_seq),
        grid=(m // tm, D_FF_PAD // tf),
        in_specs=[
            pl.BlockSpec((tm, D_MODEL), lambda i, j: (i, 0)),
            pl.BlockSpec((HALO, D_MODEL), lambda i, j: (jnp.maximum(i * hb - 1, 0), 0)),
            pl.BlockSpec((HALO, D_MODEL), lambda i, j: (jnp.minimum((i + 1) * hb, last), 0)),
            pl.BlockSpec((N_META, D_MODEL), lambda i, j: (i // tiles_per_seq, 0)),
            pl.BlockSpec((D_MODEL, tf), lambda i, j: (0, j)),
            pl.BlockSpec((D_MODEL, tf), lambda i, j: (0, j)),
            pl.BlockSpec((3, tf), lambda i, j: (0, j)),
            pl.BlockSpec((1, tf), lambda i, j: (0, j)),
        ],
        out_specs=pl.BlockSpec((tm, tf), lambda i, j: (i, j)),
        out_shape=jax.ShapeDtypeStruct((m, D_FF_PAD), BF16),
        scratch_shapes=[pltpu.VMEM((tm + 2 * HALO, D_MODEL), BF16)],
        compiler_params=_params(("parallel", "arbitrary")),
        name=name,
    )(hn, hn, hn, hn_meta, wa, wg, wc, bc)


def _prep_weights(g_pre_mix, w_in, g_cq, w_uq, g_ckv, w_ukv, g_qn, g_kn, w_pa, w_pb, w_o,
                  g_post_mix, g_pre_ffn, w_up, w_conv, b_conv, w_down, g_post_ffn):
    qtr = HEAD_DIM // 4
    perm_b = np.arange(HEAD_DIM).reshape(2, 2, qtr).transpose(1, 0, 2).reshape(-1)
    o = np.cumsum((0, MLA_Q_LORA, MLA_KV_LORA, MLA_ROPE, GQA_HEADS * HEAD_DIM, GQA_KV_HEADS * HEAD_DIM,
                   GQA_KV_HEADS * HEAD_DIM, D_MODEL, D_MODEL))
    w = w_in[0]
    cq, ckv, kr, gq, gk, gv, ga, gb = (w[:, o[i]:o[i + 1]] for i in range(8))
    gq = gq.reshape(D_MODEL, GQA_HEADS, HEAD_DIM)[:, :, perm_b].reshape(D_MODEL, -1)
    gk = gk.reshape(D_MODEL, GQA_KV_HEADS, HEAD_DIM)[:, :, perm_b].reshape(D_MODEL, -1)
    half = MLA_ROPE // 2
    zr = jnp.zeros((D_MODEL, HALF_LANES - half), w.dtype)
    kr = jnp.concatenate([kr[:, :half], zr, kr[:, half:], zr], axis=1)
    tail = jnp.zeros((D_MODEL, D_IN_PAD - OFF_KR - LANES), w.dtype)
    w_in_p = jnp.concatenate([ga, gb, gq, cq, ckv, gk, gv, kr, tail], axis=1).astype(BF16)

    uq = w_uq[0].reshape(MLA_Q_LORA, MLA_HEADS, MLA_NOPE + MLA_ROPE)
    zq = jnp.zeros((MLA_Q_LORA, MLA_HEADS, HALF_LANES - half), uq.dtype)
    uq = jnp.concatenate([uq[:, :, :MLA_NOPE], uq[:, :, MLA_NOPE:MLA_NOPE + half], zq,
                          uq[:, :, MLA_NOPE + half:], zq], axis=2).reshape(MLA_Q_LORA, -1).astype(BF16)
    ukv = w_ukv[0].reshape(MLA_KV_LORA, MLA_HEADS, MLA_NOPE + MLA_V)
    uk = ukv[:, :, :MLA_NOPE].reshape(MLA_KV_LORA, -1).astype(BF16)
    uv = ukv[:, :, MLA_NOPE:].reshape(MLA_KV_LORA, -1).astype(BF16)

    fpad = D_FF_PAD - D_FF
    up = w_up[0]
    return dict(
        g_pre_mix=g_pre_mix, w_in=w_in_p, g_cq=g_cq, w_uq=uq, g_ckv=g_ckv, w_uk=uk, w_uvt=uv.T,
        eye_kv=jnp.eye(GQA_KV_HEADS * HEAD_DIM, dtype=BF16),
        g_qn=g_qn[:, perm_b], g_kn=g_kn[:, perm_b],
        w_pa=w_pa[0].astype(BF16), w_pb=w_pb[0].astype(BF16), w_o=w_o[0].astype(BF16),
        g_post_mix=g_post_mix, g_pre_ffn=g_pre_ffn,
        w_up_a=jnp.pad(up[:, :D_FF], ((0, 0), (0, fpad))).astype(BF16),
        w_up_g=jnp.pad(up[:, D_FF:], ((0, 0), (0, fpad))).astype(BF16),
        w_conv=jnp.pad(w_conv[0], ((0, 0), (0, fpad))), b_conv=jnp.pad(b_conv, ((0, 0), (0, fpad))),
        w_down=jnp.pad(w_down[0], ((0, fpad), (0, 0))).astype(BF16), g_post_ffn=g_post_ffn,
    )


def _rope_tables(pos, dim):
    inv = ROPE_THETA ** (-jnp.arange(0, dim, 2, dtype=F32) / dim)
    ang = pos.astype(F32)[:, None] * inv[None, :]
    return jnp.cos(ang), jnp.sin(ang)


def _tables(pos, row, col):
    c1, s1 = _rope_tables(pos, MLA_ROPE)
    z = jnp.zeros((c1.shape[0], HALF_LANES - c1.shape[1]), F32)
    cos_a = jnp.concatenate([c1, z, c1, z], axis=1)
    sin_a = jnp.concatenate([-s1, z, s1, z], axis=1)
    cr, sr = _rope_tables(row, HEAD_DIM // 2)
    cc, sc = _rope_tables(col, HEAD_DIM // 2)
    cos_b = jnp.concatenate([cr, cc, cr, cc], axis=1)
    sin_b = jnp.concatenate([-sr, -sc, sr, sc], axis=1)
    return cos_a, sin_a, cos_b, sin_b


def kernel(x_prompt, x_sample, meta_tokens, g_pre_mix, w_in, g_cq, w_uq, g_ckv, w_ukv, g_qn, g_kn,
           w_pa, w_pb, w_o, g_post_mix, g_pre_ffn, w_up, w_conv, b_conv, w_down, g_post_ffn):
    w = _prep_weights(g_pre_mix, w_in, g_cq, w_uq, g_ckv, w_ukv, g_qn, g_kn, w_pa, w_pb, w_o,
                      g_post_mix, g_pre_ffn, w_up, w_conv, b_conv, w_down, g_post_ffn)
    t = jnp.arange(SEQ, dtype=jnp.int32)
    tabs_real = _tables(t + N_META, t // GRID_W, t % GRID_W)
    zero = jnp.zeros((N_META,), jnp.int32)
    tabs_meta = _tables(jnp.arange(N_META), zero, zero)
    return (_trunk(x_prompt, meta_tokens, w, tabs_real, tabs_meta, "p"),
            _trunk(x_sample, meta_tokens, w, tabs_real, tabs_meta, "s"))


def _trunk(x, meta_tokens, w, tabs_real, tabs_meta, tag):
    batch = x.shape[0]
    xr = x.reshape(batch * SEQ, D_MODEL)
    xm = jnp.broadcast_to(meta_tokens[None], (batch, N_META, D_MODEL)).reshape(batch * N_META, D_MODEL)
    mrows = batch * N_META
    tabs_meta = tuple(jnp.tile(a, (batch, 1)) for a in tabs_meta)

    z = _matmul(_norm_rows(xr, w["g_pre_mix"], TM_NORM, "norm_" + tag), w["w_in"], TM_MM, TN_MM, BF16,
                "in_proj_" + tag)
    zm = _matmul(_norm_rows(xm, w["g_pre_mix"], mrows, "norm_meta_" + tag), w["w_in"], mrows, TN_MM, BF16,
                 "in_proj_meta_" + tag)
    qa, ka, vat, qb, kb, vbt = _heads(z, w, tabs_real, TM_HEADS, SEQ // TM_HEADS, "heads_" + tag)
    qam, kam, vatm, qbm, kbm, vbtm = _heads(zm, w, tabs_meta, mrows, 1, "heads_meta_" + tag)

    def per_batch(t):
        return t.reshape(t.shape[0], batch, N_META).transpose(1, 0, 2)

    def token_major(t):
        return t.transpose(0, 2, 1).reshape(batch * N_META, t.shape[1])

    vatm, vbtm = per_batch(vatm), per_batch(vbtm)

    cfg_a = dict(batch=batch, heads=MLA_HEADS, group=1, dk=MLA_QK)
    cfg_b = dict(batch=batch, heads=GQA_HEADS, group=GQA_GROUP, dk=HEAD_DIM)
    oa = _attention(qa, ka, vat, kam, vatm, lq=SEQ, tq=TQ_ATTN, tk=TK_ATTN, name="attn_a_" + tag, **cfg_a)
    ob = _attention(qb, kb, vbt, kbm, vbtm, lq=SEQ, tq=TQ_ATTN, tk=TK_ATTN, name="attn_b_" + tag, **cfg_b)
    oam = token_major(_attention(qam, ka, vat, kam, vatm, lq=N_META, tq=N_META, tk=TK_ATTN,
                                 name="attn_a_meta_" + tag, **cfg_a))
    obm = token_major(_attention(qbm, kb, vbt, kbm, vbtm, lq=N_META, tq=N_META, tk=TK_ATTN,
                                 name="attn_b_meta_" + tag, **cfg_b))

    mg = _merge(oa, ob, z, w["w_pa"], w["w_pb"], TM_MM, TN_MM, "merge_" + tag)
    mgm = _merge(oam, obm, zm, w["w_pa"], w["w_pb"], mrows, TN_MM, "merge_meta_" + tag)
    h1, hn = _proj_resident(mg, w["w_o"], xr, w["g_post_mix"], w["g_pre_ffn"], TM_OUT, "out_proj_" + tag)
    _, hnm = _proj_residual(mgm, w["w_o"], xm, w["g_post_mix"], w["g_pre_ffn"], mrows, TK_META,
                            "out_proj_meta_" + tag)

    f = _ffn_up(hn, hnm, w["w_up_a"], w["w_up_g"], w["w_conv"], w["b_conv"], TM_UP, TF_UP, "ffn_up_" + tag)
    y = _proj_residual(f, w["w_down"], h1, w["g_post_ffn"], None, TM_DOWN, TK_DOWN, "ffn_down_" + tag)
    return y.reshape(batch, SEQ, D_MODEL)
```

```python
import functools
import math

import jax
import jax.numpy as jnp
import numpy as np
from jax import lax
from jax.experimental import pallas as pl
from jax.experimental.pallas import tpu as pltpu

F32 = jnp.float32
BF16 = jnp.bfloat16

D_MODEL = 4096
SEQ = 4096
N_META = 16
GRID_W = 64
HEAD_DIM = 128
MLA_HEADS = 16
MLA_Q_LORA = 1024
MLA_KV_LORA = 512
MLA_NOPE = 128
MLA_ROPE = 64
MLA_V = 128
MLA_QK = 256
GQA_HEADS = 16
GQA_KV_HEADS = 4
GQA_GROUP = GQA_HEADS // GQA_KV_HEADS
D_FF = 11008
D_FF_PAD = 11264
ROPE_THETA = 10000.0
EPS = 1e-6
LOG2E = math.log2(math.e)

OFF_GA = 0
OFF_GB = 4096
OFF_GQ = 8192
OFF_CQ = 10240
OFF_CKV = 11264
OFF_GK = 11776
OFF_GV = 12288
OFF_KR = 12800
D_IN_PAD = 13312

LANES = 128
HALF_LANES = LANES // 2
MXU_N = 256
VMEM_LIMIT = 60 * 1024 * 1024

TM_NORM = 256
TM_MM, TN_MM = 1024, 1024
TM_HEADS = 256
TQ_ATTN, TK_ATTN = 1024, 2048
TM_OUT = 256
TM_UP, TF_UP = 1024, 512
TM_DOWN, TK_DOWN = 512, 1024
TK_META = 512


def _params(sem):
    return pltpu.CompilerParams(dimension_semantics=sem, vmem_limit_bytes=VMEM_LIMIT)


def _rms(x, g):
    return x * lax.rsqrt(jnp.mean(x * x, axis=-1, keepdims=True) + EPS) * g


def _norm_kernel(x_ref, g_ref, o_ref):
    o_ref[...] = _rms(x_ref[...], g_ref[...]).astype(o_ref.dtype)


def _norm_rows(x, g, tm, name):
    m = x.shape[0]
    return pl.pallas_call(
        _norm_kernel,
        grid=(m // tm,),
        in_specs=[pl.BlockSpec((tm, D_MODEL), lambda i: (i, 0)), pl.BlockSpec((1, D_MODEL), lambda i: (0, 0))],
        out_specs=pl.BlockSpec((tm, D_MODEL), lambda i: (i, 0)),
        out_shape=jax.ShapeDtypeStruct((m, D_MODEL), BF16),
        compiler_params=_params(("parallel",)),
        name=name,
    )(x, g)


def _mm_kernel(a_ref, b_ref, o_ref):
    o_ref[...] = jnp.dot(a_ref[...], b_ref[...], preferred_element_type=F32).astype(o_ref.dtype)


def _matmul(a, b, tm, tn, out_dtype, name):
    m, k = a.shape
    n = b.shape[1]
    return pl.pallas_call(
        _mm_kernel,
        grid=(m // tm, n // tn),
        in_specs=[pl.BlockSpec((tm, k), lambda i, j: (i, 0)), pl.BlockSpec((k, tn), lambda i, j: (0, j))],
        out_specs=pl.BlockSpec((tm, tn), lambda i, j: (i, j)),
        out_shape=jax.ShapeDtypeStruct((m, n), out_dtype),
        compiler_params=_params(("parallel", "parallel")),
        name=name,
    )(a, b)


NT_DIMS = (((1,), (1,)), ((), ()))


def _rope(x, cos, sin_signed):
    return x * cos + pltpu.roll(x, HALF_LANES, 1) * sin_signed


def _heads_kernel(cq_ref, ckv_ref, kr_ref, gq_ref, gk_ref, gv_ref, wuq_ref, wuk_ref, wuvt_ref, eye_ref,
                  gcq_ref, gckv_ref, gqn_ref, gkn_ref, cosa_ref, sina_ref, cosb_ref, sinb_ref,
                  qa_ref, ka_ref, vat_ref, qb_ref, kb_ref, vbt_ref):
    cosa, sina = cosa_ref[...], sina_ref[...]
    cosb, sinb = cosb_ref[...], sinb_ref[...]
    scale_a = LOG2E / math.sqrt(MLA_NOPE + MLA_ROPE)
    scale_b = LOG2E / math.sqrt(HEAD_DIM)

    cq = _rms(cq_ref[...].astype(F32), gcq_ref[...]).astype(BF16)
    for h in range(MLA_HEADS):
        q = jnp.dot(cq, wuq_ref[:, h * MLA_QK:(h + 1) * MLA_QK], preferred_element_type=F32)
        qa_ref[:, h * MLA_QK:h * MLA_QK + MLA_NOPE] = (q[:, :MLA_NOPE] * scale_a).astype(BF16)
        qa_ref[:, h * MLA_QK + MLA_NOPE:(h + 1) * MLA_QK] = (
            _rope(q[:, MLA_NOPE:], cosa, sina) * scale_a).astype(BF16)

    ckv = _rms(ckv_ref[...].astype(F32), gckv_ref[...]).astype(BF16)
    k_rope = _rope(kr_ref[...].astype(F32), cosa, sina).astype(BF16)
    k_nope = jnp.dot(ckv, wuk_ref[...], preferred_element_type=F32).astype(BF16)
    for h in range(MLA_HEADS):
        ka_ref[:, h * MLA_QK:h * MLA_QK + MLA_NOPE] = k_nope[:, h * MLA_NOPE:(h + 1) * MLA_NOPE]
        ka_ref[:, h * MLA_QK + MLA_NOPE:(h + 1) * MLA_QK] = k_rope
    vat_ref[...] = lax.dot_general(wuvt_ref[...], ckv, NT_DIMS, preferred_element_type=F32).astype(BF16)
    vbt_ref[...] = lax.dot_general(eye_ref[...], gv_ref[...], NT_DIMS, preferred_element_type=F32).astype(BF16)

    gqn, gkn = gqn_ref[...], gkn_ref[...]
    for h in range(GQA_HEADS):
        x = gq_ref[:, h * HEAD_DIM:(h + 1) * HEAD_DIM].astype(F32)
        qb_ref[:, h * HEAD_DIM:(h + 1) * HEAD_DIM] = (_rope(_rms(x, gqn), cosb, sinb) * scale_b).astype(BF16)
    for h in range(GQA_KV_HEADS):
        x = gk_ref[:, h * HEAD_DIM:(h + 1) * HEAD_DIM].astype(F32)
        kb_ref[:, h * HEAD_DIM:(h + 1) * HEAD_DIM] = _rope(_rms(x, gkn), cosb, sinb).astype(BF16)


def _heads(z, w, tabs, tm, tab_blocks, name):
    m = z.shape[0]

    def zspec(width, off):
        return pl.BlockSpec((tm, width), lambda i: (i, off // width))

    def full(a):
        return pl.BlockSpec(a.shape, lambda i: (0,) * a.ndim)

    tab_spec = pl.BlockSpec((tm, LANES), lambda i: (i % tab_blocks, 0))
    outs = ((MLA_HEADS * MLA_QK, True), (MLA_HEADS * MLA_QK, True), (MLA_HEADS * MLA_V, False),
            (GQA_HEADS * HEAD_DIM, True), (GQA_KV_HEADS * HEAD_DIM, True), (GQA_KV_HEADS * HEAD_DIM, False))
    consts = (w["w_uq"], w["w_uk"], w["w_uvt"], w["eye_kv"], w["g_cq"], w["g_ckv"], w["g_qn"], w["g_kn"])
    return pl.pallas_call(
        _heads_kernel,
        grid=(m // tm,),
        in_specs=[zspec(MLA_Q_LORA, OFF_CQ), zspec(MLA_KV_LORA, OFF_CKV), zspec(LANES, OFF_KR),
                  zspec(GQA_HEADS * HEAD_DIM, OFF_GQ), zspec(GQA_KV_HEADS * HEAD_DIM, OFF_GK),
                  zspec(GQA_KV_HEADS * HEAD_DIM, OFF_GV)]
        + [full(a) for a in consts] + [tab_spec] * 4,
        out_specs=[pl.BlockSpec((tm, n), lambda i: (i, 0)) if rows else pl.BlockSpec((n, tm), lambda i: (0, i))
                   for n, rows in outs],
        out_shape=[jax.ShapeDtypeStruct((m, n) if rows else (n, m), BF16) for n, rows in outs],
        compiler_params=_params(("parallel",)),
        name=name,
    )(z, z, z, z, z, z, *consts, *tabs)


def _attn_kernel(tk, transpose_out, q_ref, k_ref, vt_ref, km_ref, vmt_ref, o_ref):
    q = q_ref[...]
    s = lax.dot_general(km_ref[...], q, NT_DIMS, preferred_element_type=F32)
    m = jnp.max(s, axis=0, keepdims=True)
    p = jnp.exp2(s - m)
    l = jnp.sum(p, axis=0, keepdims=True)
    acc = jnp.dot(vmt_ref[...], p.astype(BF16), preferred_element_type=F32)

    def scores(c):
        return lax.dot_general(k_ref[c * tk:(c + 1) * tk, :], q, NT_DIMS, preferred_element_type=F32)

    n_chunks = k_ref.shape[0] // tk
    s_next = scores(0)
    for c in range(n_chunks):
        s = s_next
        if c + 1 < n_chunks:
            s_next = scores(c + 1)
        m_new = jnp.maximum(m, jnp.max(s, axis=0, keepdims=True))
        alpha = jnp.exp2(m - m_new)
        p = jnp.exp2(s - m_new)
        l = alpha * l + jnp.sum(p, axis=0, keepdims=True)
        acc = alpha * acc + jnp.dot(vt_ref[:, c * tk:(c + 1) * tk], p.astype(BF16), preferred_element_type=F32)
        m = m_new
    o = acc / l
    o_ref[...] = (o.T if transpose_out else o).astype(o_ref.dtype)


def _attention(q, k, vt, km, vmt, *, batch, heads, group, dk, lq, tq, tk, name):
    nq = lq // tq
    dv = HEAD_DIM
    transpose_out = tq % LANES == 0
    if transpose_out:
        out_spec = pl.BlockSpec((tq, dv), lambda b, h, i: (b * nq + i, h))
        out_shape = jax.ShapeDtypeStruct((batch * lq, heads * dv), BF16)
    else:
        out_spec = pl.BlockSpec((None, dv, tq), lambda b, h, i: (b, h, i))
        out_shape = jax.ShapeDtypeStruct((batch, heads * dv, lq), BF16)
    return pl.pallas_call(
        functools.partial(_attn_kernel, tk, transpose_out),
        grid=(batch, heads, nq),
        in_specs=[
            pl.BlockSpec((tq, dk), lambda b, h, i: (b * nq + i, h)),
            pl.BlockSpec((SEQ, dk), lambda b, h, i: (b, h // group)),
            pl.BlockSpec((dv, SEQ), lambda b, h, i: (h // group, b)),
            pl.BlockSpec((N_META, dk), lambda b, h, i: (b, h // group)),
            pl.BlockSpec((None, dv, N_META), lambda b, h, i: (b, h // group, 0)),
        ],
        out_specs=out_spec,
        out_shape=out_shape,
        compiler_params=_params(("parallel", "parallel", "parallel")),
        name=name,
    )(q, k, vt, km, vmt)


def _merge_kernel(oa_ref, ob_ref, ga_ref, gb_ref, wpa_ref, wpb_ref, o_ref):
    a = jnp.dot(oa_ref[...], wpa_ref[...], preferred_element_type=F32)
    b = jnp.dot(ob_ref[...], wpb_ref[...], preferred_element_type=F32)
    ga = jax.nn.sigmoid(ga_ref[...].astype(F32))
    gb = jax.nn.sigmoid(gb_ref[...].astype(F32))
    o_ref[...] = (ga * a + gb * b).astype(o_ref.dtype)


def _merge(oa, ob, z, wpa, wpb, tm, tn, name):
    m, k = oa.shape
    return pl.pallas_call(
        _merge_kernel,
        grid=(m // tm, D_MODEL // tn),
        in_specs=[
            pl.BlockSpec((tm, k), lambda i, j: (i, 0)),
            pl.BlockSpec((tm, k), lambda i, j: (i, 0)),
            pl.BlockSpec((tm, tn), lambda i, j: (i, OFF_GA // tn + j)),
            pl.BlockSpec((tm, tn), lambda i, j: (i, OFF_GB // tn + j)),
            pl.BlockSpec((k, tn), lambda i, j: (0, j)),
            pl.BlockSpec((k, tn), lambda i, j: (0, j)),
        ],
        out_specs=pl.BlockSpec((tm, tn), lambda i, j: (i, j)),
        out_shape=jax.ShapeDtypeStruct((m, D_MODEL), BF16),
        compiler_params=_params(("parallel", "parallel")),
        name=name,
    )(oa, ob, z, z, wpa, wpb)


PROJ_CHUNK = 512
NORM_ROWS = 128
NORM_PIECE = 16


def _inv_rms(x):
    return lax.rsqrt(jnp.mean(x * x, axis=-1, keepdims=True) + EPS)


def _norm_residual_items(res_ref, g_ref, g2_ref, h_ref, hn_ref, row_lo, row_hi):
    items = []
    for r0 in range(row_lo, row_hi, NORM_ROWS):
        blk = min(NORM_ROWS, row_hi - r0)
        pieces = [(r0 + s0, min(NORM_PIECE, blk - s0), s0) for s0 in range(0, blk, NORM_PIECE)]
        inv = {}

        def stats(key, r0=r0, blk=blk, inv=inv):
            inv[key] = _inv_rms(h_ref[r0:r0 + blk, :])

        def scale(lo, sz, s0, inv=inv):
            r = inv["h"][s0:s0 + sz]
            h_ref[lo:lo + sz, :] = res_ref[lo:lo + sz, :] + h_ref[lo:lo + sz, :] * r * g_ref[...]

        def scale_next(lo, sz, s0, inv=inv):
            r = inv["hn"][s0:s0 + sz]
            hn_ref[lo:lo + sz, :] = (h_ref[lo:lo + sz, :] * r * g2_ref[...]).astype(hn_ref.dtype)

        items.append(functools.partial(stats, "h"))
        items += [functools.partial(scale, *p) for p in pieces]
        if hn_ref is not None:
            items.append(functools.partial(stats, "hn"))
            items += [functools.partial(scale_next, *p) for p in pieces]
    return items


def _alternate(leaders, followers):
    n = len(leaders)
    for i, lead in enumerate(leaders):
        lead()
        for f in followers[i * len(followers) // n:(i + 1) * len(followers) // n]:
            f()


def _project_and_normalise(dot_into, res_ref, g_ref, g2_ref, h_ref, hn_ref):
    tm = h_ref.shape[0]
    half = tm // 2 if tm % (2 * NORM_ROWS) == 0 else tm
    cols = range(0, D_MODEL, PROJ_CHUNK)
    norm = functools.partial(_norm_residual_items, res_ref, g_ref, g2_ref, h_ref, hn_ref)
    for c in cols:
        dot_into(slice(0, half), c)
    if half < tm:
        _alternate([functools.partial(dot_into, slice(half, tm), c) for c in cols], norm(0, half))
    for item in norm(half if half < tm else 0, tm):
        item()


def _proj_residual_kernel(nk, a_ref, w_ref, res_ref, g_ref, g2_ref, h_ref, hn_ref):
    kk = pl.program_id(1)

    def dot_into(first, rows, c):
        part = jnp.dot(a_ref[rows, :], w_ref[:, c:c + PROJ_CHUNK], preferred_element_type=F32)
        if first:
            h_ref[rows, c:c + PROJ_CHUNK] = part
        else:
            h_ref[rows, c:c + PROJ_CHUNK] += part

    def accumulate(first):
        for c in range(0, D_MODEL, PROJ_CHUNK):
            dot_into(first, slice(None), c)

    def finish(first):
        _project_and_normalise(functools.partial(dot_into, first), res_ref, g_ref, g2_ref, h_ref, hn_ref)

    if nk == 1:
        finish(True)
    else:
        pl.when(kk == 0)(functools.partial(accumulate, True))
        pl.when(jnp.logical_and(kk > 0, kk < nk - 1))(functools.partial(accumulate, False))
        pl.when(kk == nk - 1)(functools.partial(finish, False))


def _proj_residual_nohn_kernel(nk, a_ref, w_ref, res_ref, g_ref, h_ref):
    _proj_residual_kernel(nk, a_ref, w_ref, res_ref, g_ref, None, h_ref, None)


def _proj_residual(a, w, res, g, g2, tm, tk, name):
    m, k = a.shape
    nk = k // tk
    gspec = pl.BlockSpec((1, D_MODEL), lambda i, kk: (0, 0))
    row_spec = pl.BlockSpec((tm, D_MODEL), lambda i, kk: (i, 0))
    in_specs = [pl.BlockSpec((tm, tk), lambda i, kk: (i, kk)),
                pl.BlockSpec((tk, D_MODEL), lambda i, kk: (kk, 0)), row_spec, gspec]
    if g2 is None:
        return pl.pallas_call(
            functools.partial(_proj_residual_nohn_kernel, nk),
            grid=(m // tm, nk), in_specs=in_specs, out_specs=row_spec,
            out_shape=jax.ShapeDtypeStruct((m, D_MODEL), F32),
            compiler_params=_params(("parallel", "arbitrary")), name=name,
        )(a, w, res, g)
    return pl.pallas_call(
        functools.partial(_proj_residual_kernel, nk),
        grid=(m // tm, nk), in_specs=in_specs + [gspec], out_specs=[row_spec, row_spec],
        out_shape=[jax.ShapeDtypeStruct((m, D_MODEL), F32), jax.ShapeDtypeStruct((m, D_MODEL), BF16)],
        compiler_params=_params(("parallel", "arbitrary")), name=name,
    )(a, w, res, g, g2)


def _proj_resident_kernel(a_ref, w_ref, res_ref, g_ref, g2_ref, h_ref, hn_ref):
    def dot_into(rows, c):
        h_ref[rows, c:c + PROJ_CHUNK] = jnp.dot(a_ref[rows, :], w_ref[:, c:c + PROJ_CHUNK],
                                                preferred_element_type=F32)

    _project_and_normalise(dot_into, res_ref, g_ref, g2_ref, h_ref, hn_ref)


def _proj_resident(a, w, res, g, g2, tm, name):
    m, k = a.shape
    gspec = pl.BlockSpec((1, D_MODEL), lambda i: (0, 0))
    row_spec = pl.BlockSpec((tm, D_MODEL), lambda i: (i, 0))
    return pl.pallas_call(
        _proj_resident_kernel,
        grid=(m // tm,),
        in_specs=[pl.BlockSpec((tm, k), lambda i: (i, 0)),
                  pl.BlockSpec((k, D_MODEL), lambda i: (0, 0), pipeline_mode=pl.Buffered(1)),
                  row_spec, gspec, gspec],
        out_specs=[row_spec, row_spec],
        out_shape=[jax.ShapeDtypeStruct((m, D_MODEL), F32), jax.ShapeDtypeStruct((m, D_MODEL), BF16)],
        compiler_params=_params(("parallel",)), name=name,
    )(a, w, res, g, g2)


HALO = N_META
FFN_ROW_BLOCKS = 4


GELU_K1 = -2.0 * math.sqrt(2.0 / math.pi) * LOG2E
GELU_K2 = GELU_K1 * 0.044715


def _gelu_tanh(x):
    return x / (1.0 + jnp.exp2(x * (x * x * GELU_K2 + GELU_K1)))


def _ffn_up_kernel(tm, tiles_per_seq, hn_ref, prev_ref, next_ref, hnm_ref, wa_ref, wg_ref, wc_ref, bc_ref,
                   f_ref, ext_ref):
    i = pl.program_id(0)

    @pl.when(pl.program_id(1) == 0)
    def _():
        first = i % tiles_per_seq == 0
        last = i % tiles_per_seq == tiles_per_seq - 1
        ext_ref[0:HALO, :] = jnp.where(first, hnm_ref[...], prev_ref[...])
        ext_ref[HALO:HALO + tm, :] = hn_ref[...]
        ext_ref[HALO + tm:, :] = jnp.where(last, jnp.zeros_like(next_ref), next_ref[...])

    wc = wc_ref[...]
    bc = bc_ref[...]
    n_chunks = f_ref.shape[1] // MXU_N
    rows_g = tm // FFN_ROW_BLOCKS
    rows_e = tm // (2 * FFN_ROW_BLOCKS)
    pad = 8

    def a_block(c, r):
        lo = 0 if r == 0 else HALO + r * rows_g
        hi = tm + 2 * HALO if r == FFN_ROW_BLOCKS - 1 else HALO + (r + 1) * rows_g
        return jnp.dot(ext_ref[lo:hi, :], wa_ref[:, c * MXU_N:(c + 1) * MXU_N], preferred_element_type=F32)

    def g_block(c, r):
        return jnp.dot(ext_ref[HALO + r * rows_g:HALO + (r + 1) * rows_g, :],
                       wg_ref[:, c * MXU_N:(c + 1) * MXU_N], preferred_element_type=F32)

    def epilogue_piece(c, a, gate, k):
        cols = slice(c * MXU_N, (c + 1) * MXU_N)
        lo = HALO + k * rows_e
        win = a[lo - pad:lo + rows_e + pad]
        prev = pltpu.roll(win, 1, 0)[pad:pad + rows_e]
        nxt = pltpu.roll(win, win.shape[0] - 1, 0)[pad:pad + rows_e]
        conv = prev * wc[0:1, cols] + win[pad:pad + rows_e] * wc[1:2, cols] + nxt * wc[2:3, cols] + bc[:, cols]
        g = gate[k * rows_e:(k + 1) * rows_e]
        f_ref[k * rows_e:(k + 1) * rows_e, cols] = (_gelu_tanh(conv) * g).astype(f_ref.dtype)

    a_next = jnp.concatenate([a_block(0, r) for r in range(FFN_ROW_BLOCKS)], axis=0)
    g_next = jnp.concatenate([g_block(0, r) for r in range(FFN_ROW_BLOCKS)], axis=0)
    for c in range(n_chunks):
        a_cur, g_cur = a_next, g_next
        a_parts, g_parts = [], []
        for k in range(2 * FFN_ROW_BLOCKS):
            if c + 1 < n_chunks:
                if k < FFN_ROW_BLOCKS:
                    a_parts.append(a_block(c + 1, k))
                else:
                    g_parts.append(g_block(c + 1, k - FFN_ROW_BLOCKS))
            epilogue_piece(c, a_cur, g_cur, k)
        if c + 1 < n_chunks:
            a_next = jnp.concatenate(a_parts, axis=0)
            g_next = jnp.concatenate(g_parts, axis=0)


def _ffn_up(hn, hn_meta, wa, wg, wc, bc, tm, tf, name):
    m = hn.shape[0]
    tiles_per_seq = SEQ // tm
    hb = tm // HALO
    last = m // HALO - 1
    return pl.pallas_call(
        functools.partial(_ffn_up_kernel, tm, tiles_per_seq),
        grid=(m // tm, D_FF_PAD // tf),
        in_specs=[
            pl.BlockSpec((tm, D_MODEL), lambda i, j: (i, 0)),
            pl.BlockSpec((HALO, D_MODEL), lambda i, j: (jnp.maximum(i * hb - 1, 0), 0)),
            pl.BlockSpec((HALO, D_MODEL), lambda i, j: (jnp.minimum((i + 1) * hb, last), 0)),
            pl.BlockSpec((N_META, D_MODEL), lambda i, j: (i // tiles_per_seq, 0)),
            pl.BlockSpec((D_MODEL, tf), lambda i, j: (0, j)),
            pl.BlockSpec((D_MODEL, tf), lambda i, j: (0, j)),
            pl.BlockSpec((3, tf), lambda i, j: (0, j)),
            pl.BlockSpec((1, tf), lambda i, j: (0, j)),
        ],
        out_specs=pl.BlockSpec((tm, tf), lambda i, j: (i, j)),
        out_shape=jax.ShapeDtypeStruct((m, D_FF_PAD), BF16),
        scratch_shapes=[pltpu.VMEM((tm + 2 * HALO, D_MODEL), BF16)],
        compiler_params=_params(("parallel", "arbitrary")),
        name=name,
    )(hn, hn, hn, hn_meta, wa, wg, wc, bc)


SRC_CQ, SRC_CKV, SRC_KR = 0, MLA_Q_LORA, MLA_Q_LORA + MLA_KV_LORA
SRC_GQ = SRC_KR + MLA_ROPE
SRC_GK = SRC_GQ + GQA_HEADS * HEAD_DIM
SRC_GV = SRC_GK + GQA_KV_HEADS * HEAD_DIM
SRC_GA = SRC_GV + GQA_KV_HEADS * HEAD_DIM
SRC_GB = SRC_GA + D_MODEL
D_IN = SRC_GB + D_MODEL
TR_PREP = 256


def _w_in_layout_kernel(x_ref, o_ref):
    lane = lax.broadcasted_iota(jnp.int32, (x_ref.shape[0], LANES), 1)
    qtr = LANES // 4

    def src(tile):
        return x_ref[:, tile * LANES:(tile + 1) * LANES]

    def put(off, val):
        o_ref[:, off:off + LANES] = val.astype(o_ref.dtype)

    def half_off(src_off):
        assert src_off % LANES == HALF_LANES
        return src_off // LANES

    def shifted(t):
        return jnp.where(lane < HALF_LANES, pltpu.roll(src(t), HALF_LANES, 1), pltpu.roll(src(t + 1), HALF_LANES, 1))

    def head_b(t):
        a, b = src(t), src(t + 1)
        return jnp.where(lane < qtr, pltpu.roll(a, HALF_LANES, 1),
                         jnp.where(lane < 2 * qtr, pltpu.roll(b, qtr, 1),
                                   jnp.where(lane < 3 * qtr, pltpu.roll(a, 3 * qtr, 1),
                                             pltpu.roll(b, HALF_LANES, 1))))

    for dst, s, width in ((OFF_GA, SRC_GA, D_MODEL), (OFF_GB, SRC_GB, D_MODEL),
                          (OFF_GV, SRC_GV, GQA_KV_HEADS * HEAD_DIM)):
        for j in range(width // LANES):
            put(dst + j * LANES, shifted(half_off(s) + j))
    for dst, s, heads in ((OFF_GQ, SRC_GQ, GQA_HEADS), (OFF_GK, SRC_GK, GQA_KV_HEADS)):
        for h in range(heads):
            put(dst + h * LANES, head_b(half_off(s) + h))
    for dst, s, width in ((OFF_CQ, SRC_CQ, MLA_Q_LORA), (OFF_CKV, SRC_CKV, MLA_KV_LORA)):
        for j in range(width // LANES):
            put(dst + j * LANES, src(s // LANES + j))
    kr = src(SRC_KR // LANES)
    put(OFF_KR, jnp.where(lane < qtr, kr,
                          jnp.where(jnp.logical_and(lane >= 2 * qtr, lane < 3 * qtr), pltpu.roll(kr, qtr, 1), 0.0)))
    for off in range(OFF_KR + LANES, D_IN_PAD, LANES):
        put(off, jnp.zeros(lane.shape, F32))


def _w_in_layout(w):
    d_in_tiles = pl.cdiv(D_IN, LANES) * LANES
    return pl.pallas_call(
        _w_in_layout_kernel,
        grid=(D_MODEL // TR_PREP,),
        in_specs=[pl.BlockSpec((TR_PREP, d_in_tiles), lambda i: (i, 0))],
        out_specs=pl.BlockSpec((TR_PREP, D_IN_PAD), lambda i: (i, 0)),
        out_shape=jax.ShapeDtypeStruct((D_MODEL, D_IN_PAD), BF16),
        compiler_params=_params(("parallel",)),
        name="w_in_layout",
    )(w)


def _prep_weights(g_pre_mix, w_in, g_cq, w_uq, g_ckv, w_ukv, g_qn, g_kn, w_pa, w_pb, w_o,
                  g_post_mix, g_pre_ffn, w_up, w_conv, b_conv, w_down, g_post_ffn):
    qtr = HEAD_DIM // 4
    perm_b = np.arange(HEAD_DIM).reshape(2, 2, qtr).transpose(1, 0, 2).reshape(-1)
    half = MLA_ROPE // 2
    w_in_p = _w_in_layout(w_in[0])

    uq = w_uq[0].reshape(MLA_Q_LORA, MLA_HEADS, MLA_NOPE + MLA_ROPE)
    zq = jnp.zeros((MLA_Q_LORA, MLA_HEADS, HALF_LANES - half), uq.dtype)
    uq = jnp.concatenate([uq[:, :, :MLA_NOPE], uq[:, :, MLA_NOPE:MLA_NOPE + half], zq,
                          uq[:, :, MLA_NOPE + half:], zq], axis=2).reshape(MLA_Q_LORA, -1).astype(BF16)
    ukv = w_ukv[0].reshape(MLA_KV_LORA, MLA_HEADS, MLA_NOPE + MLA_V)
    uk = ukv[:, :, :MLA_NOPE].reshape(MLA_KV_LORA, -1).astype(BF16)
    uv = ukv[:, :, MLA_NOPE:].reshape(MLA_KV_LORA, -1).astype(BF16)

    fpad = D_FF_PAD - D_FF
    up = w_up[0]
    return dict(
        g_pre_mix=g_pre_mix, w_in=w_in_p, g_cq=g_cq, w_uq=uq, g_ckv=g_ckv, w_uk=uk, w_uvt=uv.T,
        eye_kv=jnp.eye(GQA_KV_HEADS * HEAD_DIM, dtype=BF16),
        g_qn=g_qn[:, perm_b], g_kn=g_kn[:, perm_b],
        w_pa=w_pa[0].astype(BF16), w_pb=w_pb[0].astype(BF16), w_o=w_o[0].astype(BF16),
        g_post_mix=g_post_mix, g_pre_ffn=g_pre_ffn,
        w_up_a=jnp.pad(up[:, :D_FF], ((0, 0), (0, fpad))).astype(BF16),
        w_up_g=jnp.pad(up[:, D_FF:], ((0, 0), (0, fpad))).astype(BF16),
        w_conv=jnp.pad(w_conv[0], ((0, 0), (0, fpad))), b_conv=jnp.pad(b_conv, ((0, 0), (0, fpad))),
        w_down=jnp.pad(w_down[0], ((0, fpad), (0, 0))).astype(BF16), g_post_ffn=g_post_ffn,
    )


def _rope_tables(pos, dim):
    inv = ROPE_THETA ** (-jnp.arange(0, dim, 2, dtype=F32) / dim)
    ang = pos.astype(F32)[:, None] * inv[None, :]
    return jnp.cos(ang), jnp.sin(ang)


def _tables(pos, row, col):
    c1, s1 = _rope_tables(pos, MLA_ROPE)
    z = jnp.zeros((c1.shape[0], HALF_LANES - c1.shape[1]), F32)
    cos_a = jnp.concatenate([c1, z, c1, z], axis=1)
    sin_a = jnp.concatenate([-s1, z, s1, z], axis=1)
    cr, sr = _rope_tables(row, HEAD_DIM // 2)
    cc, sc = _rope_tables(col, HEAD_DIM // 2)
    cos_b = jnp.concatenate([cr, cc, cr, cc], axis=1)
    sin_b = jnp.concatenate([-sr, -sc, sr, sc], axis=1)
    return cos_a, sin_a, cos_b, sin_b


def kernel(x_prompt, x_sample, meta_tokens, g_pre_mix, w_in, g_cq, w_uq, g_ckv, w_ukv, g_qn, g_kn,
           w_pa, w_pb, w_o, g_post_mix, g_pre_ffn, w_up, w_conv, b_conv, w_down, g_post_ffn):
    w = _prep_weights(g_pre_mix, w_in, g_cq, w_uq, g_ckv, w_ukv, g_qn, g_kn, w_pa, w_pb, w_o,
                      g_post_mix, g_pre_ffn, w_up, w_conv, b_conv, w_down, g_post_ffn)
    t = jnp.arange(SEQ, dtype=jnp.int32)
    tabs_real = _tables(t + N_META, t // GRID_W, t % GRID_W)
    zero = jnp.zeros((N_META,), jnp.int32)
    tabs_meta = _tables(jnp.arange(N_META), zero, zero)
    return (_trunk(x_prompt, meta_tokens, w, tabs_real, tabs_meta, "p"),
            _trunk(x_sample, meta_tokens, w, tabs_real, tabs_meta, "s"))


def _trunk(x, meta_tokens, w, tabs_real, tabs_meta, tag):
    batch = x.shape[0]
    xr = x.reshape(batch * SEQ, D_MODEL)
    xm = jnp.broadcast_to(meta_tokens[None], (batch, N_META, D_MODEL)).reshape(batch * N_META, D_MODEL)
    mrows = batch * N_META
    tabs_meta = tuple(jnp.tile(a, (batch, 1)) for a in tabs_meta)

    z = _matmul(_norm_rows(xr, w["g_pre_mix"], TM_NORM, "norm_" + tag), w["w_in"], TM_MM, TN_MM, BF16,
                "in_proj_" + tag)
    zm = _matmul(_norm_rows(xm, w["g_pre_mix"], mrows, "norm_meta_" + tag), w["w_in"], mrows, TN_MM, BF16,
                 "in_proj_meta_" + tag)
    qa, ka, vat, qb, kb, vbt = _heads(z, w, tabs_real, TM_HEADS, SEQ // TM_HEADS, "heads_" + tag)
    qam, kam, vatm, qbm, kbm, vbtm = _heads(zm, w, tabs_meta, mrows, 1, "heads_meta_" + tag)

    def per_batch(t):
        return t.reshape(t.shape[0], batch, N_META).transpose(1, 0, 2)

    def token_major(t):
        return t.transpose(0, 2, 1).reshape(batch * N_META, t.shape[1])

    vatm, vbtm = per_batch(vatm), per_batch(vbtm)

    cfg_a = dict(batch=batch, heads=MLA_HEADS, group=1, dk=MLA_QK)
    cfg_b = dict(batch=batch, heads=GQA_HEADS, group=GQA_GROUP, dk=HEAD_DIM)
    oa = _attention(qa, ka, vat, kam, vatm, lq=SEQ, tq=TQ_ATTN, tk=TK_ATTN, name="attn_a_" + tag, **cfg_a)
    ob = _attention(qb, kb, vbt, kbm, vbtm, lq=SEQ, tq=TQ_ATTN, tk=TK_ATTN, name="attn_b_" + tag, **cfg_b)
    oam = token_major(_attention(qam, ka, vat, kam, vatm, lq=N_META, tq=N_META, tk=TK_ATTN,
                                 name="attn_a_meta_" + tag, **cfg_a))
    obm = token_major(_attention(qbm, kb, vbt, kbm, vbtm, lq=N_META, tq=N_META, tk=TK_ATTN,
                                 name="attn_b_meta_" + tag, **cfg_b))

    mg = _merge(oa, ob, z, w["w_pa"], w["w_pb"], TM_MM, TN_MM, "merge_" + tag)
    mgm = _merge(oam, obm, zm, w["w_pa"], w["w_pb"], mrows, TN_MM, "merge_meta_" + tag)
    h1, hn = _proj_resident(mg, w["w_o"], xr, w["g_post_mix"], w["g_pre_ffn"], TM_OUT, "out_proj_" + tag)
    _, hnm = _proj_residual(mgm, w["w_o"], xm, w["g_post_mix"], w["g_pre_ffn"], mrows, TK_META,
                            "out_proj_meta_" + tag)

    f = _ffn_up(hn, hnm, w["w_up_a"], w["w_up_g"], w["w_conv"], w["b_conv"], TM_UP, TF_UP, "ffn_up_" + tag)
    y = _proj_residual(f, w["w_down"], h1, w["g_post_ffn"], None, TM_DOWN, TK_DOWN, "ffn_down_" + tag)
    return y.reshape(batch, SEQ, D_MODEL)
```

```python
import functools
import math

import jax
import jax.numpy as jnp
import numpy as np
from jax import lax
from jax.experimental import pallas as pl
from jax.experimental.pallas import tpu as pltpu

F32 = jnp.float32
BF16 = jnp.bfloat16

D_MODEL = 4096
SEQ = 4096
N_META = 16
GRID_W = 64
HEAD_DIM = 128
MLA_HEADS = 16
MLA_Q_LORA = 1024
MLA_KV_LORA = 512
MLA_NOPE = 128
MLA_ROPE = 64
MLA_V = 128
MLA_QK = 256
GQA_HEADS = 16
GQA_KV_HEADS = 4
GQA_GROUP = GQA_HEADS // GQA_KV_HEADS
D_FF = 11008
D_FF_PAD = 11264
ROPE_THETA = 10000.0
EPS = 1e-6
LOG2E = math.log2(math.e)

OFF_GA = 0
OFF_GB = 4096
OFF_GQ = 8192
OFF_CQ = 10240
OFF_CKV = 11264
OFF_GK = 11776
OFF_GV = 12288
OFF_KR = 12800
D_IN_PAD = 13312

LANES = 128
HALF_LANES = LANES // 2
MXU_N = 256
VMEM_LIMIT = 60 * 1024 * 1024

TM_NORM = 256
TM_MM, TN_MM = 1024, 1024
TM_HEADS = 256
TQ_ATTN, TK_ATTN = 1024, 2048
TM_OUT = 256
TM_UP, TF_UP = 1024, 512
TM_DOWN, TK_DOWN = 512, 1024
TK_META = 512


def _params(sem):
    return pltpu.CompilerParams(dimension_semantics=sem, vmem_limit_bytes=VMEM_LIMIT)


def _rms(x, g):
    return x * lax.rsqrt(jnp.mean(x * x, axis=-1, keepdims=True) + EPS) * g


def _norm_kernel(x_ref, g_ref, o_ref):
    o_ref[...] = _rms(x_ref[...], g_ref[...]).astype(o_ref.dtype)


def _norm_rows(x, g, tm, name):
    m = x.shape[0]
    return pl.pallas_call(
        _norm_kernel,
        grid=(m // tm,),
        in_specs=[pl.BlockSpec((tm, D_MODEL), lambda i: (i, 0)), pl.BlockSpec((1, D_MODEL), lambda i: (0, 0))],
        out_specs=pl.BlockSpec((tm, D_MODEL), lambda i: (i, 0)),
        out_shape=jax.ShapeDtypeStruct((m, D_MODEL), BF16),
        compiler_params=_params(("parallel",)),
        name=name,
    )(x, g)


NT_DIMS = (((1,), (1,)), ((), ()))


def _rope(x, cos, sin_signed):
    return x * cos + pltpu.roll(x, HALF_LANES, 1) * sin_signed


def _heads_kernel(cq_ref, ckv_ref, kr_ref, gq_ref, gk_ref, gv_ref, wuq_ref, wuk_ref, wuvt_ref, eye_ref,
                  gcq_ref, gckv_ref, gqn_ref, gkn_ref, cosa_ref, sina_ref, cosb_ref, sinb_ref,
                  qa_ref, ka_ref, vat_ref, qb_ref, kb_ref, vbt_ref):
    cosa, sina = cosa_ref[...], sina_ref[...]
    cosb, sinb = cosb_ref[...], sinb_ref[...]
    scale_a = LOG2E / math.sqrt(MLA_NOPE + MLA_ROPE)
    scale_b = LOG2E / math.sqrt(HEAD_DIM)

    cq = _rms(cq_ref[...].astype(F32), gcq_ref[...]).astype(BF16)
    for h in range(MLA_HEADS):
        q = jnp.dot(cq, wuq_ref[:, h * MLA_QK:(h + 1) * MLA_QK], preferred_element_type=F32)
        qa_ref[:, h * MLA_QK:h * MLA_QK + MLA_NOPE] = (q[:, :MLA_NOPE] * scale_a).astype(BF16)
        qa_ref[:, h * MLA_QK + MLA_NOPE:(h + 1) * MLA_QK] = (
            _rope(q[:, MLA_NOPE:], cosa, sina) * scale_a).astype(BF16)

    ckv = _rms(ckv_ref[...].astype(F32), gckv_ref[...]).astype(BF16)
    k_rope = _rope(kr_ref[...].astype(F32), cosa, sina).astype(BF16)
    k_nope = jnp.dot(ckv, wuk_ref[...], preferred_element_type=F32).astype(BF16)
    for h in range(MLA_HEADS):
        ka_ref[:, h * MLA_QK:h * MLA_QK + MLA_NOPE] = k_nope[:, h * MLA_NOPE:(h + 1) * MLA_NOPE]
        ka_ref[:, h * MLA_QK + MLA_NOPE:(h + 1) * MLA_QK] = k_rope
    vat_ref[...] = lax.dot_general(wuvt_ref[...], ckv, NT_DIMS, preferred_element_type=F32).astype(BF16)
    vbt_ref[...] = lax.dot_general(eye_ref[...], gv_ref[...], NT_DIMS, preferred_element_type=F32).astype(BF16)

    gqn, gkn = gqn_ref[...], gkn_ref[...]
    for h in range(GQA_HEADS):
        x = gq_ref[:, h * HEAD_DIM:(h + 1) * HEAD_DIM].astype(F32)
        qb_ref[:, h * HEAD_DIM:(h + 1) * HEAD_DIM] = (_rope(_rms(x, gqn), cosb, sinb) * scale_b).astype(BF16)
    for h in range(GQA_KV_HEADS):
        x = gk_ref[:, h * HEAD_DIM:(h + 1) * HEAD_DIM].astype(F32)
        kb_ref[:, h * HEAD_DIM:(h + 1) * HEAD_DIM] = _rope(_rms(x, gkn), cosb, sinb).astype(BF16)


def _heads(z, w, tabs, tm, tab_blocks, name):
    m = z.shape[0]

    def zspec(width, off):
        return pl.BlockSpec((tm, width), lambda i: (i, off // width))

    def full(a):
        return pl.BlockSpec(a.shape, lambda i: (0,) * a.ndim)

    tab_spec = pl.BlockSpec((tm, LANES), lambda i: (i % tab_blocks, 0))
    outs = ((MLA_HEADS * MLA_QK, True), (MLA_HEADS * MLA_QK, True), (MLA_HEADS * MLA_V, False),
            (GQA_HEADS * HEAD_DIM, True), (GQA_KV_HEADS * HEAD_DIM, True), (GQA_KV_HEADS * HEAD_DIM, False))
    consts = (w["w_uq"], w["w_uk"], w["w_uvt"], w["eye_kv"], w["g_cq"], w["g_ckv"], w["g_qn"], w["g_kn"])
    return pl.pallas_call(
        _heads_kernel,
        grid=(m // tm,),
        in_specs=[zspec(MLA_Q_LORA, OFF_CQ), zspec(MLA_KV_LORA, OFF_CKV), zspec(LANES, OFF_KR),
                  zspec(GQA_HEADS * HEAD_DIM, OFF_GQ), zspec(GQA_KV_HEADS * HEAD_DIM, OFF_GK),
                  zspec(GQA_KV_HEADS * HEAD_DIM, OFF_GV)]
        + [full(a) for a in consts] + [tab_spec] * 4,
        out_specs=[pl.BlockSpec((tm, n), lambda i: (i, 0)) if rows else pl.BlockSpec((n, tm), lambda i: (0, i))
                   for n, rows in outs],
        out_shape=[jax.ShapeDtypeStruct((m, n) if rows else (n, m), BF16) for n, rows in outs],
        compiler_params=_params(("parallel",)),
        name=name,
    )(z, z, z, z, z, z, *consts, *tabs)


def _attn_kernel(tk, transpose_out, q_ref, k_ref, vt_ref, km_ref, vmt_ref, o_ref):
    q = q_ref[...]
    s = lax.dot_general(km_ref[...], q, NT_DIMS, preferred_element_type=F32)
    m = jnp.max(s, axis=0, keepdims=True)
    p = jnp.exp2(s - m)
    l = jnp.sum(p, axis=0, keepdims=True)
    acc = jnp.dot(vmt_ref[...], p.astype(BF16), preferred_element_type=F32)

    def scores(c):
        return lax.dot_general(k_ref[c * tk:(c + 1) * tk, :], q, NT_DIMS, preferred_element_type=F32)

    n_chunks = k_ref.shape[0] // tk
    s_next = scores(0)
    for c in range(n_chunks):
        s = s_next
        if c + 1 < n_chunks:
            s_next = scores(c + 1)
        m_new = jnp.maximum(m, jnp.max(s, axis=0, keepdims=True))
        alpha = jnp.exp2(m - m_new)
        p = jnp.exp2(s - m_new)
        l = alpha * l + jnp.sum(p, axis=0, keepdims=True)
        acc = alpha * acc + jnp.dot(vt_ref[:, c * tk:(c + 1) * tk], p.astype(BF16), preferred_element_type=F32)
        m = m_new
    o = acc / l
    o_ref[...] = (o.T if transpose_out else o).astype(o_ref.dtype)


def _attention(q, k, vt, km, vmt, *, batch, heads, group, dk, lq, tq, tk, name):
    nq = lq // tq
    dv = HEAD_DIM
    transpose_out = tq % LANES == 0
    if transpose_out:
        out_spec = pl.BlockSpec((tq, dv), lambda b, h, i: (b * nq + i, h))
        out_shape = jax.ShapeDtypeStruct((batch * lq, heads * dv), BF16)
    else:
        out_spec = pl.BlockSpec((None, dv, tq), lambda b, h, i: (b, h, i))
        out_shape = jax.ShapeDtypeStruct((batch, heads * dv, lq), BF16)
    return pl.pallas_call(
        functools.partial(_attn_kernel, tk, transpose_out),
        grid=(batch, heads, nq),
        in_specs=[
            pl.BlockSpec((tq, dk), lambda b, h, i: (b * nq + i, h)),
            pl.BlockSpec((SEQ, dk), lambda b, h, i: (b, h // group)),
            pl.BlockSpec((dv, SEQ), lambda b, h, i: (h // group, b)),
            pl.BlockSpec((N_META, dk), lambda b, h, i: (b, h // group)),
            pl.BlockSpec((None, dv, N_META), lambda b, h, i: (b, h // group, 0)),
        ],
        out_specs=out_spec,
        out_shape=out_shape,
        compiler_params=_params(("parallel", "parallel", "parallel")),
        name=name,
    )(q, k, vt, km, vmt)


def _merge_kernel(oa_ref, ob_ref, ga_ref, gb_ref, wpa_ref, wpb_ref, o_ref):
    a = jnp.dot(oa_ref[...], wpa_ref[...], preferred_element_type=F32)
    b = jnp.dot(ob_ref[...], wpb_ref[...], preferred_element_type=F32)
    ga = jax.nn.sigmoid(ga_ref[...].astype(F32))
    gb = jax.nn.sigmoid(gb_ref[...].astype(F32))
    o_ref[...] = (ga * a + gb * b).astype(o_ref.dtype)


def _merge(oa, ob, z, wpa, wpb, tm, tn, name):
    m, k = oa.shape
    return pl.pallas_call(
        _merge_kernel,
        grid=(m // tm, D_MODEL // tn),
        in_specs=[
            pl.BlockSpec((tm, k), lambda i, j: (i, 0)),
            pl.BlockSpec((tm, k), lambda i, j: (i, 0)),
            pl.BlockSpec((tm, tn), lambda i, j: (i, OFF_GA // tn + j)),
            pl.BlockSpec((tm, tn), lambda i, j: (i, OFF_GB // tn + j)),
            pl.BlockSpec((k, tn), lambda i, j: (0, j)),
            pl.BlockSpec((k, tn), lambda i, j: (0, j)),
        ],
        out_specs=pl.BlockSpec((tm, tn), lambda i, j: (i, j)),
        out_shape=jax.ShapeDtypeStruct((m, D_MODEL), BF16),
        compiler_params=_params(("parallel", "parallel")),
        name=name,
    )(oa, ob, z, z, wpa, wpb)


PROJ_CHUNK = 512
NORM_ROWS = 128
NORM_PIECE = 16


def _inv_rms(x):
    return lax.rsqrt(jnp.mean(x * x, axis=-1, keepdims=True) + EPS)


def _norm_residual_items(res_ref, g_ref, g2_ref, h_ref, hn_ref, row_lo, row_hi):
    items = []
    for r0 in range(row_lo, row_hi, NORM_ROWS):
        blk = min(NORM_ROWS, row_hi - r0)
        pieces = [(r0 + s0, min(NORM_PIECE, blk - s0), s0) for s0 in range(0, blk, NORM_PIECE)]
        inv = {}

        def stats(key, r0=r0, blk=blk, inv=inv):
            inv[key] = _inv_rms(h_ref[r0:r0 + blk, :])

        def scale(lo, sz, s0, inv=inv):
            r = inv["h"][s0:s0 + sz]
            h_ref[lo:lo + sz, :] = res_ref[lo:lo + sz, :] + h_ref[lo:lo + sz, :] * r * g_ref[...]

        def scale_next(lo, sz, s0, inv=inv):
            r = inv["hn"][s0:s0 + sz]
            hn_ref[lo:lo + sz, :] = (h_ref[lo:lo + sz, :] * r * g2_ref[...]).astype(hn_ref.dtype)

        items.append(functools.partial(stats, "h"))
        items += [functools.partial(scale, *p) for p in pieces]
        if hn_ref is not None:
            items.append(functools.partial(stats, "hn"))
            items += [functools.partial(scale_next, *p) for p in pieces]
    return items


def _alternate(leaders, followers):
    n = len(leaders)
    for i, lead in enumerate(leaders):
        lead()
        for f in followers[i * len(followers) // n:(i + 1) * len(followers) // n]:
            f()


def _project_and_normalise(dot_into, res_ref, g_ref, g2_ref, h_ref, hn_ref):
    tm = h_ref.shape[0]
    half = tm // 2 if tm % (2 * NORM_ROWS) == 0 else tm
    cols = range(0, D_MODEL, PROJ_CHUNK)
    norm = functools.partial(_norm_residual_items, res_ref, g_ref, g2_ref, h_ref, hn_ref)
    for c in cols:
        dot_into(slice(0, half), c)
    if half < tm:
        _alternate([functools.partial(dot_into, slice(half, tm), c) for c in cols], norm(0, half))
    for item in norm(half if half < tm else 0, tm):
        item()


def _proj_residual_kernel(nk, a_ref, w_ref, res_ref, g_ref, g2_ref, h_ref, hn_ref):
    kk = pl.program_id(1)

    def dot_into(first, rows, c):
        part = jnp.dot(a_ref[rows, :], w_ref[:, c:c + PROJ_CHUNK], preferred_element_type=F32)
        if first:
            h_ref[rows, c:c + PROJ_CHUNK] = part
        else:
            h_ref[rows, c:c + PROJ_CHUNK] += part

    def accumulate(first):
        for c in range(0, D_MODEL, PROJ_CHUNK):
            dot_into(first, slice(None), c)

    def finish(first):
        _project_and_normalise(functools.partial(dot_into, first), res_ref, g_ref, g2_ref, h_ref, hn_ref)

    if nk == 1:
        finish(True)
    else:
        pl.when(kk == 0)(functools.partial(accumulate, True))
        pl.when(jnp.logical_and(kk > 0, kk < nk - 1))(functools.partial(accumulate, False))
        pl.when(kk == nk - 1)(functools.partial(finish, False))


def _proj_residual_nohn_kernel(nk, a_ref, w_ref, res_ref, g_ref, h_ref):
    _proj_residual_kernel(nk, a_ref, w_ref, res_ref, g_ref, None, h_ref, None)


def _proj_residual(a, w, res, g, g2, tm, tk, name):
    m, k = a.shape
    nk = k // tk
    gspec = pl.BlockSpec((1, D_MODEL), lambda i, kk: (0, 0))
    row_spec = pl.BlockSpec((tm, D_MODEL), lambda i, kk: (i, 0))
    in_specs = [pl.BlockSpec((tm, tk), lambda i, kk: (i, kk)),
                pl.BlockSpec((tk, D_MODEL), lambda i, kk: (kk, 0)), row_spec, gspec]
    if g2 is None:
        return pl.pallas_call(
            functools.partial(_proj_residual_nohn_kernel, nk),
            grid=(m // tm, nk), in_specs=in_specs, out_specs=row_spec,
            out_shape=jax.ShapeDtypeStruct((m, D_MODEL), F32),
            compiler_params=_params(("parallel", "arbitrary")), name=name,
        )(a, w, res, g)
    return pl.pallas_call(
        functools.partial(_proj_residual_kernel, nk),
        grid=(m // tm, nk), in_specs=in_specs + [gspec], out_specs=[row_spec, row_spec],
        out_shape=[jax.ShapeDtypeStruct((m, D_MODEL), F32), jax.ShapeDtypeStruct((m, D_MODEL), BF16)],
        compiler_params=_params(("parallel", "arbitrary")), name=name,
    )(a, w, res, g, g2)


def _proj_resident_kernel(a_ref, w_ref, res_ref, g_ref, g2_ref, h_ref, hn_ref):
    def dot_into(rows, c):
        h_ref[rows, c:c + PROJ_CHUNK] = jnp.dot(a_ref[rows, :], w_ref[:, c:c + PROJ_CHUNK],
                                                preferred_element_type=F32)

    _project_and_normalise(dot_into, res_ref, g_ref, g2_ref, h_ref, hn_ref)


def _proj_resident(a, w, res, g, g2, tm, name):
    m, k = a.shape
    gspec = pl.BlockSpec((1, D_MODEL), lambda i: (0, 0))
    row_spec = pl.BlockSpec((tm, D_MODEL), lambda i: (i, 0))
    return pl.pallas_call(
        _proj_resident_kernel,
        grid=(m // tm,),
        in_specs=[pl.BlockSpec((tm, k), lambda i: (i, 0)),
                  pl.BlockSpec((k, D_MODEL), lambda i: (0, 0), pipeline_mode=pl.Buffered(1)),
                  row_spec, gspec, gspec],
        out_specs=[row_spec, row_spec],
        out_shape=[jax.ShapeDtypeStruct((m, D_MODEL), F32), jax.ShapeDtypeStruct((m, D_MODEL), BF16)],
        compiler_params=_params(("parallel",)), name=name,
    )(a, w, res, g, g2)


HALO = N_META
FFN_ROW_BLOCKS = 4


GELU_K1 = -2.0 * math.sqrt(2.0 / math.pi) * LOG2E
GELU_K2 = GELU_K1 * 0.044715


def _gelu_tanh(x):
    return x / (1.0 + jnp.exp2(x * (x * x * GELU_K2 + GELU_K1)))


def _ffn_up_kernel(tm, tiles_per_seq, hn_ref, prev_ref, next_ref, hnm_ref, wa_ref, wg_ref, wc_ref, bc_ref,
                   f_ref, ext_ref):
    i = pl.program_id(0)

    @pl.when(pl.program_id(1) == 0)
    def _():
        first = i % tiles_per_seq == 0
        last = i % tiles_per_seq == tiles_per_seq - 1
        ext_ref[0:HALO, :] = jnp.where(first, hnm_ref[...], prev_ref[...])
        ext_ref[HALO:HALO + tm, :] = hn_ref[...]
        ext_ref[HALO + tm:, :] = jnp.where(last, jnp.zeros_like(next_ref), next_ref[...])

    wc = wc_ref[...]
    bc = bc_ref[...]
    n_chunks = f_ref.shape[1] // MXU_N
    rows_g = tm // FFN_ROW_BLOCKS
    rows_e = tm // (2 * FFN_ROW_BLOCKS)
    pad = 8

    def a_block(c, r):
        lo = 0 if r == 0 else HALO + r * rows_g
        hi = tm + 2 * HALO if r == FFN_ROW_BLOCKS - 1 else HALO + (r + 1) * rows_g
        return jnp.dot(ext_ref[lo:hi, :], wa_ref[:, c * MXU_N:(c + 1) * MXU_N], preferred_element_type=F32)

    def g_block(c, r):
        return jnp.dot(ext_ref[HALO + r * rows_g:HALO + (r + 1) * rows_g, :],
                       wg_ref[:, c * MXU_N:(c + 1) * MXU_N], preferred_element_type=F32)

    def epilogue_piece(c, a, gate, k):
        cols = slice(c * MXU_N, (c + 1) * MXU_N)
        lo = HALO + k * rows_e
        win = a[lo - pad:lo + rows_e + pad]
        prev = pltpu.roll(win, 1, 0)[pad:pad + rows_e]
        nxt = pltpu.roll(win, win.shape[0] - 1, 0)[pad:pad + rows_e]
        conv = prev * wc[0:1, cols] + win[pad:pad + rows_e] * wc[1:2, cols] + nxt * wc[2:3, cols] + bc[:, cols]
        g = gate[k * rows_e:(k + 1) * rows_e]
        f_ref[k * rows_e:(k + 1) * rows_e, cols] = (_gelu_tanh(conv) * g).astype(f_ref.dtype)

    a_next = jnp.concatenate([a_block(0, r) for r in range(FFN_ROW_BLOCKS)], axis=0)
    g_next = jnp.concatenate([g_block(0, r) for r in range(FFN_ROW_BLOCKS)], axis=0)
    for c in range(n_chunks):
        a_cur, g_cur = a_next, g_next
        a_parts, g_parts = [], []
        for k in range(2 * FFN_ROW_BLOCKS):
            if c + 1 < n_chunks:
                if k < FFN_ROW_BLOCKS:
                    a_parts.append(a_block(c + 1, k))
                else:
                    g_parts.append(g_block(c + 1, k - FFN_ROW_BLOCKS))
            epilogue_piece(c, a_cur, g_cur, k)
        if c + 1 < n_chunks:
            a_next = jnp.concatenate(a_parts, axis=0)
            g_next = jnp.concatenate(g_parts, axis=0)


def _ffn_up(hn, hn_meta, wa, wg, wc, bc, tm, tf, name):
    m = hn.shape[0]
    tiles_per_seq = SEQ // tm
    hb = tm // HALO
    last = m // HALO - 1
    return pl.pallas_call(
        functools.partial(_ffn_up_kernel, tm, tiles_per_seq),
        grid=(m // tm, D_FF_PAD // tf),
        in_specs=[
            pl.BlockSpec((tm, D_MODEL), lambda i, j: (i, 0)),
            pl.BlockSpec((HALO, D_MODEL), lambda i, j: (jnp.maximum(i * hb - 1, 0), 0)),
            pl.BlockSpec((HALO, D_MODEL), lambda i, j: (jnp.minimum((i + 1) * hb, last), 0)),
            pl.BlockSpec((N_META, D_MODEL), lambda i, j: (i // tiles_per_seq, 0)),
            pl.BlockSpec((D_MODEL, tf), lambda i, j: (0, j)),
            pl.BlockSpec((D_MODEL, tf), lambda i, j: (0, j)),
            pl.BlockSpec((3, tf), lambda i, j: (0, j)),
            pl.BlockSpec((1, tf), lambda i, j: (0, j)),
        ],
        out_specs=pl.BlockSpec((tm, tf), lambda i, j: (i, j)),
        out_shape=jax.ShapeDtypeStruct((m, D_FF_PAD), BF16),
        scratch_shapes=[pltpu.VMEM((tm + 2 * HALO, D_MODEL), BF16)],
        compiler_params=_params(("parallel", "arbitrary")),
        name=name,
    )(hn, hn, hn, hn_meta, wa, wg, wc, bc)


SRC_CQ, SRC_CKV, SRC_KR = 0, MLA_Q_LORA, MLA_Q_LORA + MLA_KV_LORA
SRC_GQ = SRC_KR + MLA_ROPE
SRC_GK = SRC_GQ + GQA_HEADS * HEAD_DIM
SRC_GV = SRC_GK + GQA_KV_HEADS * HEAD_DIM
SRC_GA = SRC_GV + GQA_KV_HEADS * HEAD_DIM
SRC_GB = SRC_GA + D_MODEL
ROW_BLK = HEAD_DIM // 4
ROW_BLKS_PER_STEP = 4


def _w_in_block_table():
    table = np.full(D_IN_PAD // ROW_BLK, -1, np.int32)

    def fill(dst, src, n_blocks, order=None):
        for b in range(n_blocks):
            s = b if order is None else (b // len(order)) * len(order) + order[b % len(order)]
            if s >= 0:
                table[dst // ROW_BLK + b] = src // ROW_BLK + s

    for dst, src, width in ((OFF_GA, SRC_GA, D_MODEL), (OFF_GB, SRC_GB, D_MODEL), (OFF_CQ, SRC_CQ, MLA_Q_LORA),
                            (OFF_CKV, SRC_CKV, MLA_KV_LORA), (OFF_GV, SRC_GV, GQA_KV_HEADS * HEAD_DIM)):
        fill(dst, src, width // ROW_BLK)
    fill(OFF_GQ, SRC_GQ, GQA_HEADS * HEAD_DIM // ROW_BLK, order=(0, 2, 1, 3))
    fill(OFF_GK, SRC_GK, GQA_KV_HEADS * HEAD_DIM // ROW_BLK, order=(0, 2, 1, 3))
    table[OFF_KR // ROW_BLK] = SRC_KR // ROW_BLK
    table[OFF_KR // ROW_BLK + 2] = SRC_KR // ROW_BLK + 1
    return table


def _w_in_rows_kernel(tbl_ref, *refs):
    *x_refs, o_ref = refs
    i = pl.program_id(0)
    for q, x_ref in enumerate(x_refs):
        keep = tbl_ref[i * ROW_BLKS_PER_STEP + q] >= 0
        o_ref[q * ROW_BLK:(q + 1) * ROW_BLK, :] = jnp.where(keep, x_ref[...], 0.0).astype(o_ref.dtype)


def _w_in_rows_layout(wt):
    table = jnp.asarray(_w_in_block_table())

    def in_spec(q):
        return pl.BlockSpec((ROW_BLK, D_MODEL),
                            lambda i, tbl: (jnp.maximum(tbl[i * ROW_BLKS_PER_STEP + q], 0), 0))

    rows = ROW_BLK * ROW_BLKS_PER_STEP
    return pl.pallas_call(
        _w_in_rows_kernel,
        grid_spec=pltpu.PrefetchScalarGridSpec(
            num_scalar_prefetch=1,
            grid=(D_IN_PAD // rows,),
            in_specs=[in_spec(q) for q in range(ROW_BLKS_PER_STEP)],
            out_specs=pl.BlockSpec((rows, D_MODEL), lambda i, tbl: (i, 0)),
        ),
        out_shape=jax.ShapeDtypeStruct((D_IN_PAD, D_MODEL), BF16),
        compiler_params=_params(("arbitrary",)),
        name="w_in_layout",
    )(table, *([wt] * ROW_BLKS_PER_STEP))


def _mm_nt_kernel(a_ref, bt_ref, o_ref):
    o_ref[...] = lax.dot_general(a_ref[...], bt_ref[...], NT_DIMS, preferred_element_type=F32).astype(o_ref.dtype)


def _matmul_nt(a, bt, tm, tn, out_dtype, name):
    m, k = a.shape
    n = bt.shape[0]
    return pl.pallas_call(
        _mm_nt_kernel,
        grid=(m // tm, n // tn),
        in_specs=[pl.BlockSpec((tm, k), lambda i, j: (i, 0)), pl.BlockSpec((tn, k), lambda i, j: (j, 0))],
        out_specs=pl.BlockSpec((tm, tn), lambda i, j: (i, j)),
        out_shape=jax.ShapeDtypeStruct((m, n), out_dtype),
        compiler_params=_params(("parallel", "parallel")),
        name=name,
    )(a, bt)


def _prep_weights(g_pre_mix, w_in, g_cq, w_uq, g_ckv, w_ukv, g_qn, g_kn, w_pa, w_pb, w_o,
                  g_post_mix, g_pre_ffn, w_up, w_conv, b_conv, w_down, g_post_ffn):
    qtr = HEAD_DIM // 4
    perm_b = np.arange(HEAD_DIM).reshape(2, 2, qtr).transpose(1, 0, 2).reshape(-1)
    half = MLA_ROPE // 2
    w_in_p = _w_in_rows_layout(w_in[0].T)

    uq = w_uq[0].reshape(MLA_Q_LORA, MLA_HEADS, MLA_NOPE + MLA_ROPE)
    zq = jnp.zeros((MLA_Q_LORA, MLA_HEADS, HALF_LANES - half), uq.dtype)
    uq = jnp.concatenate([uq[:, :, :MLA_NOPE], uq[:, :, MLA_NOPE:MLA_NOPE + half], zq,
                          uq[:, :, MLA_NOPE + half:], zq], axis=2).reshape(MLA_Q_LORA, -1).astype(BF16)
    ukv = w_ukv[0].reshape(MLA_KV_LORA, MLA_HEADS, MLA_NOPE + MLA_V)
    uk = ukv[:, :, :MLA_NOPE].reshape(MLA_KV_LORA, -1).astype(BF16)
    uv = ukv[:, :, MLA_NOPE:].reshape(MLA_KV_LORA, -1).astype(BF16)

    fpad = D_FF_PAD - D_FF
    up = w_up[0]
    return dict(
        g_pre_mix=g_pre_mix, w_in=w_in_p, g_cq=g_cq, w_uq=uq, g_ckv=g_ckv, w_uk=uk, w_uvt=uv.T,
        eye_kv=jnp.eye(GQA_KV_HEADS * HEAD_DIM, dtype=BF16),
        g_qn=g_qn[:, perm_b], g_kn=g_kn[:, perm_b],
        w_pa=w_pa[0].astype(BF16), w_pb=w_pb[0].astype(BF16), w_o=w_o[0].astype(BF16),
        g_post_mix=g_post_mix, g_pre_ffn=g_pre_ffn,
        w_up_a=jnp.pad(up[:, :D_FF], ((0, 0), (0, fpad))).astype(BF16),
        w_up_g=jnp.pad(up[:, D_FF:], ((0, 0), (0, fpad))).astype(BF16),
        w_conv=jnp.pad(w_conv[0], ((0, 0), (0, fpad))), b_conv=jnp.pad(b_conv, ((0, 0), (0, fpad))),
        w_down=jnp.pad(w_down[0], ((0, fpad), (0, 0))).astype(BF16), g_post_ffn=g_post_ffn,
    )


def _rope_tables(pos, dim):
    inv = ROPE_THETA ** (-jnp.arange(0, dim, 2, dtype=F32) / dim)
    ang = pos.astype(F32)[:, None] * inv[None, :]
    return jnp.cos(ang), jnp.sin(ang)


def _tables(pos, row, col):
    c1, s1 = _rope_tables(pos, MLA_ROPE)
    z = jnp.zeros((c1.shape[0], HALF_LANES - c1.shape[1]), F32)
    cos_a = jnp.concatenate([c1, z, c1, z], axis=1)
    sin_a = jnp.concatenate([-s1, z, s1, z], axis=1)
    cr, sr = _rope_tables(row, HEAD_DIM // 2)
    cc, sc = _rope_tables(col, HEAD_DIM // 2)
    cos_b = jnp.concatenate([cr, cc, cr, cc], axis=1)
    sin_b = jnp.concatenate([-sr, -sc, sr, sc], axis=1)
    return cos_a, sin_a, cos_b, sin_b


def kernel(x_prompt, x_sample, meta_tokens, g_pre_mix, w_in, g_cq, w_uq, g_ckv, w_ukv, g_qn, g_kn,
           w_pa, w_pb, w_o, g_post_mix, g_pre_ffn, w_up, w_conv, b_conv, w_down, g_post_ffn):
    w = _prep_weights(g_pre_mix, w_in, g_cq, w_uq, g_ckv, w_ukv, g_qn, g_kn, w_pa, w_pb, w_o,
                      g_post_mix, g_pre_ffn, w_up, w_conv, b_conv, w_down, g_post_ffn)
    t = jnp.arange(SEQ, dtype=jnp.int32)
    tabs_real = _tables(t + N_META, t // GRID_W, t % GRID_W)
    zero = jnp.zeros((N_META,), jnp.int32)
    tabs_meta = _tables(jnp.arange(N_META), zero, zero)
    return (_trunk(x_prompt, meta_tokens, w, tabs_real, tabs_meta, "p"),
            _trunk(x_sample, meta_tokens, w, tabs_real, tabs_meta, "s"))


def _trunk(x, meta_tokens, w, tabs_real, tabs_meta, tag):
    batch = x.shape[0]
    xr = x.reshape(batch * SEQ, D_MODEL)
    xm = jnp.broadcast_to(meta_tokens[None], (batch, N_META, D_MODEL)).reshape(batch * N_META, D_MODEL)
    mrows = batch * N_META
    tabs_meta = tuple(jnp.tile(a, (batch, 1)) for a in tabs_meta)

    z = _matmul_nt(_norm_rows(xr, w["g_pre_mix"], TM_NORM, "norm_" + tag), w["w_in"], TM_MM, TN_MM, BF16,
                   "in_proj_" + tag)
    zm = _matmul_nt(_norm_rows(xm, w["g_pre_mix"], mrows, "norm_meta_" + tag), w["w_in"], mrows, TN_MM, BF16,
                    "in_proj_meta_" + tag)
    qa, ka, vat, qb, kb, vbt = _heads(z, w, tabs_real, TM_HEADS, SEQ // TM_HEADS, "heads_" + tag)
    qam, kam, vatm, qbm, kbm, vbtm = _heads(zm, w, tabs_meta, mrows, 1, "heads_meta_" + tag)

    def per_batch(t):
        return t.reshape(t.shape[0], batch, N_META).transpose(1, 0, 2)

    def token_major(t):
        return t.transpose(0, 2, 1).reshape(batch * N_META, t.shape[1])

    vatm, vbtm = per_batch(vatm), per_batch(vbtm)

    cfg_a = dict(batch=batch, heads=MLA_HEADS, group=1, dk=MLA_QK)
    cfg_b = dict(batch=batch, heads=GQA_HEADS, group=GQA_GROUP, dk=HEAD_DIM)
    oa = _attention(qa, ka, vat, kam, vatm, lq=SEQ, tq=TQ_ATTN, tk=TK_ATTN, name="attn_a_" + tag, **cfg_a)
    ob = _attention(qb, kb, vbt, kbm, vbtm, lq=SEQ, tq=TQ_ATTN, tk=TK_ATTN, name="attn_b_" + tag, **cfg_b)
    oam = token_major(_attention(qam, ka, vat, kam, vatm, lq=N_META, tq=N_META, tk=TK_ATTN,
                                 name="attn_a_meta_" + tag, **cfg_a))
    obm = token_major(_attention(qbm, kb, vbt, kbm, vbtm, lq=N_META, tq=N_META, tk=TK_ATTN,
                                 name="attn_b_meta_" + tag, **cfg_b))

    mg = _merge(oa, ob, z, w["w_pa"], w["w_pb"], TM_MM, TN_MM, "merge_" + tag)
    mgm = _merge(oam, obm, zm, w["w_pa"], w["w_pb"], mrows, TN_MM, "merge_meta_" + tag)
    h1, hn = _proj_resident(mg, w["w_o"], xr, w["g_post_mix"], w["g_pre_ffn"], TM_OUT, "out_proj_" + tag)
    _, hnm = _proj_residual(mgm, w["w_o"], xm, w["g_post_mix"], w["g_pre_ffn"], mrows, TK_META,
                            "out_proj_meta_" + tag)

    f = _ffn_up(hn, hnm, w["w_up_a"], w["w_up_g"], w["w_conv"], w["b_conv"], TM_UP, TF_UP, "ffn_up_" + tag)
    y = _proj_residual(f, w["w_down"], h1, w["g_post_ffn"], None, TM_DOWN, TK_DOWN, "ffn_down_" + tag)
    return y.reshape(batch, SEQ, D_MODEL)
```

```python
import functools
import math

import jax
import jax.numpy as jnp
import numpy as np
from jax import lax
from jax.experimental import pallas as pl
from jax.experimental.pallas import tpu as pltpu

F32 = jnp.float32
BF16 = jnp.bfloat16

D_MODEL = 4096
SEQ = 4096
N_META = 16
GRID_W = 64
HEAD_DIM = 128
MLA_HEADS = 16
MLA_Q_LORA = 1024
MLA_KV_LORA = 512
MLA_NOPE = 128
MLA_ROPE = 64
MLA_V = 128
MLA_QK = 256
GQA_HEADS = 16
GQA_KV_HEADS = 4
GQA_GROUP = GQA_HEADS // GQA_KV_HEADS
D_FF = 11008
D_FF_PAD = 11264
ROPE_THETA = 10000.0
EPS = 1e-6
LOG2E = math.log2(math.e)

OFF_GA = 0
OFF_GB = 4096
OFF_GQ = 8192
OFF_CQ = 10240
OFF_CKV = 11264
OFF_GK = 11776
OFF_GV = 12288
OFF_KR = 12800
D_IN_PAD = 13312

LANES = 128
HALF_LANES = LANES // 2
MXU_N = 256
VMEM_LIMIT = 60 * 1024 * 1024

TM_NORM = 256
TM_MM, TN_MM = 1024, 1024
TM_HEADS = 256
TQ_ATTN, TK_ATTN = 1024, 2048
TM_OUT = 256
TM_UP, TF_UP = 1024, 512
TM_DOWN, TK_DOWN = 512, 1024
TK_META = 512


def _params(sem):
    return pltpu.CompilerParams(dimension_semantics=sem, vmem_limit_bytes=VMEM_LIMIT)


def _rms(x, g):
    return x * lax.rsqrt(jnp.mean(x * x, axis=-1, keepdims=True) + EPS) * g


def _norm_kernel(x_ref, g_ref, o_ref):
    o_ref[...] = _rms(x_ref[...], g_ref[...]).astype(o_ref.dtype)


def _norm_rows(x, g, tm, name):
    m = x.shape[0]
    return pl.pallas_call(
        _norm_kernel,
        grid=(m // tm,),
        in_specs=[pl.BlockSpec((tm, D_MODEL), lambda i: (i, 0)), pl.BlockSpec((1, D_MODEL), lambda i: (0, 0))],
        out_specs=pl.BlockSpec((tm, D_MODEL), lambda i: (i, 0)),
        out_shape=jax.ShapeDtypeStruct((m, D_MODEL), BF16),
        compiler_params=_params(("parallel",)),
        name=name,
    )(x, g)


NT_DIMS = (((1,), (1,)), ((), ()))


def _rope(x, cos, sin_signed):
    return x * cos + pltpu.roll(x, HALF_LANES, 1) * sin_signed


def _heads_kernel(cq_ref, ckv_ref, kr_ref, gq_ref, gk_ref, gv_ref, wuq_ref, wuk_ref, wuvt_ref, eye_ref,
                  gcq_ref, gckv_ref, gqn_ref, gkn_ref, cosa_ref, sina_ref, cosb_ref, sinb_ref,
                  qa_ref, ka_ref, vat_ref, qb_ref, kb_ref, vbt_ref):
    cosa, sina = cosa_ref[...], sina_ref[...]
    cosb, sinb = cosb_ref[...], sinb_ref[...]
    scale_a = LOG2E / math.sqrt(MLA_NOPE + MLA_ROPE)
    scale_b = LOG2E / math.sqrt(HEAD_DIM)

    cq = _rms(cq_ref[...].astype(F32), gcq_ref[...]).astype(BF16)
    for h in range(MLA_HEADS):
        q = jnp.dot(cq, wuq_ref[:, h * MLA_QK:(h + 1) * MLA_QK], preferred_element_type=F32)
        qa_ref[:, h * MLA_QK:h * MLA_QK + MLA_NOPE] = (q[:, :MLA_NOPE] * scale_a).astype(BF16)
        qa_ref[:, h * MLA_QK + MLA_NOPE:(h + 1) * MLA_QK] = (
            _rope(q[:, MLA_NOPE:], cosa, sina) * scale_a).astype(BF16)

    ckv = _rms(ckv_ref[...].astype(F32), gckv_ref[...]).astype(BF16)
    k_rope = _rope(kr_ref[...].astype(F32), cosa, sina).astype(BF16)
    k_nope = jnp.dot(ckv, wuk_ref[...], preferred_element_type=F32).astype(BF16)
    for h in range(MLA_HEADS):
        ka_ref[:, h * MLA_QK:h * MLA_QK + MLA_NOPE] = k_nope[:, h * MLA_NOPE:(h + 1) * MLA_NOPE]
        ka_ref[:, h * MLA_QK + MLA_NOPE:(h + 1) * MLA_QK] = k_rope
    vat_ref[...] = lax.dot_general(wuvt_ref[...], ckv, NT_DIMS, preferred_element_type=F32).astype(BF16)
    vbt_ref[...] = lax.dot_general(eye_ref[...], gv_ref[...], NT_DIMS, preferred_element_type=F32).astype(BF16)

    gqn, gkn = gqn_ref[...], gkn_ref[...]
    for h in range(GQA_HEADS):
        x = gq_ref[:, h * HEAD_DIM:(h + 1) * HEAD_DIM].astype(F32)
        qb_ref[:, h * HEAD_DIM:(h + 1) * HEAD_DIM] = (_rope(_rms(x, gqn), cosb, sinb) * scale_b).astype(BF16)
    for h in range(GQA_KV_HEADS):
        x = gk_ref[:, h * HEAD_DIM:(h + 1) * HEAD_DIM].astype(F32)
        kb_ref[:, h * HEAD_DIM:(h + 1) * HEAD_DIM] = _rope(_rms(x, gkn), cosb, sinb).astype(BF16)


def _heads(z, w, tabs, tm, tab_blocks, name):
    m = z.shape[0]

    def zspec(width, off):
        return pl.BlockSpec((tm, width), lambda i: (i, off // width))

    def full(a):
        return pl.BlockSpec(a.shape, lambda i: (0,) * a.ndim)

    tab_spec = pl.BlockSpec((tm, LANES), lambda i: (i % tab_blocks, 0))
    outs = ((MLA_HEADS * MLA_QK, True), (MLA_HEADS * MLA_QK, True), (MLA_HEADS * MLA_V, False),
            (GQA_HEADS * HEAD_DIM, True), (GQA_KV_HEADS * HEAD_DIM, True), (GQA_KV_HEADS * HEAD_DIM, False))
    consts = (w["w_uq"], w["w_uk"], w["w_uvt"], w["eye_kv"], w["g_cq"], w["g_ckv"], w["g_qn"], w["g_kn"])
    return pl.pallas_call(
        _heads_kernel,
        grid=(m // tm,),
        in_specs=[zspec(MLA_Q_LORA, OFF_CQ), zspec(MLA_KV_LORA, OFF_CKV), zspec(LANES, OFF_KR),
                  zspec(GQA_HEADS * HEAD_DIM, OFF_GQ), zspec(GQA_KV_HEADS * HEAD_DIM, OFF_GK),
                  zspec(GQA_KV_HEADS * HEAD_DIM, OFF_GV)]
        + [full(a) for a in consts] + [tab_spec] * 4,
        out_specs=[pl.BlockSpec((tm, n), lambda i: (i, 0)) if rows else pl.BlockSpec((n, tm), lambda i: (0, i))
                   for n, rows in outs],
        out_shape=[jax.ShapeDtypeStruct((m, n) if rows else (n, m), BF16) for n, rows in outs],
        compiler_params=_params(("parallel",)),
        name=name,
    )(z, z, z, z, z, z, *consts, *tabs)


def _attn_kernel(tk, transpose_out, q_ref, k_ref, vt_ref, km_ref, vmt_ref, o_ref):
    q = q_ref[...]
    s = lax.dot_general(km_ref[...], q, NT_DIMS, preferred_element_type=F32)
    m = jnp.max(s, axis=0, keepdims=True)
    p = jnp.exp2(s - m)
    l = jnp.sum(p, axis=0, keepdims=True)
    acc = jnp.dot(vmt_ref[...], p.astype(BF16), preferred_element_type=F32)

    def scores(c):
        return lax.dot_general(k_ref[c * tk:(c + 1) * tk, :], q, NT_DIMS, preferred_element_type=F32)

    n_chunks = k_ref.shape[0] // tk
    s_next = scores(0)
    for c in range(n_chunks):
        s = s_next
        if c + 1 < n_chunks:
            s_next = scores(c + 1)
        m_new = jnp.maximum(m, jnp.max(s, axis=0, keepdims=True))
        alpha = jnp.exp2(m - m_new)
        p = jnp.exp2(s - m_new)
        l = alpha * l + jnp.sum(p, axis=0, keepdims=True)
        acc = alpha * acc + jnp.dot(vt_ref[:, c * tk:(c + 1) * tk], p.astype(BF16), preferred_element_type=F32)
        m = m_new
    o = acc / l
    o_ref[...] = (o.T if transpose_out else o).astype(o_ref.dtype)


def _attention(q, k, vt, km, vmt, *, batch, heads, group, dk, lq, tq, tk, name):
    nq = lq // tq
    dv = HEAD_DIM
    transpose_out = tq % LANES == 0
    if transpose_out:
        out_spec = pl.BlockSpec((tq, dv), lambda b, h, i: (b * nq + i, h))
        out_shape = jax.ShapeDtypeStruct((batch * lq, heads * dv), BF16)
    else:
        out_spec = pl.BlockSpec((None, dv, tq), lambda b, h, i: (b, h, i))
        out_shape = jax.ShapeDtypeStruct((batch, heads * dv, lq), BF16)
    return pl.pallas_call(
        functools.partial(_attn_kernel, tk, transpose_out),
        grid=(batch, heads, nq),
        in_specs=[
            pl.BlockSpec((tq, dk), lambda b, h, i: (b * nq + i, h)),
            pl.BlockSpec((SEQ, dk), lambda b, h, i: (b, h // group)),
            pl.BlockSpec((dv, SEQ), lambda b, h, i: (h // group, b)),
            pl.BlockSpec((N_META, dk), lambda b, h, i: (b, h // group)),
            pl.BlockSpec((None, dv, N_META), lambda b, h, i: (b, h // group, 0)),
        ],
        out_specs=out_spec,
        out_shape=out_shape,
        compiler_params=_params(("parallel", "parallel", "parallel")),
        name=name,
    )(q, k, vt, km, vmt)


def _merge_kernel(oa_ref, ob_ref, ga_ref, gb_ref, wpa_ref, wpb_ref, o_ref):
    a = jnp.dot(oa_ref[...], wpa_ref[...], preferred_element_type=F32)
    b = jnp.dot(ob_ref[...], wpb_ref[...], preferred_element_type=F32)
    ga = jax.nn.sigmoid(ga_ref[...].astype(F32))
    gb = jax.nn.sigmoid(gb_ref[...].astype(F32))
    o_ref[...] = (ga * a + gb * b).astype(o_ref.dtype)


def _merge(oa, ob, z, wpa, wpb, tm, tn, name):
    m, k = oa.shape
    return pl.pallas_call(
        _merge_kernel,
        grid=(m // tm, D_MODEL // tn),
        in_specs=[
            pl.BlockSpec((tm, k), lambda i, j: (i, 0)),
            pl.BlockSpec((tm, k), lambda i, j: (i, 0)),
            pl.BlockSpec((tm, tn), lambda i, j: (i, OFF_GA // tn + j)),
            pl.BlockSpec((tm, tn), lambda i, j: (i, OFF_GB // tn + j)),
            pl.BlockSpec((k, tn), lambda i, j: (0, j)),
            pl.BlockSpec((k, tn), lambda i, j: (0, j)),
        ],
        out_specs=pl.BlockSpec((tm, tn), lambda i, j: (i, j)),
        out_shape=jax.ShapeDtypeStruct((m, D_MODEL), BF16),
        compiler_params=_params(("parallel", "parallel")),
        name=name,
    )(oa, ob, z, z, wpa, wpb)


PROJ_CHUNK = 512
NORM_ROWS = 128
NORM_PIECE = 16


def _inv_rms(x):
    return lax.rsqrt(jnp.mean(x * x, axis=-1, keepdims=True) + EPS)


def _norm_residual_items(res_ref, g_ref, g2_ref, h_ref, hn_ref, row_lo, row_hi):
    items = []
    for r0 in range(row_lo, row_hi, NORM_ROWS):
        blk = min(NORM_ROWS, row_hi - r0)
        pieces = [(r0 + s0, min(NORM_PIECE, blk - s0), s0) for s0 in range(0, blk, NORM_PIECE)]
        inv = {}

        def stats(key, r0=r0, blk=blk, inv=inv):
            inv[key] = _inv_rms(h_ref[r0:r0 + blk, :])

        def scale(lo, sz, s0, inv=inv):
            r = inv["h"][s0:s0 + sz]
            h_ref[lo:lo + sz, :] = res_ref[lo:lo + sz, :] + h_ref[lo:lo + sz, :] * r * g_ref[...]

        def scale_next(lo, sz, s0, inv=inv):
            r = inv["hn"][s0:s0 + sz]
            hn_ref[lo:lo + sz, :] = (h_ref[lo:lo + sz, :] * r * g2_ref[...]).astype(hn_ref.dtype)

        items.append(functools.partial(stats, "h"))
        items += [functools.partial(scale, *p) for p in pieces]
        if hn_ref is not None:
            items.append(functools.partial(stats, "hn"))
            items += [functools.partial(scale_next, *p) for p in pieces]
    return items


def _alternate(leaders, followers):
    n = len(leaders)
    for i, lead in enumerate(leaders):
        lead()
        for f in followers[i * len(followers) // n:(i + 1) * len(followers) // n]:
            f()


def _project_and_normalise(dot_into, res_ref, g_ref, g2_ref, h_ref, hn_ref):
    tm = h_ref.shape[0]
    half = tm // 2 if tm % (2 * NORM_ROWS) == 0 else tm
    cols = range(0, D_MODEL, PROJ_CHUNK)
    norm = functools.partial(_norm_residual_items, res_ref, g_ref, g2_ref, h_ref, hn_ref)
    for c in cols:
        dot_into(slice(0, half), c)
    if half < tm:
        _alternate([functools.partial(dot_into, slice(half, tm), c) for c in cols], norm(0, half))
    for item in norm(half if half < tm else 0, tm):
        item()


def _proj_residual_kernel(nk, a_ref, w_ref, res_ref, g_ref, g2_ref, h_ref, hn_ref):
    kk = pl.program_id(1)

    def dot_into(first, rows, c):
        part = jnp.dot(a_ref[rows, :], w_ref[:, c:c + PROJ_CHUNK], preferred_element_type=F32)
        if first:
            h_ref[rows, c:c + PROJ_CHUNK] = part
        else:
            h_ref[rows, c:c + PROJ_CHUNK] += part

    def accumulate(first):
        for c in range(0, D_MODEL, PROJ_CHUNK):
            dot_into(first, slice(None), c)

    def finish(first):
        _project_and_normalise(functools.partial(dot_into, first), res_ref, g_ref, g2_ref, h_ref, hn_ref)

    if nk == 1:
        finish(True)
    else:
        pl.when(kk == 0)(functools.partial(accumulate, True))
        pl.when(jnp.logical_and(kk > 0, kk < nk - 1))(functools.partial(accumulate, False))
        pl.when(kk == nk - 1)(functools.partial(finish, False))


def _proj_residual_nohn_kernel(nk, a_ref, w_ref, res_ref, g_ref, h_ref):
    _proj_residual_kernel(nk, a_ref, w_ref, res_ref, g_ref, None, h_ref, None)


def _proj_residual(a, w, res, g, g2, tm, tk, name):
    m, k = a.shape
    nk = k // tk
    gspec = pl.BlockSpec((1, D_MODEL), lambda i, kk: (0, 0))
    row_spec = pl.BlockSpec((tm, D_MODEL), lambda i, kk: (i, 0))
    in_specs = [pl.BlockSpec((tm, tk), lambda i, kk: (i, kk)),
                pl.BlockSpec((tk, D_MODEL), lambda i, kk: (kk, 0)), row_spec, gspec]
    if g2 is None:
        return pl.pallas_call(
            functools.partial(_proj_residual_nohn_kernel, nk),
            grid=(m // tm, nk), in_specs=in_specs, out_specs=row_spec,
            out_shape=jax.ShapeDtypeStruct((m, D_MODEL), F32),
            compiler_params=_params(("parallel", "arbitrary")), name=name,
        )(a, w, res, g)
    return pl.pallas_call(
        functools.partial(_proj_residual_kernel, nk),
        grid=(m // tm, nk), in_specs=in_specs + [gspec], out_specs=[row_spec, row_spec],
        out_shape=[jax.ShapeDtypeStruct((m, D_MODEL), F32), jax.ShapeDtypeStruct((m, D_MODEL), BF16)],
        compiler_params=_params(("parallel", "arbitrary")), name=name,
    )(a, w, res, g, g2)


def _proj_resident_kernel(a_ref, w_ref, res_ref, g_ref, g2_ref, h_ref, hn_ref):
    def dot_into(rows, c):
        h_ref[rows, c:c + PROJ_CHUNK] = jnp.dot(a_ref[rows, :], w_ref[:, c:c + PROJ_CHUNK],
                                                preferred_element_type=F32)

    _project_and_normalise(dot_into, res_ref, g_ref, g2_ref, h_ref, hn_ref)


def _proj_resident(a, w, res, g, g2, tm, name):
    m, k = a.shape
    gspec = pl.BlockSpec((1, D_MODEL), lambda i: (0, 0))
    row_spec = pl.BlockSpec((tm, D_MODEL), lambda i: (i, 0))
    return pl.pallas_call(
        _proj_resident_kernel,
        grid=(m // tm,),
        in_specs=[pl.BlockSpec((tm, k), lambda i: (i, 0)),
                  pl.BlockSpec((k, D_MODEL), lambda i: (0, 0), pipeline_mode=pl.Buffered(1)),
                  row_spec, gspec, gspec],
        out_specs=[row_spec, row_spec],
        out_shape=[jax.ShapeDtypeStruct((m, D_MODEL), F32), jax.ShapeDtypeStruct((m, D_MODEL), BF16)],
        compiler_params=_params(("parallel",)), name=name,
    )(a, w, res, g, g2)


HALO = N_META
FFN_ROW_BLOCKS = 4


GELU_K1 = -2.0 * math.sqrt(2.0 / math.pi) * LOG2E
GELU_K2 = GELU_K1 * 0.044715


def _gelu_tanh(x):
    return x / (1.0 + jnp.exp2(x * (x * x * GELU_K2 + GELU_K1)))


def _ffn_up_kernel(tm, tiles_per_seq, hn_ref, prev_ref, next_ref, hnm_ref, wa_ref, wg_ref, wc_ref, bc_ref,
                   f_ref, ext_ref):
    i = pl.program_id(0)

    @pl.when(pl.program_id(1) == 0)
    def _():
        first = i % tiles_per_seq == 0
        last = i % tiles_per_seq == tiles_per_seq - 1
        ext_ref[0:HALO, :] = jnp.where(first, hnm_ref[...], prev_ref[...])
        ext_ref[HALO:HALO + tm, :] = hn_ref[...]
        ext_ref[HALO + tm:, :] = jnp.where(last, jnp.zeros_like(next_ref), next_ref[...])

    wc = wc_ref[...]
    bc = bc_ref[...]
    n_chunks = f_ref.shape[1] // MXU_N
    rows_g = tm // FFN_ROW_BLOCKS
    rows_e = tm // (2 * FFN_ROW_BLOCKS)
    pad = 8

    def a_block(c, r):
        lo = 0 if r == 0 else HALO + r * rows_g
        hi = tm + 2 * HALO if r == FFN_ROW_BLOCKS - 1 else HALO + (r + 1) * rows_g
        return jnp.dot(ext_ref[lo:hi, :], wa_ref[:, c * MXU_N:(c + 1) * MXU_N], preferred_element_type=F32)

    def g_block(c, r):
        return jnp.dot(ext_ref[HALO + r * rows_g:HALO + (r + 1) * rows_g, :],
                       wg_ref[:, c * MXU_N:(c + 1) * MXU_N], preferred_element_type=F32)

    def epilogue_piece(c, a, gate, k):
        cols = slice(c * MXU_N, (c + 1) * MXU_N)
        lo = HALO + k * rows_e
        win = a[lo - pad:lo + rows_e + pad]
        prev = pltpu.roll(win, 1, 0)[pad:pad + rows_e]
        nxt = pltpu.roll(win, win.shape[0] - 1, 0)[pad:pad + rows_e]
        conv = prev * wc[0:1, cols] + win[pad:pad + rows_e] * wc[1:2, cols] + nxt * wc[2:3, cols] + bc[:, cols]
        g = gate[k * rows_e:(k + 1) * rows_e]
        f_ref[k * rows_e:(k + 1) * rows_e, cols] = (_gelu_tanh(conv) * g).astype(f_ref.dtype)

    a_next = jnp.concatenate([a_block(0, r) for r in range(FFN_ROW_BLOCKS)], axis=0)
    g_next = jnp.concatenate([g_block(0, r) for r in range(FFN_ROW_BLOCKS)], axis=0)
    for c in range(n_chunks):
        a_cur, g_cur = a_next, g_next
        a_parts, g_parts = [], []
        for k in range(2 * FFN_ROW_BLOCKS):
            if c + 1 < n_chunks:
                if k < FFN_ROW_BLOCKS:
                    a_parts.append(a_block(c + 1, k))
                else:
                    g_parts.append(g_block(c + 1, k - FFN_ROW_BLOCKS))
            epilogue_piece(c, a_cur, g_cur, k)
        if c + 1 < n_chunks:
            a_next = jnp.concatenate(a_parts, axis=0)
            g_next = jnp.concatenate(g_parts, axis=0)


def _ffn_up(hn, hn_meta, wa, wg, wc, bc, tm, tf, name):
    m = hn.shape[0]
    tiles_per_seq = SEQ // tm
    hb = tm // HALO
    last = m // HALO - 1
    return pl.pallas_call(
        functools.partial(_ffn_up_kernel, tm, tiles_per_seq),
        grid=(m // tm, D_FF_PAD // tf),
        in_specs=[
            pl.BlockSpec((tm, D_MODEL), lambda i, j: (i, 0)),
            pl.BlockSpec((HALO, D_MODEL), lambda i, j: (jnp.maximum(i * hb - 1, 0), 0)),
            pl.BlockSpec((HALO, D_MODEL), lambda i, j: (jnp.minimum((i + 1) * hb, last), 0)),
            pl.BlockSpec((N_META, D_MODEL), lambda i, j: (i // tiles_per_seq, 0)),
            pl.BlockSpec((D_MODEL, tf), lambda i, j: (0, j)),
            pl.BlockSpec((D_MODEL, tf), lambda i, j: (0, j)),
            pl.BlockSpec((3, tf), lambda i, j: (0, j)),
            pl.BlockSpec((1, tf), lambda i, j: (0, j)),
        ],
        out_specs=pl.BlockSpec((tm, tf), lambda i, j: (i, j)),
        out_shape=jax.ShapeDtypeStruct((m, D_FF_PAD), BF16),
        scratch_shapes=[pltpu.VMEM((tm + 2 * HALO, D_MODEL), BF16)],
        compiler_params=_params(("parallel", "arbitrary")),
        name=name,
    )(hn, hn, hn, hn_meta, wa, wg, wc, bc)


SRC_CQ, SRC_CKV, SRC_KR = 0, MLA_Q_LORA, MLA_Q_LORA + MLA_KV_LORA
SRC_GQ = SRC_KR + MLA_ROPE
SRC_GK = SRC_GQ + GQA_HEADS * HEAD_DIM
SRC_GV = SRC_GK + GQA_KV_HEADS * HEAD_DIM
SRC_GA = SRC_GV + GQA_KV_HEADS * HEAD_DIM
SRC_GB = SRC_GA + D_MODEL
ROW_BLK = HEAD_DIM // 4
ROW_BLKS_PER_STEP = 4


def _w_in_block_table():
    table = np.full(D_IN_PAD // ROW_BLK, -1, np.int32)

    def fill(dst, src, n_blocks, order=None):
        for b in range(n_blocks):
            s = b if order is None else (b // len(order)) * len(order) + order[b % len(order)]
            if s >= 0:
                table[dst // ROW_BLK + b] = src // ROW_BLK + s

    for dst, src, width in ((OFF_GA, SRC_GA, D_MODEL), (OFF_GB, SRC_GB, D_MODEL), (OFF_CQ, SRC_CQ, MLA_Q_LORA),
                            (OFF_CKV, SRC_CKV, MLA_KV_LORA), (OFF_GV, SRC_GV, GQA_KV_HEADS * HEAD_DIM)):
        fill(dst, src, width // ROW_BLK)
    fill(OFF_GQ, SRC_GQ, GQA_HEADS * HEAD_DIM // ROW_BLK, order=(0, 2, 1, 3))
    fill(OFF_GK, SRC_GK, GQA_KV_HEADS * HEAD_DIM // ROW_BLK, order=(0, 2, 1, 3))
    table[OFF_KR // ROW_BLK] = SRC_KR // ROW_BLK
    table[OFF_KR // ROW_BLK + 2] = SRC_KR // ROW_BLK + 1
    return table


def _w_in_rows_kernel(tbl_ref, *refs):
    *x_refs, o_ref = refs
    i = pl.program_id(0)
    for q, x_ref in enumerate(x_refs):
        keep = tbl_ref[i * ROW_BLKS_PER_STEP + q] >= 0
        o_ref[q * ROW_BLK:(q + 1) * ROW_BLK, :] = jnp.where(keep, x_ref[...], 0.0).astype(o_ref.dtype)


def _w_in_rows_layout(wt):
    table = jnp.asarray(_w_in_block_table())

    def in_spec(q):
        return pl.BlockSpec((ROW_BLK, D_MODEL),
                            lambda i, tbl: (jnp.maximum(tbl[i * ROW_BLKS_PER_STEP + q], 0), 0))

    rows = ROW_BLK * ROW_BLKS_PER_STEP
    return pl.pallas_call(
        _w_in_rows_kernel,
        grid_spec=pltpu.PrefetchScalarGridSpec(
            num_scalar_prefetch=1,
            grid=(D_IN_PAD // rows,),
            in_specs=[in_spec(q) for q in range(ROW_BLKS_PER_STEP)],
            out_specs=pl.BlockSpec((rows, D_MODEL), lambda i, tbl: (i, 0)),
        ),
        out_shape=jax.ShapeDtypeStruct((D_IN_PAD, D_MODEL), BF16),
        compiler_params=_params(("arbitrary",)),
        name="w_in_layout",
    )(table, *([wt] * ROW_BLKS_PER_STEP))


PAD_BLK = 256
assert D_FF % PAD_BLK == 0 and D_FF_PAD == D_FF + PAD_BLK


def _cast_pad_kernel(axis, n_valid, x_ref, o_ref):
    o_ref[...] = jnp.where(pl.program_id(axis) < n_valid, x_ref[...], 0.0).astype(o_ref.dtype)


def _cast_pad(x, out_shape, block, axis, first_block, n_valid, name):
    grid = tuple(s // b for s, b in zip(out_shape, block))

    def in_index(*ids):
        ids = list(ids)
        ids[axis] = first_block + jnp.minimum(ids[axis], n_valid - 1)
        return tuple(ids)

    return pl.pallas_call(
        functools.partial(_cast_pad_kernel, axis, n_valid),
        grid=grid,
        in_specs=[pl.BlockSpec(block, in_index)],
        out_specs=pl.BlockSpec(block, lambda *ids: ids),
        out_shape=jax.ShapeDtypeStruct(out_shape, BF16),
        compiler_params=_params(("parallel",) * len(grid)),
        name=name,
    )(x)


def _mm_nt_kernel(a_ref, bt_ref, o_ref):
    o_ref[...] = lax.dot_general(a_ref[...], bt_ref[...], NT_DIMS, preferred_element_type=F32).astype(o_ref.dtype)


def _matmul_nt(a, bt, tm, tn, out_dtype, name):
    m, k = a.shape
    n = bt.shape[0]
    return pl.pallas_call(
        _mm_nt_kernel,
        grid=(m // tm, n // tn),
        in_specs=[pl.BlockSpec((tm, k), lambda i, j: (i, 0)), pl.BlockSpec((tn, k), lambda i, j: (j, 0))],
        out_specs=pl.BlockSpec((tm, tn), lambda i, j: (i, j)),
        out_shape=jax.ShapeDtypeStruct((m, n), out_dtype),
        compiler_params=_params(("parallel", "parallel")),
        name=name,
    )(a, bt)


def _prep_weights(g_pre_mix, w_in, g_cq, w_uq, g_ckv, w_ukv, g_qn, g_kn, w_pa, w_pb, w_o,
                  g_post_mix, g_pre_ffn, w_up, w_conv, b_conv, w_down, g_post_ffn):
    qtr = HEAD_DIM // 4
    perm_b = np.arange(HEAD_DIM).reshape(2, 2, qtr).transpose(1, 0, 2).reshape(-1)
    half = MLA_ROPE // 2
    w_in_p = _w_in_rows_layout(w_in[0].T)

    uq = w_uq[0].reshape(MLA_Q_LORA, MLA_HEADS, MLA_NOPE + MLA_ROPE)
    zq = jnp.zeros((MLA_Q_LORA, MLA_HEADS, HALF_LANES - half), uq.dtype)
    uq = jnp.concatenate([uq[:, :, :MLA_NOPE], uq[:, :, MLA_NOPE:MLA_NOPE + half], zq,
                          uq[:, :, MLA_NOPE + half:], zq], axis=2).reshape(MLA_Q_LORA, -1).astype(BF16)
    ukv = w_ukv[0].reshape(MLA_KV_LORA, MLA_HEADS, MLA_NOPE + MLA_V)
    uk = ukv[:, :, :MLA_NOPE].reshape(MLA_KV_LORA, -1).astype(BF16)
    uv = ukv[:, :, MLA_NOPE:].reshape(MLA_KV_LORA, -1).astype(BF16)

    fpad = D_FF_PAD - D_FF
    up = w_up[0]
    return dict(
        g_pre_mix=g_pre_mix, w_in=w_in_p, g_cq=g_cq, w_uq=uq, g_ckv=g_ckv, w_uk=uk, w_uvt=uv.T,
        eye_kv=jnp.eye(GQA_KV_HEADS * HEAD_DIM, dtype=BF16),
        g_qn=g_qn[:, perm_b], g_kn=g_kn[:, perm_b],
        w_pa=w_pa[0].astype(BF16), w_pb=w_pb[0].astype(BF16), w_o=w_o[0].astype(BF16),
        g_post_mix=g_post_mix, g_pre_ffn=g_pre_ffn,
        w_up_a=jnp.pad(up[:, :D_FF], ((0, 0), (0, fpad))).astype(BF16),
        w_up_g=_cast_pad(up, (D_MODEL, D_FF_PAD), (D_MODEL // 2, PAD_BLK), 1, D_FF // PAD_BLK, D_FF // PAD_BLK,
                         "w_up_gate_layout"),
        w_conv=jnp.pad(w_conv[0], ((0, 0), (0, fpad))), b_conv=jnp.pad(b_conv, ((0, 0), (0, fpad))),
        w_down=_cast_pad(w_down[0], (D_FF_PAD, D_MODEL), (PAD_BLK, D_MODEL), 0, 0, D_FF // PAD_BLK,
                         "w_down_layout"),
        g_post_ffn=g_post_ffn,
    )


def _rope_tables(pos, dim):
    inv = ROPE_THETA ** (-jnp.arange(0, dim, 2, dtype=F32) / dim)
    ang = pos.astype(F32)[:, None] * inv[None, :]
    return jnp.cos(ang), jnp.sin(ang)


def _tables(pos, row, col):
    c1, s1 = _rope_tables(pos, MLA_ROPE)
    z = jnp.zeros((c1.shape[0], HALF_LANES - c1.shape[1]), F32)
    cos_a = jnp.concatenate([c1, z, c1, z], axis=1)
    sin_a = jnp.concatenate([-s1, z, s1, z], axis=1)
    cr, sr = _rope_tables(row, HEAD_DIM // 2)
    cc, sc = _rope_tables(col, HEAD_DIM // 2)
    cos_b = jnp.concatenate([cr, cc, cr, cc], axis=1)
    sin_b = jnp.concatenate([-sr, -sc, sr, sc], axis=1)
    return cos_a, sin_a, cos_b, sin_b


def kernel(x_prompt, x_sample, meta_tokens, g_pre_mix, w_in, g_cq, w_uq, g_ckv, w_ukv, g_qn, g_kn,
           w_pa, w_pb, w_o, g_post_mix, g_pre_ffn, w_up, w_conv, b_conv, w_down, g_post_ffn):
    w = _prep_weights(g_pre_mix, w_in, g_cq, w_uq, g_ckv, w_ukv, g_qn, g_kn, w_pa, w_pb, w_o,
                      g_post_mix, g_pre_ffn, w_up, w_conv, b_conv, w_down, g_post_ffn)
    t = jnp.arange(SEQ, dtype=jnp.int32)
    tabs_real = _tables(t + N_META, t // GRID_W, t % GRID_W)
    zero = jnp.zeros((N_META,), jnp.int32)
    tabs_meta = _tables(jnp.arange(N_META), zero, zero)
    return (_trunk(x_prompt, meta_tokens, w, tabs_real, tabs_meta, "p"),
            _trunk(x_sample, meta_tokens, w, tabs_real, tabs_meta, "s"))


def _trunk(x, meta_tokens, w, tabs_real, tabs_meta, tag):
    batch = x.shape[0]
    xr = x.reshape(batch * SEQ, D_MODEL)
    xm = jnp.broadcast_to(meta_tokens[None], (batch, N_META, D_MODEL)).reshape(batch * N_META, D_MODEL)
    mrows = batch * N_META
    tabs_meta = tuple(jnp.tile(a, (batch, 1)) for a in tabs_meta)

    z = _matmul_nt(_norm_rows(xr, w["g_pre_mix"], TM_NORM, "norm_" + tag), w["w_in"], TM_MM, TN_MM, BF16,
                   "in_proj_" + tag)
    zm = _matmul_nt(_norm_rows(xm, w["g_pre_mix"], mrows, "norm_meta_" + tag), w["w_in"], mrows, TN_MM, BF16,
                    "in_proj_meta_" + tag)
    qa, ka, vat, qb, kb, vbt = _heads(z, w, tabs_real, TM_HEADS, SEQ // TM_HEADS, "heads_" + tag)
    qam, kam, vatm, qbm, kbm, vbtm = _heads(zm, w, tabs_meta, mrows, 1, "heads_meta_" + tag)

    def per_batch(t):
        return t.reshape(t.shape[0], batch, N_META).transpose(1, 0, 2)

    def token_major(t):
        return t.transpose(0, 2, 1).reshape(batch * N_META, t.shape[1])

    vatm, vbtm = per_batch(vatm), per_batch(vbtm)

    cfg_a = dict(batch=batch, heads=MLA_HEADS, group=1, dk=MLA_QK)
    cfg_b = dict(batch=batch, heads=GQA_HEADS, group=GQA_GROUP, dk=HEAD_DIM)
    oa = _attention(qa, ka, vat, kam, vatm, lq=SEQ, tq=TQ_ATTN, tk=TK_ATTN, name="attn_a_" + tag, **cfg_a)
    ob = _attention(qb, kb, vbt, kbm, vbtm, lq=SEQ, tq=TQ_ATTN, tk=TK_ATTN, name="attn_b_" + tag, **cfg_b)
    oam = token_major(_attention(qam, ka, vat, kam, vatm, lq=N_META, tq=N_META, tk=TK_ATTN,
                                 name="attn_a_meta_" + tag, **cfg_a))
    obm = token_major(_attention(qbm, kb, vbt, kbm, vbtm, lq=N_META, tq=N_META, tk=TK_ATTN,
                                 name="attn_b_meta_" + tag, **cfg_b))

    mg = _merge(oa, ob, z, w["w_pa"], w["w_pb"], TM_MM, TN_MM, "merge_" + tag)
    mgm = _merge(oam, obm, zm, w["w_pa"], w["w_pb"], mrows, TN_MM, "merge_meta_" + tag)
    h1, hn = _proj_resident(mg, w["w_o"], xr, w["g_post_mix"], w["g_pre_ffn"], TM_OUT, "out_proj_" + tag)
    _, hnm = _proj_residual(mgm, w["w_o"], xm, w["g_post_mix"], w["g_pre_ffn"], mrows, TK_META,
                            "out_proj_meta_" + tag)

    f = _ffn_up(hn, hnm, w["w_up_a"], w["w_up_g"], w["w_conv"], w["b_conv"], TM_UP, TF_UP, "ffn_up_" + tag)
    y = _proj_residual(f, w["w_down"], h1, w["g_post_ffn"], None, TM_DOWN, TK_DOWN, "ffn_down_" + tag)
    return y.reshape(batch, SEQ, D_MODEL)
```

```python
import functools
import math

import jax
import jax.numpy as jnp
import numpy as np
from jax import lax
from jax.experimental import pallas as pl
from jax.experimental.pallas import tpu as pltpu

F32 = jnp.float32
BF16 = jnp.bfloat16

D_MODEL = 4096
SEQ = 4096
N_META = 16
GRID_W = 64
HEAD_DIM = 128
MLA_HEADS = 16
MLA_Q_LORA = 1024
MLA_KV_LORA = 512
MLA_NOPE = 128
MLA_ROPE = 64
MLA_V = 128
MLA_QK = 256
GQA_HEADS = 16
GQA_KV_HEADS = 4
GQA_GROUP = GQA_HEADS // GQA_KV_HEADS
D_FF = 11008
D_FF_PAD = 11264
ROPE_THETA = 10000.0
EPS = 1e-6
LOG2E = math.log2(math.e)

OFF_GA = 0
OFF_GB = 4096
OFF_GQ = 8192
OFF_CQ = 10240
OFF_CKV = 11264
OFF_GK = 11776
OFF_GV = 12288
OFF_KR = 12800
D_IN_PAD = 13312

LANES = 128
HALF_LANES = LANES // 2
MXU_N = 256
VMEM_LIMIT = 60 * 1024 * 1024

TM_NORM = 512
TM_MM, TN_MM = 1024, 1024
TM_HEADS = 512
TQ_ATTN, TK_ATTN = 1024, 2048
TM_OUT = 256
TM_UP, TF_UP = 1024, 512
TM_DOWN, TK_DOWN = 512, 1024
TK_META = 512


def _params(sem):
    return pltpu.CompilerParams(dimension_semantics=sem, vmem_limit_bytes=VMEM_LIMIT)


def _rms(x, g):
    return x * lax.rsqrt(jnp.mean(x * x, axis=-1, keepdims=True) + EPS) * g


def _norm_kernel(x_ref, g_ref, o_ref):
    o_ref[...] = _rms(x_ref[...], g_ref[...]).astype(o_ref.dtype)


def _norm_rows(x, g, tm, name):
    m = x.shape[0]
    return pl.pallas_call(
        _norm_kernel,
        grid=(m // tm,),
        in_specs=[pl.BlockSpec((tm, D_MODEL), lambda i: (i, 0)), pl.BlockSpec((1, D_MODEL), lambda i: (0, 0))],
        out_specs=pl.BlockSpec((tm, D_MODEL), lambda i: (i, 0)),
        out_shape=jax.ShapeDtypeStruct((m, D_MODEL), BF16),
        compiler_params=_params(("parallel",)),
        name=name,
    )(x, g)


NT_DIMS = (((1,), (1,)), ((), ()))


def _rope(x, cos, sin_signed):
    return x * cos + pltpu.roll(x, HALF_LANES, 1) * sin_signed


def _heads_kernel(cq_ref, ckv_ref, kr_ref, gq_ref, gk_ref, gv_ref, wuq_ref, wuk_ref, wuvt_ref, eye_ref,
                  gcq_ref, gckv_ref, gqn_ref, gkn_ref, cosa_ref, sina_ref, cosb_ref, sinb_ref,
                  qa_ref, ka_ref, vat_ref, qb_ref, kb_ref, vbt_ref):
    cosa, sina = cosa_ref[...], sina_ref[...]
    cosb, sinb = cosb_ref[...], sinb_ref[...]
    scale_a = LOG2E / math.sqrt(MLA_NOPE + MLA_ROPE)
    scale_b = LOG2E / math.sqrt(HEAD_DIM)

    cq = _rms(cq_ref[...].astype(F32), gcq_ref[...]).astype(BF16)
    for h in range(MLA_HEADS):
        q = jnp.dot(cq, wuq_ref[:, h * MLA_QK:(h + 1) * MLA_QK], preferred_element_type=F32)
        qa_ref[:, h * MLA_QK:h * MLA_QK + MLA_NOPE] = (q[:, :MLA_NOPE] * scale_a).astype(BF16)
        qa_ref[:, h * MLA_QK + MLA_NOPE:(h + 1) * MLA_QK] = (
            _rope(q[:, MLA_NOPE:], cosa, sina) * scale_a).astype(BF16)

    ckv = _rms(ckv_ref[...].astype(F32), gckv_ref[...]).astype(BF16)
    k_rope = _rope(kr_ref[...].astype(F32), cosa, sina).astype(BF16)
    k_nope = jnp.dot(ckv, wuk_ref[...], preferred_element_type=F32).astype(BF16)
    for h in range(MLA_HEADS):
        ka_ref[:, h * MLA_QK:h * MLA_QK + MLA_NOPE] = k_nope[:, h * MLA_NOPE:(h + 1) * MLA_NOPE]
        ka_ref[:, h * MLA_QK + MLA_NOPE:(h + 1) * MLA_QK] = k_rope
    vat_ref[...] = lax.dot_general(wuvt_ref[...], ckv, NT_DIMS, preferred_element_type=F32).astype(BF16)
    vbt_ref[...] = lax.dot_general(eye_ref[...], gv_ref[...], NT_DIMS, preferred_element_type=F32).astype(BF16)

    gqn, gkn = gqn_ref[...], gkn_ref[...]
    for h in range(GQA_HEADS):
        x = gq_ref[:, h * HEAD_DIM:(h + 1) * HEAD_DIM].astype(F32)
        qb_ref[:, h * HEAD_DIM:(h + 1) * HEAD_DIM] = (_rope(_rms(x, gqn), cosb, sinb) * scale_b).astype(BF16)
    for h in range(GQA_KV_HEADS):
        x = gk_ref[:, h * HEAD_DIM:(h + 1) * HEAD_DIM].astype(F32)
        kb_ref[:, h * HEAD_DIM:(h + 1) * HEAD_DIM] = _rope(_rms(x, gkn), cosb, sinb).astype(BF16)


def _heads(z, w, tabs, tm, tab_blocks, name):
    m = z.shape[0]

    def zspec(width, off):
        return pl.BlockSpec((tm, width), lambda i: (i, off // width))

    def full(a):
        return pl.BlockSpec(a.shape, lambda i: (0,) * a.ndim, pipeline_mode=pl.Buffered(1))

    tab_spec = pl.BlockSpec((tm, LANES), lambda i: (i % tab_blocks, 0))
    outs = ((MLA_HEADS * MLA_QK, True), (MLA_HEADS * MLA_QK, True), (MLA_HEADS * MLA_V, False),
            (GQA_HEADS * HEAD_DIM, True), (GQA_KV_HEADS * HEAD_DIM, True), (GQA_KV_HEADS * HEAD_DIM, False))
    consts = (w["w_uq"], w["w_uk"], w["w_uvt"], w["eye_kv"], w["g_cq"], w["g_ckv"], w["g_qn"], w["g_kn"])
    return pl.pallas_call(
        _heads_kernel,
        grid=(m // tm,),
        in_specs=[zspec(MLA_Q_LORA, OFF_CQ), zspec(MLA_KV_LORA, OFF_CKV), zspec(LANES, OFF_KR),
                  zspec(GQA_HEADS * HEAD_DIM, OFF_GQ), zspec(GQA_KV_HEADS * HEAD_DIM, OFF_GK),
                  zspec(GQA_KV_HEADS * HEAD_DIM, OFF_GV)]
        + [full(a) for a in consts] + [tab_spec] * 4,
        out_specs=[pl.BlockSpec((tm, n), lambda i: (i, 0)) if rows else pl.BlockSpec((n, tm), lambda i: (0, i))
                   for n, rows in outs],
        out_shape=[jax.ShapeDtypeStruct((m, n) if rows else (n, m), BF16) for n, rows in outs],
        compiler_params=_params(("parallel",)),
        name=name,
    )(z, z, z, z, z, z, *consts, *tabs)


def _attn_kernel(tk, transpose_out, q_ref, k_ref, vt_ref, km_ref, vmt_ref, o_ref):
    q = q_ref[...]
    s = lax.dot_general(km_ref[...], q, NT_DIMS, preferred_element_type=F32)
    m = jnp.max(s, axis=0, keepdims=True)
    p = jnp.exp2(s - m)
    l = jnp.sum(p, axis=0, keepdims=True)
    acc = jnp.dot(vmt_ref[...], p.astype(BF16), preferred_element_type=F32)

    def scores(c):
        return lax.dot_general(k_ref[c * tk:(c + 1) * tk, :], q, NT_DIMS, preferred_element_type=F32)

    n_chunks = k_ref.shape[0] // tk
    s_next = scores(0)
    for c in range(n_chunks):
        s = s_next
        if c + 1 < n_chunks:
            s_next = scores(c + 1)
        m_new = jnp.maximum(m, jnp.max(s, axis=0, keepdims=True))
        alpha = jnp.exp2(m - m_new)
        p = jnp.exp2(s - m_new)
        l = alpha * l + jnp.sum(p, axis=0, keepdims=True)
        acc = alpha * acc + jnp.dot(vt_ref[:, c * tk:(c + 1) * tk], p.astype(BF16), preferred_element_type=F32)
        m = m_new
    o = acc / l
    o_ref[...] = (o.T if transpose_out else o).astype(o_ref.dtype)


def _attention(q, k, vt, km, vmt, *, batch, heads, group, dk, lq, tq, tk, name):
    nq = lq // tq
    dv = HEAD_DIM
    transpose_out = tq % LANES == 0
    if transpose_out:
        out_spec = pl.BlockSpec((tq, dv), lambda b, h, i: (b * nq + i, h))
        out_shape = jax.ShapeDtypeStruct((batch * lq, heads * dv), BF16)
    else:
        out_spec = pl.BlockSpec((None, dv, tq), lambda b, h, i: (b, h, i))
        out_shape = jax.ShapeDtypeStruct((batch, heads * dv, lq), BF16)
    return pl.pallas_call(
        functools.partial(_attn_kernel, tk, transpose_out),
        grid=(batch, heads, nq),
        in_specs=[
            pl.BlockSpec((tq, dk), lambda b, h, i: (b * nq + i, h)),
            pl.BlockSpec((SEQ, dk), lambda b, h, i: (b, h // group)),
            pl.BlockSpec((dv, SEQ), lambda b, h, i: (h // group, b)),
            pl.BlockSpec((N_META, dk), lambda b, h, i: (b, h // group)),
            pl.BlockSpec((None, dv, N_META), lambda b, h, i: (b, h // group, 0)),
        ],
        out_specs=out_spec,
        out_shape=out_shape,
        compiler_params=_params(("parallel", "parallel", "parallel")),
        name=name,
    )(q, k, vt, km, vmt)


def _merge_kernel(oa_ref, ob_ref, ga_ref, gb_ref, wpa_ref, wpb_ref, o_ref):
    a = jnp.dot(oa_ref[...], wpa_ref[...], preferred_element_type=F32)
    b = jnp.dot(ob_ref[...], wpb_ref[...], preferred_element_type=F32)
    ga = jax.nn.sigmoid(ga_ref[...].astype(F32))
    gb = jax.nn.sigmoid(gb_ref[...].astype(F32))
    o_ref[...] = (ga * a + gb * b).astype(o_ref.dtype)


def _merge(oa, ob, z, wpa, wpb, tm, tn, name):
    m, k = oa.shape
    return pl.pallas_call(
        _merge_kernel,
        grid=(m // tm, D_MODEL // tn),
        in_specs=[
            pl.BlockSpec((tm, k), lambda i, j: (i, 0)),
            pl.BlockSpec((tm, k), lambda i, j: (i, 0)),
            pl.BlockSpec((tm, tn), lambda i, j: (i, OFF_GA // tn + j)),
            pl.BlockSpec((tm, tn), lambda i, j: (i, OFF_GB // tn + j)),
            pl.BlockSpec((k, tn), lambda i, j: (0, j)),
            pl.BlockSpec((k, tn), lambda i, j: (0, j)),
        ],
        out_specs=pl.BlockSpec((tm, tn), lambda i, j: (i, j)),
        out_shape=jax.ShapeDtypeStruct((m, D_MODEL), BF16),
        compiler_params=_params(("parallel", "parallel")),
        name=name,
    )(oa, ob, z, z, wpa, wpb)


PROJ_CHUNK = 512
NORM_ROWS = 128
NORM_PIECE = 16


def _inv_rms(x):
    return lax.rsqrt(jnp.mean(x * x, axis=-1, keepdims=True) + EPS)


def _norm_residual_items(res_ref, g_ref, g2_ref, h_ref, hn_ref, row_lo, row_hi):
    items = []
    for r0 in range(row_lo, row_hi, NORM_ROWS):
        blk = min(NORM_ROWS, row_hi - r0)
        pieces = [(r0 + s0, min(NORM_PIECE, blk - s0), s0) for s0 in range(0, blk, NORM_PIECE)]
        inv = {}

        def stats(key, r0=r0, blk=blk, inv=inv):
            inv[key] = _inv_rms(h_ref[r0:r0 + blk, :])

        def scale(lo, sz, s0, inv=inv):
            r = inv["h"][s0:s0 + sz]
            h_ref[lo:lo + sz, :] = res_ref[lo:lo + sz, :] + h_ref[lo:lo + sz, :] * r * g_ref[...]

        def scale_next(lo, sz, s0, inv=inv):
            r = inv["hn"][s0:s0 + sz]
            hn_ref[lo:lo + sz, :] = (h_ref[lo:lo + sz, :] * r * g2_ref[...]).astype(hn_ref.dtype)

        items.append(functools.partial(stats, "h"))
        items += [functools.partial(scale, *p) for p in pieces]
        if hn_ref is not None:
            items.append(functools.partial(stats, "hn"))
            items += [functools.partial(scale_next, *p) for p in pieces]
    return items


def _alternate(leaders, followers):
    n = len(leaders)
    for i, lead in enumerate(leaders):
        lead()
        for f in followers[i * len(followers) // n:(i + 1) * len(followers) // n]:
            f()


def _project_and_normalise(dot_into, res_ref, g_ref, g2_ref, h_ref, hn_ref):
    tm = h_ref.shape[0]
    half = tm // 2 if tm % (2 * NORM_ROWS) == 0 else tm
    cols = range(0, D_MODEL, PROJ_CHUNK)
    norm = functools.partial(_norm_residual_items, res_ref, g_ref, g2_ref, h_ref, hn_ref)
    for c in cols:
        dot_into(slice(0, half), c)
    if half < tm:
        _alternate([functools.partial(dot_into, slice(half, tm), c) for c in cols], norm(0, half))
    for item in norm(half if half < tm else 0, tm):
        item()


def _proj_residual_kernel(nk, a_ref, w_ref, res_ref, g_ref, g2_ref, h_ref, hn_ref):
    kk = pl.program_id(1)

    def dot_into(first, rows, c):
        part = jnp.dot(a_ref[rows, :], w_ref[:, c:c + PROJ_CHUNK], preferred_element_type=F32)
        if first:
            h_ref[rows, c:c + PROJ_CHUNK] = part
        else:
            h_ref[rows, c:c + PROJ_CHUNK] += part

    def accumulate(first):
        for c in range(0, D_MODEL, PROJ_CHUNK):
            dot_into(first, slice(None), c)

    def finish(first):
        _project_and_normalise(functools.partial(dot_into, first), res_ref, g_ref, g2_ref, h_ref, hn_ref)

    if nk == 1:
        finish(True)
    else:
        pl.when(kk == 0)(functools.partial(accumulate, True))
        pl.when(jnp.logical_and(kk > 0, kk < nk - 1))(functools.partial(accumulate, False))
        pl.when(kk == nk - 1)(functools.partial(finish, False))


def _proj_residual_nohn_kernel(nk, a_ref, w_ref, res_ref, g_ref, h_ref):
    _proj_residual_kernel(nk, a_ref, w_ref, res_ref, g_ref, None, h_ref, None)


def _proj_residual(a, w, res, g, g2, tm, tk, name):
    m, k = a.shape
    nk = k // tk
    gspec = pl.BlockSpec((1, D_MODEL), lambda i, kk: (0, 0))
    row_spec = pl.BlockSpec((tm, D_MODEL), lambda i, kk: (i, 0))
    in_specs = [pl.BlockSpec((tm, tk), lambda i, kk: (i, kk)),
                pl.BlockSpec((tk, D_MODEL), lambda i, kk: (kk, 0)), row_spec, gspec]
    if g2 is None:
        return pl.pallas_call(
            functools.partial(_proj_residual_nohn_kernel, nk),
            grid=(m // tm, nk), in_specs=in_specs, out_specs=row_spec,
            out_shape=jax.ShapeDtypeStruct((m, D_MODEL), F32),
            compiler_params=_params(("parallel", "arbitrary")), name=name,
        )(a, w, res, g)
    return pl.pallas_call(
        functools.partial(_proj_residual_kernel, nk),
        grid=(m // tm, nk), in_specs=in_specs + [gspec], out_specs=[row_spec, row_spec],
        out_shape=[jax.ShapeDtypeStruct((m, D_MODEL), F32), jax.ShapeDtypeStruct((m, D_MODEL), BF16)],
        compiler_params=_params(("parallel", "arbitrary")), name=name,
    )(a, w, res, g, g2)


def _proj_resident_kernel(a_ref, w_ref, res_ref, g_ref, g2_ref, h_ref, hn_ref):
    def dot_into(rows, c):
        h_ref[rows, c:c + PROJ_CHUNK] = jnp.dot(a_ref[rows, :], w_ref[:, c:c + PROJ_CHUNK],
                                                preferred_element_type=F32)

    _project_and_normalise(dot_into, res_ref, g_ref, g2_ref, h_ref, hn_ref)


def _proj_resident(a, w, res, g, g2, tm, name):
    m, k = a.shape
    gspec = pl.BlockSpec((1, D_MODEL), lambda i: (0, 0))
    row_spec = pl.BlockSpec((tm, D_MODEL), lambda i: (i, 0))
    return pl.pallas_call(
        _proj_resident_kernel,
        grid=(m // tm,),
        in_specs=[pl.BlockSpec((tm, k), lambda i: (i, 0)),
                  pl.BlockSpec((k, D_MODEL), lambda i: (0, 0), pipeline_mode=pl.Buffered(1)),
                  row_spec, gspec, gspec],
        out_specs=[row_spec, row_spec],
        out_shape=[jax.ShapeDtypeStruct((m, D_MODEL), F32), jax.ShapeDtypeStruct((m, D_MODEL), BF16)],
        compiler_params=_params(("parallel",)), name=name,
    )(a, w, res, g, g2)


HALO = N_META
FFN_ROW_BLOCKS = 4


GELU_K1 = -2.0 * math.sqrt(2.0 / math.pi) * LOG2E
GELU_K2 = GELU_K1 * 0.044715


def _gelu_tanh(x):
    return x / (1.0 + jnp.exp2(x * (x * x * GELU_K2 + GELU_K1)))


def _ffn_up_kernel(tm, tiles_per_seq, hn_ref, prev_ref, next_ref, hnm_ref, wa_ref, wg_ref, wc_ref, bc_ref,
                   f_ref, ext_ref):
    i = pl.program_id(0)

    @pl.when(pl.program_id(1) == 0)
    def _():
        first = i % tiles_per_seq == 0
        last = i % tiles_per_seq == tiles_per_seq - 1
        ext_ref[0:HALO, :] = jnp.where(first, hnm_ref[...], prev_ref[...])
        ext_ref[HALO:HALO + tm, :] = hn_ref[...]
        ext_ref[HALO + tm:, :] = jnp.where(last, jnp.zeros_like(next_ref), next_ref[...])

    wc = wc_ref[...]
    bc = bc_ref[...]
    n_chunks = f_ref.shape[1] // MXU_N
    rows_g = tm // FFN_ROW_BLOCKS
    rows_e = tm // (2 * FFN_ROW_BLOCKS)
    pad = 8

    def a_block(c, r):
        lo = 0 if r == 0 else HALO + r * rows_g
        hi = tm + 2 * HALO if r == FFN_ROW_BLOCKS - 1 else HALO + (r + 1) * rows_g
        return jnp.dot(ext_ref[lo:hi, :], wa_ref[:, c * MXU_N:(c + 1) * MXU_N], preferred_element_type=F32)

    def g_block(c, r):
        return jnp.dot(ext_ref[HALO + r * rows_g:HALO + (r + 1) * rows_g, :],
                       wg_ref[:, c * MXU_N:(c + 1) * MXU_N], preferred_element_type=F32)

    def epilogue_piece(c, a, gate, k):
        cols = slice(c * MXU_N, (c + 1) * MXU_N)
        lo = HALO + k * rows_e
        win = a[lo - pad:lo + rows_e + pad]
        prev = pltpu.roll(win, 1, 0)[pad:pad + rows_e]
        nxt = pltpu.roll(win, win.shape[0] - 1, 0)[pad:pad + rows_e]
        conv = prev * wc[0:1, cols] + win[pad:pad + rows_e] * wc[1:2, cols] + nxt * wc[2:3, cols] + bc[:, cols]
        g = gate[k * rows_e:(k + 1) * rows_e]
        f_ref[k * rows_e:(k + 1) * rows_e, cols] = (_gelu_tanh(conv) * g).astype(f_ref.dtype)

    a_next = jnp.concatenate([a_block(0, r) for r in range(FFN_ROW_BLOCKS)], axis=0)
    g_next = jnp.concatenate([g_block(0, r) for r in range(FFN_ROW_BLOCKS)], axis=0)
    for c in range(n_chunks):
        a_cur, g_cur = a_next, g_next
        a_parts, g_parts = [], []
        for k in range(2 * FFN_ROW_BLOCKS):
            if c + 1 < n_chunks:
                if k < FFN_ROW_BLOCKS:
                    a_parts.append(a_block(c + 1, k))
                else:
                    g_parts.append(g_block(c + 1, k - FFN_ROW_BLOCKS))
            epilogue_piece(c, a_cur, g_cur, k)
        if c + 1 < n_chunks:
            a_next = jnp.concatenate(a_parts, axis=0)
            g_next = jnp.concatenate(g_parts, axis=0)


def _ffn_up(hn, hn_meta, wa, wg, wc, bc, tm, tf, name):
    m = hn.shape[0]
    tiles_per_seq = SEQ // tm
    hb = tm // HALO
    last = m // HALO - 1
    return pl.pallas_call(
        functools.partial(_ffn_up_kernel, tm, tiles_per_seq),
        grid=(m // tm, D_FF_PAD // tf),
        in_specs=[
            pl.BlockSpec((tm, D_MODEL), lambda i, j: (i, 0)),
            pl.BlockSpec((HALO, D_MODEL), lambda i, j: (jnp.maximum(i * hb - 1, 0), 0)),
            pl.BlockSpec((HALO, D_MODEL), lambda i, j: (jnp.minimum((i + 1) * hb, last), 0)),
            pl.BlockSpec((N_META, D_MODEL), lambda i, j: (i // tiles_per_seq, 0)),
            pl.BlockSpec((D_MODEL, tf), lambda i, j: (0, j)),
            pl.BlockSpec((D_MODEL, tf), lambda i, j: (0, j)),
            pl.BlockSpec((3, tf), lambda i, j: (0, j)),
            pl.BlockSpec((1, tf), lambda i, j: (0, j)),
        ],
        out_specs=pl.BlockSpec((tm, tf), lambda i, j: (i, j)),
        out_shape=jax.ShapeDtypeStruct((m, D_FF_PAD), BF16),
        scratch_shapes=[pltpu.VMEM((tm + 2 * HALO, D_MODEL), BF16)],
        compiler_params=_params(("parallel", "arbitrary")),
        name=name,
    )(hn, hn, hn, hn_meta, wa, wg, wc, bc)


SRC_CQ, SRC_CKV, SRC_KR = 0, MLA_Q_LORA, MLA_Q_LORA + MLA_KV_LORA
SRC_GQ = SRC_KR + MLA_ROPE
SRC_GK = SRC_GQ + GQA_HEADS * HEAD_DIM
SRC_GV = SRC_GK + GQA_KV_HEADS * HEAD_DIM
SRC_GA = SRC_GV + GQA_KV_HEADS * HEAD_DIM
SRC_GB = SRC_GA + D_MODEL
ROW_BLK = HEAD_DIM // 4
ROW_BLKS_PER_STEP = 4


def _w_in_block_table():
    table = np.full(D_IN_PAD // ROW_BLK, -1, np.int32)

    def fill(dst, src, n_blocks, order=None):
        for b in range(n_blocks):
            s = b if order is None else (b // len(order)) * len(order) + order[b % len(order)]
            if s >= 0:
                table[dst // ROW_BLK + b] = src // ROW_BLK + s

    for dst, src, width in ((OFF_GA, SRC_GA, D_MODEL), (OFF_GB, SRC_GB, D_MODEL), (OFF_CQ, SRC_CQ, MLA_Q_LORA),
                            (OFF_CKV, SRC_CKV, MLA_KV_LORA), (OFF_GV, SRC_GV, GQA_KV_HEADS * HEAD_DIM)):
        fill(dst, src, width // ROW_BLK)
    fill(OFF_GQ, SRC_GQ, GQA_HEADS * HEAD_DIM // ROW_BLK, order=(0, 2, 1, 3))
    fill(OFF_GK, SRC_GK, GQA_KV_HEADS * HEAD_DIM // ROW_BLK, order=(0, 2, 1, 3))
    table[OFF_KR // ROW_BLK] = SRC_KR // ROW_BLK
    table[OFF_KR // ROW_BLK + 2] = SRC_KR // ROW_BLK + 1
    return table


def _w_in_rows_kernel(tbl_ref, *refs):
    *x_refs, o_ref = refs
    i = pl.program_id(0)
    for q, x_ref in enumerate(x_refs):
        keep = tbl_ref[i * ROW_BLKS_PER_STEP + q] >= 0
        o_ref[q * ROW_BLK:(q + 1) * ROW_BLK, :] = jnp.where(keep, x_ref[...], 0.0).astype(o_ref.dtype)


def _w_in_rows_layout(wt):
    table = jnp.asarray(_w_in_block_table())

    def in_spec(q):
        return pl.BlockSpec((ROW_BLK, D_MODEL),
                            lambda i, tbl: (jnp.maximum(tbl[i * ROW_BLKS_PER_STEP + q], 0), 0))

    rows = ROW_BLK * ROW_BLKS_PER_STEP
    return pl.pallas_call(
        _w_in_rows_kernel,
        grid_spec=pltpu.PrefetchScalarGridSpec(
            num_scalar_prefetch=1,
            grid=(D_IN_PAD // rows,),
            in_specs=[in_spec(q) for q in range(ROW_BLKS_PER_STEP)],
            out_specs=pl.BlockSpec((rows, D_MODEL), lambda i, tbl: (i, 0)),
        ),
        out_shape=jax.ShapeDtypeStruct((D_IN_PAD, D_MODEL), BF16),
        compiler_params=_params(("arbitrary",)),
        name="w_in_layout",
    )(table, *([wt] * ROW_BLKS_PER_STEP))


PAD_BLK = 256
assert D_FF % PAD_BLK == 0 and D_FF_PAD == D_FF + PAD_BLK


def _cast_pad_kernel(axis, n_valid, x_ref, o_ref):
    o_ref[...] = jnp.where(pl.program_id(axis) < n_valid, x_ref[...], 0.0).astype(o_ref.dtype)


def _cast_pad(x, out_shape, block, axis, first_block, n_valid, name):
    grid = tuple(s // b for s, b in zip(out_shape, block))

    def in_index(*ids):
        ids = list(ids)
        ids[axis] = first_block + jnp.minimum(ids[axis], n_valid - 1)
        return tuple(ids)

    return pl.pallas_call(
        functools.partial(_cast_pad_kernel, axis, n_valid),
        grid=grid,
        in_specs=[pl.BlockSpec(block, in_index)],
        out_specs=pl.BlockSpec(block, lambda *ids: ids),
        out_shape=jax.ShapeDtypeStruct(out_shape, BF16),
        compiler_params=_params(("parallel",) * len(grid)),
        name=name,
    )(x)


def _mm_nt_kernel(a_ref, bt_ref, o_ref):
    o_ref[...] = lax.dot_general(a_ref[...], bt_ref[...], NT_DIMS, preferred_element_type=F32).astype(o_ref.dtype)


def _matmul_nt(a, bt, tm, tn, out_dtype, name):
    m, k = a.shape
    n = bt.shape[0]
    return pl.pallas_call(
        _mm_nt_kernel,
        grid=(m // tm, n // tn),
        in_specs=[pl.BlockSpec((tm, k), lambda i, j: (i, 0)), pl.BlockSpec((tn, k), lambda i, j: (j, 0))],
        out_specs=pl.BlockSpec((tm, tn), lambda i, j: (i, j)),
        out_shape=jax.ShapeDtypeStruct((m, n), out_dtype),
        compiler_params=_params(("parallel", "parallel")),
        name=name,
    )(a, bt)


def _prep_weights(g_pre_mix, w_in, g_cq, w_uq, g_ckv, w_ukv, g_qn, g_kn, w_pa, w_pb, w_o,
                  g_post_mix, g_pre_ffn, w_up, w_conv, b_conv, w_down, g_post_ffn):
    qtr = HEAD_DIM // 4
    perm_b = np.arange(HEAD_DIM).reshape(2, 2, qtr).transpose(1, 0, 2).reshape(-1)
    half = MLA_ROPE // 2
    w_in_p = _w_in_rows_layout(w_in[0].T)

    uq = w_uq[0].reshape(MLA_Q_LORA, MLA_HEADS, MLA_NOPE + MLA_ROPE)
    zq = jnp.zeros((MLA_Q_LORA, MLA_HEADS, HALF_LANES - half), uq.dtype)
    uq = jnp.concatenate([uq[:, :, :MLA_NOPE], uq[:, :, MLA_NOPE:MLA_NOPE + half], zq,
                          uq[:, :, MLA_NOPE + half:], zq], axis=2).reshape(MLA_Q_LORA, -1).astype(BF16)
    ukv = w_ukv[0].reshape(MLA_KV_LORA, MLA_HEADS, MLA_NOPE + MLA_V)
    uk = ukv[:, :, :MLA_NOPE].reshape(MLA_KV_LORA, -1).astype(BF16)
    uv = ukv[:, :, MLA_NOPE:].reshape(MLA_KV_LORA, -1).astype(BF16)

    fpad = D_FF_PAD - D_FF
    up = w_up[0]
    return dict(
        g_pre_mix=g_pre_mix, w_in=w_in_p, g_cq=g_cq, w_uq=uq, g_ckv=g_ckv, w_uk=uk, w_uvt=uv.T,
        eye_kv=jnp.eye(GQA_KV_HEADS * HEAD_DIM, dtype=BF16),
        g_qn=g_qn[:, perm_b], g_kn=g_kn[:, perm_b],
        w_pa=w_pa[0].astype(BF16), w_pb=w_pb[0].astype(BF16), w_o=w_o[0].astype(BF16),
        g_post_mix=g_post_mix, g_pre_ffn=g_pre_ffn,
        w_up_a=jnp.pad(up[:, :D_FF], ((0, 0), (0, fpad))).astype(BF16),
        w_up_g=_cast_pad(up, (D_MODEL, D_FF_PAD), (D_MODEL // 2, PAD_BLK), 1, D_FF // PAD_BLK, D_FF // PAD_BLK,
                         "w_up_gate_layout"),
        w_conv=jnp.pad(w_conv[0], ((0, 0), (0, fpad))), b_conv=jnp.pad(b_conv, ((0, 0), (0, fpad))),
        w_down=_cast_pad(w_down[0], (D_FF_PAD, D_MODEL), (PAD_BLK, D_MODEL), 0, 0, D_FF // PAD_BLK,
                         "w_down_layout"),
        g_post_ffn=g_post_ffn,
    )


def _rope_tables(pos, dim):
    inv = ROPE_THETA ** (-jnp.arange(0, dim, 2, dtype=F32) / dim)
    ang = pos.astype(F32)[:, None] * inv[None, :]
    return jnp.cos(ang), jnp.sin(ang)


def _tables(pos, row, col):
    c1, s1 = _rope_tables(pos, MLA_ROPE)
    z = jnp.zeros((c1.shape[0], HALF_LANES - c1.shape[1]), F32)
    cos_a = jnp.concatenate([c1, z, c1, z], axis=1)
    sin_a = jnp.concatenate([-s1, z, s1, z], axis=1)
    cr, sr = _rope_tables(row, HEAD_DIM // 2)
    cc, sc = _rope_tables(col, HEAD_DIM // 2)
    cos_b = jnp.concatenate([cr, cc, cr, cc], axis=1)
    sin_b = jnp.concatenate([-sr, -sc, sr, sc], axis=1)
    return cos_a, sin_a, cos_b, sin_b


def kernel(x_prompt, x_sample, meta_tokens, g_pre_mix, w_in, g_cq, w_uq, g_ckv, w_ukv, g_qn, g_kn,
           w_pa, w_pb, w_o, g_post_mix, g_pre_ffn, w_up, w_conv, b_conv, w_down, g_post_ffn):
    w = _prep_weights(g_pre_mix, w_in, g_cq, w_uq, g_ckv, w_ukv, g_qn, g_kn, w_pa, w_pb, w_o,
                      g_post_mix, g_pre_ffn, w_up, w_conv, b_conv, w_down, g_post_ffn)
    t = jnp.arange(SEQ, dtype=jnp.int32)
    tabs_real = _tables(t + N_META, t // GRID_W, t % GRID_W)
    zero = jnp.zeros((N_META,), jnp.int32)
    tabs_meta = _tables(jnp.arange(N_META), zero, zero)
    return (_trunk(x_prompt, meta_tokens, w, tabs_real, tabs_meta, "p"),
            _trunk(x_sample, meta_tokens, w, tabs_real, tabs_meta, "s"))


def _trunk(x, meta_tokens, w, tabs_real, tabs_meta, tag):
    batch = x.shape[0]
    xr = x.reshape(batch * SEQ, D_MODEL)
    xm = jnp.broadcast_to(meta_tokens[None], (batch, N_META, D_MODEL)).reshape(batch * N_META, D_MODEL)
    mrows = batch * N_META
    tabs_meta = tuple(jnp.tile(a, (batch, 1)) for a in tabs_meta)

    z = _matmul_nt(_norm_rows(xr, w["g_pre_mix"], TM_NORM, "norm_" + tag), w["w_in"], TM_MM, TN_MM, BF16,
                   "in_proj_" + tag)
    zm = _matmul_nt(_norm_rows(xm, w["g_pre_mix"], mrows, "norm_meta_" + tag), w["w_in"], mrows, TN_MM, BF16,
                    "in_proj_meta_" + tag)
    qa, ka, vat, qb, kb, vbt = _heads(z, w, tabs_real, TM_HEADS, SEQ // TM_HEADS, "heads_" + tag)
    qam, kam, vatm, qbm, kbm, vbtm = _heads(zm, w, tabs_meta, mrows, 1, "heads_meta_" + tag)

    def per_batch(t):
        return t.reshape(t.shape[0], batch, N_META).transpose(1, 0, 2)

    def token_major(t):
        return t.transpose(0, 2, 1).reshape(batch * N_META, t.shape[1])

    vatm, vbtm = per_batch(vatm), per_batch(vbtm)

    cfg_a = dict(batch=batch, heads=MLA_HEADS, group=1, dk=MLA_QK)
    cfg_b = dict(batch=batch, heads=GQA_HEADS, group=GQA_GROUP, dk=HEAD_DIM)
    oa = _attention(qa, ka, vat, kam, vatm, lq=SEQ, tq=TQ_ATTN, tk=TK_ATTN, name="attn_a_" + tag, **cfg_a)
    ob = _attention(qb, kb, vbt, kbm, vbtm, lq=SEQ, tq=TQ_ATTN, tk=TK_ATTN, name="attn_b_" + tag, **cfg_b)
    oam = token_major(_attention(qam, ka, vat, kam, vatm, lq=N_META, tq=N_META, tk=TK_ATTN,
                                 name="attn_a_meta_" + tag, **cfg_a))
    obm = token_major(_attention(qbm, kb, vbt, kbm, vbtm, lq=N_META, tq=N_META, tk=TK_ATTN,
                                 name="attn_b_meta_" + tag, **cfg_b))

    mg = _merge(oa, ob, z, w["w_pa"], w["w_pb"], TM_MM, TN_MM, "merge_" + tag)
    mgm = _merge(oam, obm, zm, w["w_pa"], w["w_pb"], mrows, TN_MM, "merge_meta_" + tag)
    h1, hn = _proj_resident(mg, w["w_o"], xr, w["g_post_mix"], w["g_pre_ffn"], TM_OUT, "out_proj_" + tag)
    _, hnm = _proj_residual(mgm, w["w_o"], xm, w["g_post_mix"], w["g_pre_ffn"], mrows, TK_META,
                            "out_proj_meta_" + tag)

    f = _ffn_up(hn, hnm, w["w_up_a"], w["w_up_g"], w["w_conv"], w["b_conv"], TM_UP, TF_UP, "ffn_up_" + tag)
    y = _proj_residual(f, w["w_down"], h1, w["g_post_ffn"], None, TM_DOWN, TK_DOWN, "ffn_down_" + tag)
    return y.reshape(batch, SEQ, D_MODEL)
```
